```python
import jax, jax.numpy as jnp
from jax import lax
import numpy as np

D_MODEL = 2048
BATCH = 4
SEQ = 2048
DEPTH = 1
DEC_BATCH = 128
DEC_SEQ = 4
PAST_LEN = 16384
PAGE_SIZE = 128

N_META = 16
D_A = 1024
CONV_A = 31
D_B = 2048
HEAD_DIM = 64
N_HEADS_B = D_B // HEAD_DIM
N_GROUPS = 8
D_STATE = 128
CONV_B = 4
CHUNK = 128
D_FF = 5632
CONV_F = 3
EPS = 1e-6
D_XBC = D_B + 2 * N_GROUPS * D_STATE
D_IN = 2 * D_A + D_B + D_XBC + N_HEADS_B + 2 * D_MODEL
SPLITS = (D_A, 2 * D_A, 2 * D_A + D_B, 2 * D_A + D_B + D_XBC, 2 * D_A + D_B + D_XBC + N_HEADS_B)

kernel_name = 'hybrid_conformer_ssd_convffn_step'

F32 = jnp.float32


def rmsnorm(x, g):
    xf = x.astype(F32)
    r = lax.rsqrt(jnp.mean(xf * xf, axis=-1, keepdims=True) + EPS)
    return (xf * r * g.astype(F32)).astype(x.dtype)


def layernorm(x, g, b):
    xf = x.astype(F32)
    mu = jnp.mean(xf, axis=-1, keepdims=True)
    xc = xf - mu
    r = lax.rsqrt(jnp.mean(xc * xc, axis=-1, keepdims=True) + EPS)
    return (xc * r * g.astype(F32) + b.astype(F32)).astype(x.dtype)


def causal_dwconv(u, past, w, b):
    W = w.shape[0]
    cat = jnp.concatenate([past.astype(u.dtype), u], axis=1)
    out = lax.conv_general_dilated(cat, w.astype(u.dtype)[:, None, :], window_strides=(1,), padding='VALID',
                                   dimension_numbers=('NWC', 'WIO', 'NWC'), feature_group_count=u.shape[-1])
    return out + b.astype(u.dtype), cat[:, cat.shape[1] - (W - 1):]


def ssd(xh, dt, a, Bm, Cm, h0, chunk):
    b, L, H, P = xh.shape
    G, N = Bm.shape[2], Bm.shape[3]
    R = H // G
    nc = L // chunk
    x = xh.astype(F32).reshape(b, nc, chunk, G, R, P)
    dtc = dt.astype(F32).reshape(b, nc, chunk, G, R)
    Bc = Bm.astype(F32).reshape(b, nc, chunk, G, N)
    Cc = Cm.astype(F32).reshape(b, nc, chunk, G, N)
    cum = jnp.cumsum(dtc * a.astype(F32).reshape(G, R), axis=2)
    xdt = x * dtc[..., None]
    seg = cum[:, :, :, None] - cum[:, :, None, :]
    causal = jnp.tril(jnp.ones((chunk, chunk), bool))[None, None, :, :, None, None]
    decay = jnp.exp(jnp.where(causal, seg, -jnp.inf))
    cb = jnp.einsum('bcqgn,bckgn->bcqkg', Cc, Bc)
    y_intra = jnp.einsum('bcqkgr,bckgrp->bcqgrp', cb[..., None] * decay, xdt)
    to_end = jnp.exp(cum[:, :, -1:] - cum)
    states = jnp.einsum('bckgn,bckgrp->bcgrpn', Bc, xdt * to_end[..., None])
    chunk_decay = jnp.exp(cum[:, :, -1])

    def step(h, inp):
        s, d = inp
        return d[..., None, None] * h + s, h

    hT, h_prev = lax.scan(step, h0.astype(F32).reshape(b, G, R, P, N),
                          (jnp.moveaxis(states, 1, 0), jnp.moveaxis(chunk_decay, 1, 0)))
    h_prev = jnp.moveaxis(h_prev, 0, 1)
    y_inter = jnp.einsum('bcqgn,bcgrpn->bcqgrp', Cc, h_prev) * jnp.exp(cum)[..., None]
    return (y_intra + y_inter).reshape(b, L, H, P), hT.reshape(b, H, P, N)


def hybrid_layer(x, conf_buf, xbc_buf, ssm_h, ffn_buf, lead_pad, chunk,
                 g_pre1, g_post1, w_in, b_gate, w_dw_a, b_dw_a, g_ln_a, b_ln_a, w_a_out, b_a_out,
                 w_dw_b, b_dw_b, dt_bias, a_log, d_skip, g_norm_b, w_b_out, w_o,
                 g_pre2, g_post2, w_up, w_dw_f, b_dw_f, w_down):
    b, L, _ = x.shape
    h = rmsnorm(x, g_pre1)
    proj = h @ w_in
    a_val, a_gate, z, xbc, dt_raw, gate_raw = jnp.split(proj, SPLITS, axis=-1)
    u = a_val * jax.nn.sigmoid(a_gate)
    u, conf_new = causal_dwconv(u, conf_buf, w_dw_a, b_dw_a)
    u = jax.nn.silu(layernorm(u, g_ln_a, b_ln_a))
    y_a = u @ w_a_out + b_a_out
    xbc, xbc_new = causal_dwconv(xbc, xbc_buf, w_dw_b, b_dw_b)
    xbc = jax.nn.silu(xbc)
    xs, Bm, Cm = jnp.split(xbc, (D_B, D_B + N_GROUPS * D_STATE), axis=-1)
    xh = xs.reshape(b, L, N_HEADS_B, HEAD_DIM)
    Bm = Bm.reshape(b, L, N_GROUPS, D_STATE)
    Cm = Cm.reshape(b, L, N_GROUPS, D_STATE)
    dt = jax.nn.softplus(dt_raw.astype(F32) + dt_bias.astype(F32))
    a = -jnp.exp(a_log.astype(F32))
    pad = ((0, 0), (lead_pad, 0), (0, 0), (0, 0))
    y, ssm_new = ssd(jnp.pad(xh, pad), jnp.pad(dt, pad[:3]), a, jnp.pad(Bm, pad), jnp.pad(Cm, pad), ssm_h, chunk)
    y = y[:, lead_pad:] + d_skip.astype(F32)[:, None] * xh.astype(F32)
    y = y.reshape(b, L, D_B).astype(x.dtype)
    y_b = rmsnorm(y * jax.nn.silu(z), g_norm_b) @ w_b_out
    g_a, g_b = jnp.split(jax.nn.sigmoid(gate_raw + b_gate), 2, axis=-1)
    m = (g_a * y_a + g_b * y_b) @ w_o
    x = x + rmsnorm(m, g_post1)
    h = rmsnorm(x, g_pre2)
    u = h @ w_up
    u, ffn_new = causal_dwconv(u, ffn_buf, w_dw_f, b_dw_f)
    gt, val = jnp.split(u, 2, axis=-1)
    f = (jax.nn.gelu(gt) * val) @ w_down
    x = x + rmsnorm(f, g_post2)
    return x, conf_new, xbc_new, ssm_new, ffn_new


def setup_inputs(seed: int = 0) -> dict:
    key = jax.random.key(seed)
    ks = iter(jax.random.split(key, 40))
    nrm = lambda shape, s: jax.random.normal(next(ks), shape, F32) * s
    gain = lambda shape: 1.0 + nrm(shape, 0.02)
    dt0 = jnp.exp(jax.random.uniform(next(ks), (DEPTH, N_HEADS_B), F32, np.log(1e-3), np.log(1e-1)))
    return {
        'x_prompt': nrm((BATCH, SEQ, D_MODEL), 1.0),
        'x_sample': nrm((DEC_BATCH, DEC_SEQ, D_MODEL), 1.0),
        'state_conv_a': nrm((DEPTH, DEC_BATCH, CONV_A - 1, D_A), 1.0),
        'state_conv_b': nrm((DEPTH, DEC_BATCH, CONV_B - 1, D_XBC), 1.0),
        'state_ssm': nrm((DEPTH, DEC_BATCH, N_HEADS_B, HEAD_DIM, D_STATE), 0.1),
        'state_conv_ffn': nrm((DEPTH, DEC_BATCH, CONV_F - 1, 2 * D_FF), 1.0),
        'meta_tokens': nrm((N_META, D_MODEL), 1.0),
        'g_pre1': gain((DEPTH, D_MODEL)),
        'g_post1': gain((DEPTH, D_MODEL)),
        'w_in': nrm((DEPTH, D_MODEL, D_IN), D_MODEL ** -0.5),
        'b_gate': nrm((DEPTH, 2 * D_MODEL), 0.1),
        'w_dw_a': nrm((DEPTH, CONV_A, D_A), CONV_A ** -0.5),
        'b_dw_a': nrm((DEPTH, D_A), 0.02),
        'g_ln_a': gain((DEPTH, D_A)),
        'b_ln_a': nrm((DEPTH, D_A), 0.02),
        'w_a_out': nrm((DEPTH, D_A, D_MODEL), D_A ** -0.5),
        'b_a_out': nrm((DEPTH, D_MODEL), 0.02),
        'w_dw_b': nrm((DEPTH, CONV_B, D_XBC), CONV_B ** -0.5),
        'b_dw_b': nrm((DEPTH, D_XBC), 0.02),
        'dt_bias': dt0 + jnp.log(-jnp.expm1(-dt0)),
        'a_log': jnp.log(jax.random.uniform(next(ks), (DEPTH, N_HEADS_B), F32, 1.0, 16.0)),
        'd_skip': gain((DEPTH, N_HEADS_B)),
        'g_norm_b': gain((DEPTH, D_B)),
        'w_b_out': nrm((DEPTH, D_B, D_MODEL), D_B ** -0.5),
        'w_o': nrm((DEPTH, D_MODEL, D_MODEL), D_MODEL ** -0.5),
        'g_pre2': gain((DEPTH, D_MODEL)),
        'g_post2': gain((DEPTH, D_MODEL)),
        'w_up': nrm((DEPTH, D_MODEL, 2 * D_FF), D_MODEL ** -0.5),
        'w_dw_f': nrm((DEPTH, CONV_F, 2 * D_FF), CONV_F ** -0.5),
        'b_dw_f': nrm((DEPTH, 2 * D_FF), 0.02),
        'w_down': nrm((DEPTH, D_FF, D_MODEL), D_FF ** -0.5),
    }


def reference(x_prompt, x_sample, state_conv_a, state_conv_b, state_ssm, state_conv_ffn, meta_tokens,
              g_pre1, g_post1, w_in, b_gate, w_dw_a, b_dw_a, g_ln_a, b_ln_a, w_a_out, b_a_out,
              w_dw_b, b_dw_b, dt_bias, a_log, d_skip, g_norm_b, w_b_out, w_o,
              g_pre2, g_post2, w_up, w_dw_f, b_dw_f, w_down):
    bp = x_prompt.shape[0]
    xp = jnp.concatenate([jnp.broadcast_to(meta_tokens.astype(x_prompt.dtype)[None], (bp, N_META, D_MODEL)), x_prompt], axis=1)
    xs = x_sample
    pa, pb, ph, pf = [], [], [], []
    sa, sb, sh, sf = [], [], [], []
    for l in range(DEPTH):
        w = (g_pre1[l], g_post1[l], w_in[l], b_gate[l], w_dw_a[l], b_dw_a[l], g_ln_a[l], b_ln_a[l], w_a_out[l], b_a_out[l],
             w_dw_b[l], b_dw_b[l], dt_bias[l], a_log[l], d_skip[l], g_norm_b[l], w_b_out[l], w_o[l],
             g_pre2[l], g_post2[l], w_up[l], w_dw_f[l], b_dw_f[l], w_down[l])
        xp, c_a, c_b, c_h, c_f = hybrid_layer(
            xp, jnp.zeros((bp, CONV_A - 1, D_A), xp.dtype), jnp.zeros((bp, CONV_B - 1, D_XBC), xp.dtype),
            jnp.zeros((bp, N_HEADS_B, HEAD_DIM, D_STATE), F32), jnp.zeros((bp, CONV_F - 1, 2 * D_FF), xp.dtype),
            CHUNK - N_META, CHUNK, *w)
        pa.append(c_a); pb.append(c_b); ph.append(c_h); pf.append(c_f)
        xs, d_a, d_b, d_h, d_f = hybrid_layer(
            xs, state_conv_a[l], state_conv_b[l], state_ssm[l], state_conv_ffn[l], 0, xs.shape[1], *w)
        sa.append(d_a); sb.append(d_b); sh.append(d_h); sf.append(d_f)
    y_prompt = xp[:, N_META:]
    return (y_prompt, xs, jnp.stack(pa), jnp.stack(pb), jnp.stack(ph), jnp.stack(pf),
            jnp.stack(sa), jnp.stack(sb), jnp.stack(sh), jnp.stack(sf))
```

```python
import functools

import jax
import jax.numpy as jnp
from jax import lax
from jax.experimental import pallas as pl
from jax.experimental.pallas import tpu as pltpu

D_MODEL = 2048
N_META = 16
D_A = 1024
CONV_A = 31
D_B = 2048
HEAD_DIM = 64
N_HEADS = D_B // HEAD_DIM
N_GROUPS = 8
HEADS_PER_GROUP = N_HEADS // N_GROUPS
GROUP_W = HEADS_PER_GROUP * HEAD_DIM
D_STATE = 128
CONV_B = 4
D_FF = 5632
CONV_F = 3
EPS = 1e-6
D_XBC = D_B + 2 * N_GROUPS * D_STATE
COL_Z = 2 * D_A
COL_XBC = COL_Z + D_B
COL_DT = COL_XBC + D_XBC
COL_GATE = COL_DT + N_HEADS

LANES = 128
SUBLANES = 8
SSD_KEYS = 128
VMEM_LIMIT = 56 * 1024 * 1024

F32 = jnp.float32
BF16 = jnp.bfloat16


def _params(*sem):
    return pltpu.CompilerParams(dimension_semantics=sem, vmem_limit_bytes=VMEM_LIMIT)


def _sigmoid(x):
    return 1.0 / (1.0 + jnp.exp(-x))


def _silu(x):
    return x * _sigmoid(x)


def _softplus(x):
    return jnp.maximum(x, 0.0) + jnp.log1p(jnp.exp(-jnp.abs(x)))


def _gelu_tanh(x):
    return 0.5 * x * (1.0 + jnp.tanh(0.7978845608028654 * (x + 0.044715 * (x * x * x))))


def _rms(x, g):
    r = lax.rsqrt(jnp.mean(x * x, axis=-1, keepdims=True) + EPS)
    return x * r * g


def _dot(a, b):
    return jnp.dot(a, b, preferred_element_type=F32)


def _split_bf16(v, parts):
    out = []
    for _ in range(parts):
        p = v.astype(BF16)
        out.append(p)
        v = v - p.astype(F32)
    return out


def _proj_kernel(*refs, n_w, n_b, epilogue):
    x_ref, g_ref = refs[0], refs[1]
    w_refs = refs[2:2 + n_w]
    b_refs = refs[2 + n_w:2 + n_w + n_b]
    o_ref, h_ref = refs[-2], refs[-1]

    @pl.when(pl.program_id(1) == 0)
    def _():
        h_ref[...] = _rms(x_ref[...], g_ref[...]).astype(BF16)

    h = h_ref[...]
    accs = [_dot(h, w[...]) for w in w_refs]
    o_ref[...] = epilogue(accs, [b[...] for b in b_refs]).astype(o_ref.dtype)


def _proj(x, g, ws, bs, epilogue, tn, out_dtype, name):
    rows, d = x.shape
    n = ws[0].shape[1]
    tm = min(rows, 512)
    kern = functools.partial(_proj_kernel, n_w=len(ws), n_b=len(bs), epilogue=epilogue)
    return pl.pallas_call(
        kern,
        grid=(rows // tm, n // tn),
        in_specs=[pl.BlockSpec((tm, d), lambda i, j: (i, 0)),
                  pl.BlockSpec((1, d), lambda i, j: (0, 0))]
                 + [pl.BlockSpec((d, tn), lambda i, j: (0, j)) for _ in ws]
                 + [pl.BlockSpec((1, tn), lambda i, j: (0, j)) for _ in bs],
        out_specs=pl.BlockSpec((tm, tn), lambda i, j: (i, j)),
        out_shape=jax.ShapeDtypeStruct((rows, n), out_dtype),
        scratch_shapes=[pltpu.VMEM((tm, d), BF16)],
        compiler_params=_params("parallel", "arbitrary"),
        name=name,
    )(x, g, *ws, *bs)


def _epi_glu(a, b):
    return a[0] * _sigmoid(a[1])


def _epi_dt(a, b):
    lane = lax.broadcasted_iota(jnp.int32, a[0].shape, 1)
    return jnp.where(lane < N_HEADS, _softplus(a[0] + b[0]), 0.0)


def _epi_silu(a, b):
    return _silu(a[0])


def _epi_id(a, b):
    return a[0]


def _epi_gate(a, b):
    return _sigmoid(a[0] + b[0])


def _merge_kernel(ua_ref, yb_ref, gt_ref, x_ref, wa_ref, ba_ref, wb_ref, wo_ref, gp_ref, o_ref):
    ya = _dot(ua_ref[...], wa_ref[...]) + ba_ref[...]
    yb = _dot(yb_ref[...], wb_ref[...])
    mix = gt_ref[:, :D_MODEL] * ya + gt_ref[:, D_MODEL:] * yb
    m = _dot(mix.astype(BF16), wo_ref[...])
    o_ref[...] = x_ref[...] + _rms(m, gp_ref[...])


def _merge(ua, yb, gates, x, wa, ba, wb, wo, gp):
    rows = x.shape[0]
    tm = min(rows, 256)
    row_spec = lambda w: pl.BlockSpec((tm, w), lambda i: (i, 0))
    res_spec = lambda a: pl.BlockSpec(a.shape, lambda i: (0, 0), pipeline_mode=pl.Buffered(1))
    return pl.pallas_call(
        _merge_kernel,
        grid=(rows // tm,),
        in_specs=[row_spec(D_A), row_spec(D_B), row_spec(2 * D_MODEL), row_spec(D_MODEL),
                  res_spec(wa), res_spec(ba), res_spec(wb), res_spec(wo), res_spec(gp)],
        out_specs=row_spec(D_MODEL),
        out_shape=jax.ShapeDtypeStruct((rows, D_MODEL), F32),
        compiler_params=_params("parallel"),
        name="merge",
    )(ua, yb, gates, x, wa, ba, wb, wo, gp)


def _down_kernel(act_ref, w_ref, x1_ref, g_ref, o_ref, acc_ref):
    k = pl.program_id(1)

    @pl.when(k == 0)
    def _():
        acc_ref[...] = jnp.zeros_like(acc_ref)

    acc_ref[...] += _dot(act_ref[...], w_ref[...])

    @pl.when(k == pl.num_programs(1) - 1)
    def _():
        o_ref[...] = x1_ref[...] + _rms(acc_ref[...], g_ref[...])


def _ffn_down(act, w, x1, g):
    rows = x1.shape[0]
    tm = min(rows, 512)
    tk = 512
    return pl.pallas_call(
        _down_kernel,
        grid=(rows // tm, D_FF // tk),
        in_specs=[pl.BlockSpec((tm, tk), lambda i, k: (i, k)),
                  pl.BlockSpec((tk, D_MODEL), lambda i, k: (k, 0)),
                  pl.BlockSpec((tm, D_MODEL), lambda i, k: (i, 0)),
                  pl.BlockSpec((1, D_MODEL), lambda i, k: (0, 0))],
        out_specs=pl.BlockSpec((tm, D_MODEL), lambda i, k: (i, 0)),
        out_shape=jax.ShapeDtypeStruct((rows, D_MODEL), F32),
        scratch_shapes=[pltpu.VMEM((tm, D_MODEL), F32)],
        compiler_params=_params("parallel", "arbitrary"),
        name="ffn_down",
    )(act, w, x1, g)


def _post_ln_silu(outs, extras):
    u = outs[0]
    mu = jnp.mean(u, axis=-1, keepdims=True)
    xc = u - mu
    r = lax.rsqrt(jnp.mean(xc * xc, axis=-1, keepdims=True) + EPS)
    return _silu(xc * r * extras[0] + extras[1])


def _post_silu(outs, extras):
    return _silu(outs[0])


def _post_gelu_gate(outs, extras):
    return _gelu_tanh(outs[0]) * outs[1]


def _conv_seq_kernel(*refs, width, tl, n_s, n_x, post, hist, rc):
    u_refs = refs[0:n_s]
    st_refs = refs[n_s:2 * n_s]
    w_refs = refs[2 * n_s:3 * n_s]
    b_refs = refs[3 * n_s:4 * n_s]
    x_refs = refs[4 * n_s:4 * n_s + n_x]
    o_ref = refs[4 * n_s + n_x]
    ns_refs = refs[4 * n_s + n_x + 1:4 * n_s + n_x + 1 + n_s]
    win_refs = refs[4 * n_s + n_x + 1 + n_s:]
    past = width - 1
    l = pl.program_id(2)

    for s in range(n_s):
        win = win_refs[s]

        @pl.when(l == 0)
        def _(win=win, s=s):
            if hist > past:
                win[0:hist - past, :] = jnp.zeros((hist - past, win.shape[1]), F32)
            win[hist - past:hist, :] = st_refs[s][0]

        @pl.when(l > 0)
        def _(win=win):
            win[0:hist, :] = win[tl:tl + hist, :]

        win[hist:hist + tl, :] = u_refs[s][0]

    extras = [x[...] for x in x_refs]
    for c in range(tl // rc):
        outs = []
        for s in range(n_s):
            acc = b_refs[s][...] + w_refs[s][0:1, :] * win_refs[s][c * rc + hist - past:c * rc + hist - past + rc, :]
            for k in range(1, width):
                lo = c * rc + hist - past + k
                acc = acc + w_refs[s][k:k + 1, :] * win_refs[s][lo:lo + rc, :]
            outs.append(acc)
        o_ref[0, c * rc:(c + 1) * rc, :] = post(outs, extras).astype(o_ref.dtype)

    @pl.when(l == pl.num_programs(2) - 1)
    def _():
        for s in range(n_s):
            ns_refs[s][0] = win_refs[s][hist + tl - past:hist + tl, :]


def _conv_seq(u, state, w, bias, extras, post, *, width, ct, tl, n_s, out_w, out_dtype, name):
    b, L, _ = u.shape
    past = width - 1
    hist = -(-past // SUBLANES) * SUBLANES
    nj = out_w // ct
    rc = min(tl, 32)
    kern = functools.partial(_conv_seq_kernel, width=width, tl=tl, n_s=n_s, n_x=len(extras),
                             post=post, hist=hist, rc=rc)
    in_specs = ([pl.BlockSpec((1, tl, ct), lambda i, j, l, s=s: (i, l, j + s * nj)) for s in range(n_s)]
                + [pl.BlockSpec((1, past, ct), lambda i, j, l, s=s: (i, 0, j + s * nj)) for s in range(n_s)]
                + [pl.BlockSpec((width, ct), lambda i, j, l, s=s: (0, j + s * nj)) for s in range(n_s)]
                + [pl.BlockSpec((1, ct), lambda i, j, l, s=s: (0, j + s * nj)) for s in range(n_s)]
                + [pl.BlockSpec((1, ct), lambda i, j, l: (0, j)) for _ in extras])
    out_specs = ([pl.BlockSpec((1, tl, ct), lambda i, j, l: (i, l, j))]
                 + [pl.BlockSpec((1, past, ct), lambda i, j, l: (i, 0, j)) for _ in range(n_s)])
    out_shape = ([jax.ShapeDtypeStruct((b, L, out_w), out_dtype)]
                 + [jax.ShapeDtypeStruct((b, past, out_w), F32) for _ in range(n_s)])
    res = pl.pallas_call(
        kern,
        grid=(b, nj, L // tl),
        in_specs=in_specs,
        out_specs=out_specs,
        out_shape=out_shape,
        scratch_shapes=[pltpu.VMEM((hist + tl, ct), F32) for _ in range(n_s)],
        compiler_params=_params("parallel", "parallel", "arbitrary"),
        name=name,
    )(*([u] * n_s), *([state] * n_s), *([w] * n_s), *([bias] * n_s), *extras)
    return res[0], res[1:]


def _conv_slab_kernel(*refs, width, steps, n_s, n_x, post):
    u_refs = refs[0:n_s]
    st_refs = refs[n_s:2 * n_s]
    w_refs = refs[2 * n_s:3 * n_s]
    b_refs = refs[3 * n_s:4 * n_s]
    x_refs = refs[4 * n_s:4 * n_s + n_x]
    o_ref = refs[4 * n_s + n_x]
    ns_refs = refs[4 * n_s + n_x + 1:]
    past = width - 1

    def slab(s, i):
        return st_refs[s][i] if i < past else u_refs[s][i - past]

    extras = [x[...] for x in x_refs]
    for t in range(steps):
        outs = []
        for s in range(n_s):
            acc = b_refs[s][...] + w_refs[s][0:1, :] * slab(s, t)
            for k in range(1, width):
                acc = acc + w_refs[s][k:k + 1, :] * slab(s, t + k)
            outs.append(acc)
        o_ref[t] = post(outs, extras).astype(o_ref.dtype)
    for s in range(n_s):
        for i in range(past):
            ns_refs[s][i] = slab(s, i + steps)


def _conv_slab(u, state, w, bias, extras, post, *, width, ct, nbt, n_s, out_w, out_dtype, name):
    steps, nb, _ = u.shape
    past = width - 1
    nj = out_w // ct
    kern = functools.partial(_conv_slab_kernel, width=width, steps=steps, n_s=n_s, n_x=len(extras), post=post)
    in_specs = ([pl.BlockSpec((steps, nbt, ct), lambda i, j, s=s: (0, i, j + s * nj)) for s in range(n_s)]
                + [pl.BlockSpec((past, nbt, ct), lambda i, j, s=s: (0, i, j + s * nj)) for s in range(n_s)]
                + [pl.BlockSpec((width, ct), lambda i, j, s=s: (0, j + s * nj)) for s in range(n_s)]
                + [pl.BlockSpec((1, ct), lambda i, j, s=s: (0, j + s * nj)) for s in range(n_s)]
                + [pl.BlockSpec((1, ct), lambda i, j: (0, j)) for _ in extras])
    out_specs = ([pl.BlockSpec((steps, nbt, ct), lambda i, j: (0, i, j))]
                 + [pl.BlockSpec((past, nbt, ct), lambda i, j: (0, i, j)) for _ in range(n_s)])
    out_shape = ([jax.ShapeDtypeStruct((steps, nb, out_w), out_dtype)]
                 + [jax.ShapeDtypeStruct((past, nb, out_w), F32) for _ in range(n_s)])
    res = pl.pallas_call(
        kern,
        grid=(nb // nbt, nj),
        in_specs=in_specs,
        out_specs=out_specs,
        out_shape=out_shape,
        compiler_params=_params("parallel", "parallel"),
        name=name,
    )(*([u] * n_s), *([state] * n_s), *([w] * n_s), *([bias] * n_s), *extras)
    return res[0], res[1:]


def _ssd_kernel(xbc_ref, dt_ref, zs_ref, h0_ref, alog_ref, dsk_ref, gn_ref, e_ref, et_ref,
                y_ref, ht_ref, h_scr, y_scr, *pads, q):
    qk = SSD_KEYS
    c = pl.program_id(1)

    @pl.when(c == 0)
    def _():
        h_scr[...] = h0_ref[0]

    if q == qk:
        xbc = xbc_ref[0]
        dt = dt_ref[0]
    else:
        xpad, dtpad = pads
        xpad[...] = jnp.zeros_like(xpad)
        dtpad[...] = jnp.zeros_like(dtpad)
        xpad[0:q, :] = xbc_ref[0]
        dtpad[0:q, :] = dt_ref[0]
        xbc = xpad[...]
        dt = dtpad[...]

    x = xbc[:, :D_B]
    bm = xbc[:, D_B:D_B + N_GROUPS * D_STATE].astype(BF16)
    cm = xbc[:q, D_B + N_GROUPS * D_STATE:].astype(BF16)
    a = -jnp.exp(alog_ref[...])
    d_a = dt * a

    row = lax.broadcasted_iota(jnp.int32, (qk, qk), 0)
    col = lax.broadcasted_iota(jnp.int32, (qk, qk), 1)
    tril = jnp.where(row >= col, 1.0, 0.0).astype(BF16)
    cum = sum(_dot(tril, p) for p in _split_bf16(d_a, 3))
    cum_t = cum.T
    cum_last = cum[qk - 1:qk, :]

    e = e_ref[...]

    def expand(v):
        return sum(_dot(p, e) for p in _split_bf16(v, 2))

    xdt = x * expand(dt)
    xw = (xdt * expand(jnp.exp(cum_last - cum))).astype(BF16)
    xdt = xdt.astype(BF16)
    ecum = expand(jnp.exp(cum[:q, :]))
    decay_t = jnp.broadcast_to(jnp.exp(cum_t[:, qk - 1:qk]), (LANES, D_STATE))
    decay_rows = sum(_dot(et_ref[...], p) for p in _split_bf16(decay_t, 2))

    causal = row[:q, :] >= col[:q, :]
    lane_head = lax.broadcasted_iota(jnp.int32, (q, GROUP_W), 1) // HEAD_DIM
    for g in range(N_GROUPS):
        bg = bm[:, g * D_STATE:(g + 1) * D_STATE]
        cg = cm[:, g * D_STATE:(g + 1) * D_STATE]
        cb = lax.dot_general(cg, bg, (((1,), (1,)), ((), ())), preferred_element_type=F32)
        xdt_g = xdt[:, g * GROUP_W:(g + 1) * GROUP_W]
        y_g = jnp.zeros((q, GROUP_W), F32)
        for r in range(HEADS_PER_GROUP):
            hd = g * HEADS_PER_GROUP + r
            seg = cum[:q, hd:hd + 1] - cum_t[hd:hd + 1, :]
            decay = jnp.where(causal, jnp.exp(jnp.minimum(seg, 0.0)), 0.0)
            y_r = _dot((cb * decay).astype(BF16), xdt_g)
            y_g = y_g + jnp.where(lane_head == r, y_r, 0.0)
        h_g = h_scr[g * GROUP_W:(g + 1) * GROUP_W, :]
        y_inter = lax.dot_general(cg, h_g.astype(BF16), (((1,), (1,)), ((), ())), preferred_element_type=F32)
        y_scr[:, g * GROUP_W:(g + 1) * GROUP_W] = y_g + y_inter * ecum[:, g * GROUP_W:(g + 1) * GROUP_W]
        s_g = lax.dot_general(xw[:, g * GROUP_W:(g + 1) * GROUP_W], bg, (((0,), (0,)), ((), ())),
                              preferred_element_type=F32)
        h_scr[g * GROUP_W:(g + 1) * GROUP_W, :] = decay_rows[g * GROUP_W:(g + 1) * GROUP_W, :] * h_g + s_g

    y = y_scr[...] + dsk_ref[...] * x[:q, :]
    y_ref[0] = _rms(y * zs_ref[0], gn_ref[...]).astype(y_ref.dtype)

    @pl.when(c == pl.num_programs(1) - 1)
    def _():
        ht_ref[0] = h_scr[...]


def _ssd(xbc, dt, zs, h0, alog, dsk, gn, e, et, *, q):
    b, L, _ = xbc.shape
    seq = lambda w: pl.BlockSpec((1, q, w), lambda i, c: (i, c, 0))
    full = lambda arr: pl.BlockSpec(arr.shape, lambda i, c: (0,) * arr.ndim)
    state = pl.BlockSpec((1, D_B, D_STATE), lambda i, c: (i, 0, 0))
    pads = [] if q == SSD_KEYS else [pltpu.VMEM((SSD_KEYS, D_XBC), F32), pltpu.VMEM((SSD_KEYS, LANES), F32)]
    return pl.pallas_call(
        functools.partial(_ssd_kernel, q=q),
        grid=(b, L // q),
        in_specs=[seq(D_XBC), seq(LANES), seq(D_B), state,
                  full(alog), full(dsk), full(gn), full(e), full(et)],
        out_specs=[seq(D_B), state],
        out_shape=[jax.ShapeDtypeStruct((b, L, D_B), BF16),
                   jax.ShapeDtypeStruct((b, D_B, D_STATE), F32)],
        scratch_shapes=[pltpu.VMEM((D_B, D_STATE), F32), pltpu.VMEM((q, D_B), F32)] + pads,
        compiler_params=_params("parallel", "arbitrary"),
        name="ssd",
    )(xbc, dt, zs, h0, alog, dsk, gn, e, et)


def _layer(x, wts, conv_a, conv_b, ssd_fn, conv_f):
    g1 = wts["g_pre1"]
    uglu = _proj(x, g1, [wts["w_val"], wts["w_gate"]], [], _epi_glu, 512, F32, "in_glu")
    dt = _proj(x, g1, [wts["w_dt"]], [wts["dt_bias"]], _epi_dt, LANES, F32, "in_dt")
    zs = _proj(x, g1, [wts["w_z"]], [], _epi_silu, 512, F32, "in_z")
    xbc = _proj(x, g1, [wts["w_xbc"]], [], _epi_id, 512, F32, "in_xbc")
    gates = _proj(x, g1, [wts["w_g"]], [wts["b_gate"]], _epi_gate, 512, F32, "in_gates")

    ua, st_a = conv_a(uglu)
    xbc_act, st_b = conv_b(xbc)
    yb, st_h = ssd_fn(xbc_act, dt, zs)

    x1 = _merge(ua, yb, gates, x, wts["w_a_out"], wts["b_a_out"], wts["w_b_out"], wts["w_o"], wts["g_post1"])
    u = _proj(x1, wts["g_pre2"], [wts["w_up"]], [], _epi_id, 512, F32, "ffn_up")
    act, st_f = conv_f(u)
    x2 = _ffn_down(act, wts["w_down"], x1, wts["g_post2"])
    return x2, st_a, st_b, st_h, st_f


def _seq_group(x, wts, st_a, st_b, st_h, st_f, tl, q):
    b, L, _ = x.shape

    def conv_a(uglu):
        ua, (ns,) = _conv_seq(uglu.reshape(b, L, D_A), st_a, wts["w_dw_a"], wts["b_dw_a"],
                              [wts["g_ln_a"], wts["b_ln_a"]], _post_ln_silu, width=CONV_A, ct=D_A,
                              tl=min(tl, 256), n_s=1, out_w=D_A, out_dtype=BF16, name="conv_a")
        return ua.reshape(b * L, D_A), ns

    def conv_b(xbc):
        act, (ns,) = _conv_seq(xbc.reshape(b, L, D_XBC), st_b, wts["w_dw_b"], wts["b_dw_b"], [], _post_silu,
                               width=CONV_B, ct=512, tl=tl, n_s=1, out_w=D_XBC, out_dtype=F32, name="conv_b")
        return act, ns

    def ssd_fn(xbc_act, dt, zs):
        yb, ht = _ssd(xbc_act, dt.reshape(b, L, LANES), zs.reshape(b, L, D_B), st_h,
                      wts["a_log"], wts["d_skip"], wts["g_norm_b"], wts["expand"], wts["expand_t"], q=q)
        return yb.reshape(b * L, D_B), ht

    def conv_f(u):
        act, ns = _conv_seq(u.reshape(b, L, 2 * D_FF), st_f, wts["w_dw_f"], wts["b_dw_f"], [], _post_gelu_gate,
                            width=CONV_F, ct=512, tl=tl, n_s=2, out_w=D_FF, out_dtype=BF16, name="conv_f")
        return act.reshape(b * L, D_FF), jnp.concatenate(ns, axis=-1)

    x2, ns_a, ns_b, ns_h, ns_f = _layer(x.reshape(b * L, D_MODEL), wts, conv_a, conv_b, ssd_fn, conv_f)
    return x2.reshape(b, L, D_MODEL), ns_a, ns_b, ns_h, ns_f


def _step_group(x, wts, st_a, st_b, st_h, st_f):
    nb, L, _ = x.shape
    tmaj = lambda s: jnp.transpose(s, (1, 0, 2))
    lpad = -(-L // SUBLANES) * SUBLANES

    def conv_a(uglu):
        ua, (ns,) = _conv_slab(uglu.reshape(L, nb, D_A), tmaj(st_a), wts["w_dw_a"], wts["b_dw_a"],
                               [wts["g_ln_a"], wts["b_ln_a"]], _post_ln_silu, width=CONV_A, ct=D_A,
                               nbt=32, n_s=1, out_w=D_A, out_dtype=BF16, name="conv_a_step")
        return ua.reshape(L * nb, D_A), tmaj(ns)

    def conv_b(xbc):
        act, (ns,) = _conv_slab(xbc.reshape(L, nb, D_XBC), tmaj(st_b), wts["w_dw_b"], wts["b_dw_b"], [],
                                _post_silu, width=CONV_B, ct=512, nbt=nb, n_s=1, out_w=D_XBC,
                                out_dtype=F32, name="conv_b_step")
        return act, tmaj(ns)

    def bmaj_pad(v):
        return jnp.pad(tmaj(v), ((0, 0), (0, lpad - L), (0, 0)))

    def ssd_fn(xbc_act, dt, zs):
        yb, ht = _ssd(bmaj_pad(xbc_act), bmaj_pad(dt.reshape(L, nb, LANES)), bmaj_pad(zs.reshape(L, nb, D_B)),
                      st_h, wts["a_log"], wts["d_skip"], wts["g_norm_b"], wts["expand"], wts["expand_t"], q=lpad)
        return tmaj(yb[:, :L]).reshape(L * nb, D_B), ht

    def conv_f(u):
        act, ns = _conv_slab(u.reshape(L, nb, 2 * D_FF), tmaj(st_f), wts["w_dw_f"], wts["b_dw_f"], [],
                             _post_gelu_gate, width=CONV_F, ct=512, nbt=nb, n_s=2, out_w=D_FF,
                             out_dtype=BF16, name="conv_f_step")
        return act.reshape(L * nb, D_FF), tmaj(jnp.concatenate(ns, axis=-1))

    x2, ns_a, ns_b, ns_h, ns_f = _layer(tmaj(x).reshape(L * nb, D_MODEL), wts, conv_a, conv_b, ssd_fn, conv_f)
    return tmaj(x2.reshape(L, nb, D_MODEL)), ns_a, ns_b, ns_h, ns_f


def _layer_weights(l, g_pre1, g_post1, w_in, b_gate, w_dw_a, b_dw_a, g_ln_a, b_ln_a, w_a_out, b_a_out,
                   w_dw_b, b_dw_b, dt_bias, a_log, d_skip, g_norm_b, w_b_out, w_o,
                   g_pre2, g_post2, w_up, w_dw_f, b_dw_f, w_down):
    row = lambda v: v[l].reshape(1, -1)
    lane_pad = lambda v: jnp.pad(v, ((0, 0), (0, LANES - v.shape[1])))
    w = w_in[l]
    head = jnp.arange(LANES, dtype=jnp.int32)[:, None]
    chan_head = (jnp.arange(D_B, dtype=jnp.int32) // HEAD_DIM)[None, :]
    expand = (head == chan_head).astype(BF16)
    return {
        "g_pre1": row(g_pre1), "g_post1": row(g_post1), "g_pre2": row(g_pre2), "g_post2": row(g_post2),
        "w_val": w[:, :D_A].astype(BF16), "w_gate": w[:, D_A:COL_Z].astype(BF16),
        "w_z": w[:, COL_Z:COL_XBC].astype(BF16), "w_xbc": w[:, COL_XBC:COL_DT].astype(BF16),
        "w_dt": lane_pad(w[:, COL_DT:COL_GATE]).astype(BF16), "w_g": w[:, COL_GATE:].astype(BF16),
        "dt_bias": lane_pad(row(dt_bias)), "b_gate": row(b_gate),
        "w_dw_a": w_dw_a[l], "b_dw_a": row(b_dw_a), "g_ln_a": row(g_ln_a), "b_ln_a": row(b_ln_a),
        "w_a_out": w_a_out[l].astype(BF16), "b_a_out": row(b_a_out),
        "w_dw_b": w_dw_b[l], "b_dw_b": row(b_dw_b),
        "a_log": lane_pad(row(a_log)), "d_skip": jnp.repeat(d_skip[l], HEAD_DIM).reshape(1, D_B),
        "g_norm_b": row(g_norm_b), "expand": expand, "expand_t": expand.T,
        "w_b_out": w_b_out[l].astype(BF16), "w_o": w_o[l].astype(BF16),
        "w_up": w_up[l].astype(BF16), "w_dw_f": w_dw_f[l], "b_dw_f": row(b_dw_f),
        "w_down": w_down[l].astype(BF16),
    }


def kernel(x_prompt, x_sample, state_conv_a, state_conv_b, state_ssm, state_conv_ffn, meta_tokens, g_pre1, g_post1, w_in, b_gate, w_dw_a, b_dw_a, g_ln_a, b_ln_a, w_a_out, b_a_out, w_dw_b, b_dw_b, dt_bias, a_log, d_skip, g_norm_b, w_b_out, w_o, g_pre2, g_post2, w_up, w_dw_f, b_dw_f, w_down):
    depth = w_in.shape[0]
    bp = x_prompt.shape[0]
    nb = x_sample.shape[0]
    xm = meta_tokens.astype(x_prompt.dtype)[None]
    xp, xs = x_prompt, x_sample
    pa, pb, ph, pf = [], [], [], []
    sa, sb, sh, sf = [], [], [], []
    for l in range(depth):
        wts = _layer_weights(l, g_pre1, g_post1, w_in, b_gate, w_dw_a, b_dw_a, g_ln_a, b_ln_a, w_a_out, b_a_out,
                             w_dw_b, b_dw_b, dt_bias, a_log, d_skip, g_norm_b, w_b_out, w_o,
                             g_pre2, g_post2, w_up, w_dw_f, b_dw_f, w_down)
        xm, m_a, m_b, m_h, m_f = _seq_group(
            xm, wts, jnp.zeros((1, CONV_A - 1, D_A), F32), jnp.zeros((1, CONV_B - 1, D_XBC), F32),
            jnp.zeros((1, D_B, D_STATE), F32), jnp.zeros((1, CONV_F - 1, 2 * D_FF), F32), tl=N_META, q=SUBLANES)
        rep = lambda s: jnp.broadcast_to(s, (bp,) + s.shape[1:])
        xp, c_a, c_b, c_h, c_f = _seq_group(xp, wts, rep(m_a), rep(m_b), rep(m_h), rep(m_f), tl=512, q=SSD_KEYS)
        pa.append(c_a); pb.append(c_b); ph.append(c_h.reshape(bp, N_HEADS, HEAD_DIM, D_STATE)); pf.append(c_f)
        xs, d_a, d_b, d_h, d_f = _step_group(
            xs, wts, state_conv_a[l], state_conv_b[l], state_ssm[l].reshape(nb, D_B, D_STATE), state_conv_ffn[l])
        sa.append(d_a); sb.append(d_b); sh.append(d_h.reshape(nb, N_HEADS, HEAD_DIM, D_STATE)); sf.append(d_f)
    return (xp, xs, jnp.stack(pa), jnp.stack(pb), jnp.stack(ph), jnp.stack(pf),
            jnp.stack(sa), jnp.stack(sb), jnp.stack(sh), jnp.stack(sf))
```

```python
import functools

import jax
import jax.numpy as jnp
from jax import lax
from jax.experimental import pallas as pl
from jax.experimental.pallas import tpu as pltpu

D_MODEL = 2048
N_META = 16
D_A = 1024
CONV_A = 31
D_B = 2048
HEAD_DIM = 64
N_HEADS = D_B // HEAD_DIM
N_GROUPS = 8
HEADS_PER_GROUP = N_HEADS // N_GROUPS
GROUP_W = HEADS_PER_GROUP * HEAD_DIM
D_STATE = 128
CONV_B = 4
D_FF = 5632
CONV_F = 3
EPS = 1e-6
D_XBC = D_B + 2 * N_GROUPS * D_STATE
COL_Z = 2 * D_A
COL_XBC = COL_Z + D_B
COL_DT = COL_XBC + D_XBC
COL_GATE = COL_DT + N_HEADS

LANES = 128
SUBLANES = 8
SSD_KEYS = 128
VMEM_LIMIT = 56 * 1024 * 1024

F32 = jnp.float32
BF16 = jnp.bfloat16


def _params(*sem):
    return pltpu.CompilerParams(dimension_semantics=sem, vmem_limit_bytes=VMEM_LIMIT)


def _sigmoid(x):
    return 1.0 / (1.0 + jnp.exp(-x))


def _silu(x):
    return x * _sigmoid(x)


def _softplus(x):
    return jnp.maximum(x, 0.0) + jnp.log1p(jnp.exp(-jnp.abs(x)))


def _gelu_tanh(x):
    return 0.5 * x * (1.0 + jnp.tanh(0.7978845608028654 * (x + 0.044715 * (x * x * x))))


def _rms(x, g):
    r = lax.rsqrt(jnp.mean(x * x, axis=-1, keepdims=True) + EPS)
    return x * r * g


def _dot(a, b):
    return jnp.dot(a, b, preferred_element_type=F32)


def _split_bf16(v, parts):
    out = []
    for _ in range(parts):
        p = v.astype(BF16)
        out.append(p)
        v = v - p.astype(F32)
    return out


def _norm_kernel(x_ref, g_ref, o_ref):
    o_ref[...] = _rms(x_ref[...], g_ref[...]).astype(o_ref.dtype)


def _norm(x, g):
    rows, d = x.shape
    tm = min(rows, 512)
    return pl.pallas_call(
        _norm_kernel,
        grid=(rows // tm,),
        in_specs=[pl.BlockSpec((tm, d), lambda i: (i, 0)), pl.BlockSpec((1, d), lambda i: (0, 0))],
        out_specs=pl.BlockSpec((tm, d), lambda i: (i, 0)),
        out_shape=jax.ShapeDtypeStruct((rows, d), BF16),
        compiler_params=_params("parallel"),
        name="norm",
    )(x, g)


def _proj_kernel(*refs, n_w, n_b, epilogue):
    h_ref = refs[0]
    w_refs = refs[1:1 + n_w]
    b_refs = refs[1 + n_w:1 + n_w + n_b]
    o_ref = refs[-1]
    h = h_ref[...]
    accs = [_dot(h, w[...]) for w in w_refs]
    o_ref[...] = epilogue(accs, [b[...] for b in b_refs]).astype(o_ref.dtype)


def _proj(h, ws, bs, epilogue, tn, out_dtype, name):
    rows, d = h.shape
    n = ws[0].shape[1]
    tm = min(rows, 1024)
    kern = functools.partial(_proj_kernel, n_w=len(ws), n_b=len(bs), epilogue=epilogue)
    return pl.pallas_call(
        kern,
        grid=(rows // tm, n // tn),
        in_specs=[pl.BlockSpec((tm, d), lambda i, j: (i, 0))]
                 + [pl.BlockSpec((d, tn), lambda i, j: (0, j)) for _ in ws]
                 + [pl.BlockSpec((1, tn), lambda i, j: (0, j)) for _ in bs],
        out_specs=pl.BlockSpec((tm, tn), lambda i, j: (i, j)),
        out_shape=jax.ShapeDtypeStruct((rows, n), out_dtype),
        compiler_params=_params("parallel", "parallel"),
        name=name,
    )(h, *ws, *bs)


def _epi_glu(a, b):
    return a[0] * _sigmoid(a[1])


def _epi_dt(a, b):
    lane = lax.broadcasted_iota(jnp.int32, a[0].shape, 1)
    return jnp.where(lane < N_HEADS, _softplus(a[0] + b[0]), 0.0)


def _epi_silu(a, b):
    return _silu(a[0])


def _epi_id(a, b):
    return a[0]


def _epi_gate(a, b):
    return _sigmoid(a[0] + b[0])


def _merge_kernel(ua_ref, yb_ref, gt_ref, x_ref, wa_ref, ba_ref, wb_ref, wo_ref, gp_ref, o_ref):
    ya = _dot(ua_ref[...], wa_ref[...]) + ba_ref[...]
    yb = _dot(yb_ref[...], wb_ref[...])
    mix = gt_ref[:, :D_MODEL] * ya + gt_ref[:, D_MODEL:] * yb
    m = _dot(mix.astype(BF16), wo_ref[...])
    o_ref[...] = x_ref[...] + _rms(m, gp_ref[...])


def _merge(ua, yb, gates, x, wa, ba, wb, wo, gp):
    rows = x.shape[0]
    tm = min(rows, 256)
    row_spec = lambda w: pl.BlockSpec((tm, w), lambda i: (i, 0))
    res_spec = lambda a: pl.BlockSpec(a.shape, lambda i: (0, 0), pipeline_mode=pl.Buffered(1))
    return pl.pallas_call(
        _merge_kernel,
        grid=(rows // tm,),
        in_specs=[row_spec(D_A), row_spec(D_B), row_spec(2 * D_MODEL), row_spec(D_MODEL),
                  res_spec(wa), res_spec(ba), res_spec(wb), res_spec(wo), res_spec(gp)],
        out_specs=row_spec(D_MODEL),
        out_shape=jax.ShapeDtypeStruct((rows, D_MODEL), F32),
        compiler_params=_params("parallel"),
        name="merge",
    )(ua, yb, gates, x, wa, ba, wb, wo, gp)


def _ffn_kernel(x1_ref, gpre_ref, wg_ref, wv_ref, wd_ref, stg_ref, stv_ref, cwg_ref, cwv_ref, cbg_ref, cbv_ref,
                gpost_ref, o_ref, nsg_ref, nsv_ref, h_scr, acc_scr, win_g, win_v, *carry, tm, sh, hist):
    l = pl.program_id(1)
    j = pl.program_id(2)
    past = (CONV_F - 1) * sh

    @pl.when(j == 0)
    def _():
        h_scr[...] = _rms(x1_ref[0], gpre_ref[...]).astype(BF16)
        acc_scr[...] = jnp.zeros_like(acc_scr)

    h = h_scr[...]
    conv = []
    for s, (w_ref, st_ref, cw_ref, cb_ref, ns_ref, win) in enumerate((
            (wg_ref, stg_ref, cwg_ref, cbg_ref, nsg_ref, win_g),
            (wv_ref, stv_ref, cwv_ref, cbv_ref, nsv_ref, win_v))):
        u = _dot(h, w_ref[...])
        if carry:
            @pl.when(l == 0)
            def _(win=win, st_ref=st_ref):
                win[hist - past:hist, :] = st_ref[0]

            @pl.when(l > 0)
            def _(win=win, s=s):
                win[0:hist, :] = carry[0][j, s]
        else:
            win[hist - past:hist, :] = st_ref[0]
        win[hist:hist + tm, :] = u
        conv.append(cb_ref[...] + cw_ref[0:1, :] * win[hist - 2 * sh:hist - 2 * sh + tm, :]
                    + cw_ref[1:2, :] * win[hist - sh:hist - sh + tm, :] + cw_ref[2:3, :] * u)
        ns_ref[0, 0] = win[hist + tm - past:hist + tm, :]
        if carry:
            carry[0][j, s] = win[tm:tm + hist, :]

    act = (_gelu_tanh(conv[0]) * conv[1]).astype(BF16)
    acc_scr[...] += _dot(act, wd_ref[...])

    @pl.when(j == pl.num_programs(2) - 1)
    def _():
        o_ref[0] = x1_ref[0] + _rms(acc_scr[...], gpost_ref[...])


def _ffn(x1, state, wts, *, tm, sh):
    b, L, _ = x1.shape
    tf = 512
    nj = D_FF // tf
    past = (CONV_F - 1) * sh
    hist = -(-past // SUBLANES) * SUBLANES
    n_l = L // tm
    carry = [pltpu.VMEM((nj, 2, hist, tf), F32)] if n_l > 1 else []
    col = lambda rows, off: pl.BlockSpec((rows, tf), lambda i, l, j: (0, j + off))
    st = lambda off: pl.BlockSpec((1, past, tf), lambda i, l, j: (i, 0, j + off))
    tail = pl.BlockSpec((1, 1, past, tf), lambda i, l, j: (i, l, 0, j))
    vec = pl.BlockSpec((1, D_MODEL), lambda i, l, j: (0, 0))
    xblk = pl.BlockSpec((1, tm, D_MODEL), lambda i, l, j: (i, l, 0))
    out, ns_g, ns_v = pl.pallas_call(
        functools.partial(_ffn_kernel, tm=tm, sh=sh, hist=hist),
        grid=(b, n_l, nj),
        in_specs=[xblk, vec, col(D_MODEL, 0), col(D_MODEL, nj),
                  pl.BlockSpec((tf, D_MODEL), lambda i, l, j: (j, 0)),
                  st(0), st(nj), col(CONV_F, 0), col(CONV_F, nj), col(1, 0), col(1, nj), vec],
        out_specs=[xblk, tail, tail],
        out_shape=[jax.ShapeDtypeStruct((b, L, D_MODEL), F32),
                   jax.ShapeDtypeStruct((b, n_l, past, D_FF), F32),
                   jax.ShapeDtypeStruct((b, n_l, past, D_FF), F32)],
        scratch_shapes=[pltpu.VMEM((tm, D_MODEL), BF16), pltpu.VMEM((tm, D_MODEL), F32),
                        pltpu.VMEM((hist + tm, tf), F32), pltpu.VMEM((hist + tm, tf), F32)] + carry,
        compiler_params=_params("parallel", "arbitrary", "arbitrary"),
        name="ffn",
    )(x1, wts["g_pre2"], wts["w_up"], wts["w_up"], wts["w_down"], state, state,
      wts["w_dw_f"], wts["w_dw_f"], wts["b_dw_f"], wts["b_dw_f"], wts["g_post2"])
    return out, jnp.concatenate([ns_g[:, -1], ns_v[:, -1]], axis=-1)


def _post_ln_silu(outs, extras):
    u = outs[0]
    mu = jnp.mean(u, axis=-1, keepdims=True)
    xc = u - mu
    r = lax.rsqrt(jnp.mean(xc * xc, axis=-1, keepdims=True) + EPS)
    return _silu(xc * r * extras[0] + extras[1])


def _post_silu(outs, extras):
    return _silu(outs[0])


def _conv_seq_kernel(*refs, width, tl, n_s, n_x, post, hist, rc):
    u_refs = refs[0:n_s]
    st_refs = refs[n_s:2 * n_s]
    w_refs = refs[2 * n_s:3 * n_s]
    b_refs = refs[3 * n_s:4 * n_s]
    x_refs = refs[4 * n_s:4 * n_s + n_x]
    o_ref = refs[4 * n_s + n_x]
    ns_refs = refs[4 * n_s + n_x + 1:4 * n_s + n_x + 1 + n_s]
    win_refs = refs[4 * n_s + n_x + 1 + n_s:]
    past = width - 1
    l = pl.program_id(2)

    for s in range(n_s):
        win = win_refs[s]

        @pl.when(l == 0)
        def _(win=win, s=s):
            if hist > past:
                win[0:hist - past, :] = jnp.zeros((hist - past, win.shape[1]), F32)
            win[hist - past:hist, :] = st_refs[s][0]

        @pl.when(l > 0)
        def _(win=win):
            win[0:hist, :] = win[tl:tl + hist, :]

        win[hist:hist + tl, :] = u_refs[s][0]

    extras = [x[...] for x in x_refs]
    for c in range(tl // rc):
        outs = []
        for s in range(n_s):
            acc = b_refs[s][...] + w_refs[s][0:1, :] * win_refs[s][c * rc + hist - past:c * rc + hist - past + rc, :]
            for k in range(1, width):
                lo = c * rc + hist - past + k
                acc = acc + w_refs[s][k:k + 1, :] * win_refs[s][lo:lo + rc, :]
            outs.append(acc)
        o_ref[0, c * rc:(c + 1) * rc, :] = post(outs, extras).astype(o_ref.dtype)

    @pl.when(l == pl.num_programs(2) - 1)
    def _():
        for s in range(n_s):
            ns_refs[s][0] = win_refs[s][hist + tl - past:hist + tl, :]


def _conv_seq(u, state, w, bias, extras, post, *, width, ct, tl, n_s, out_w, out_dtype, name):
    b, L, _ = u.shape
    past = width - 1
    hist = -(-past // SUBLANES) * SUBLANES
    nj = out_w // ct
    rc = min(tl, 32)
    kern = functools.partial(_conv_seq_kernel, width=width, tl=tl, n_s=n_s, n_x=len(extras),
                             post=post, hist=hist, rc=rc)
    in_specs = ([pl.BlockSpec((1, tl, ct), lambda i, j, l, s=s: (i, l, j + s * nj)) for s in range(n_s)]
                + [pl.BlockSpec((1, past, ct), lambda i, j, l, s=s: (i, 0, j + s * nj)) for s in range(n_s)]
                + [pl.BlockSpec((width, ct), lambda i, j, l, s=s: (0, j + s * nj)) for s in range(n_s)]
                + [pl.BlockSpec((1, ct), lambda i, j, l, s=s: (0, j + s * nj)) for s in range(n_s)]
                + [pl.BlockSpec((1, ct), lambda i, j, l: (0, j)) for _ in extras])
    out_specs = ([pl.BlockSpec((1, tl, ct), lambda i, j, l: (i, l, j))]
                 + [pl.BlockSpec((1, past, ct), lambda i, j, l: (i, 0, j)) for _ in range(n_s)])
    out_shape = ([jax.ShapeDtypeStruct((b, L, out_w), out_dtype)]
                 + [jax.ShapeDtypeStruct((b, past, out_w), F32) for _ in range(n_s)])
    res = pl.pallas_call(
        kern,
        grid=(b, nj, L // tl),
        in_specs=in_specs,
        out_specs=out_specs,
        out_shape=out_shape,
        scratch_shapes=[pltpu.VMEM((hist + tl, ct), F32) for _ in range(n_s)],
        compiler_params=_params("parallel", "parallel", "arbitrary"),
        name=name,
    )(*([u] * n_s), *([state] * n_s), *([w] * n_s), *([bias] * n_s), *extras)
    return res[0], res[1:]


def _conv_slab_kernel(*refs, width, steps, n_s, n_x, post):
    u_refs = refs[0:n_s]
    st_refs = refs[n_s:2 * n_s]
    w_refs = refs[2 * n_s:3 * n_s]
    b_refs = refs[3 * n_s:4 * n_s]
    x_refs = refs[4 * n_s:4 * n_s + n_x]
    o_ref = refs[4 * n_s + n_x]
    ns_refs = refs[4 * n_s + n_x + 1:]
    past = width - 1

    def slab(s, i):
        return st_refs[s][i] if i < past else u_refs[s][i - past]

    extras = [x[...] for x in x_refs]
    for t in range(steps):
        outs = []
        for s in range(n_s):
            acc = b_refs[s][...] + w_refs[s][0:1, :] * slab(s, t)
            for k in range(1, width):
                acc = acc + w_refs[s][k:k + 1, :] * slab(s, t + k)
            outs.append(acc)
        o_ref[t] = post(outs, extras).astype(o_ref.dtype)
    for s in range(n_s):
        for i in range(past):
            ns_refs[s][i] = slab(s, i + steps)


def _conv_slab(u, state, w, bias, extras, post, *, width, ct, nbt, n_s, out_w, out_dtype, name):
    steps, nb, _ = u.shape
    past = width - 1
    nj = out_w // ct
    kern = functools.partial(_conv_slab_kernel, width=width, steps=steps, n_s=n_s, n_x=len(extras), post=post)
    in_specs = ([pl.BlockSpec((steps, nbt, ct), lambda i, j, s=s: (0, i, j + s * nj)) for s in range(n_s)]
                + [pl.BlockSpec((past, nbt, ct), lambda i, j, s=s: (0, i, j + s * nj)) for s in range(n_s)]
                + [pl.BlockSpec((width, ct), lambda i, j, s=s: (0, j + s * nj)) for s in range(n_s)]
                + [pl.BlockSpec((1, ct), lambda i, j, s=s: (0, j + s * nj)) for s in range(n_s)]
                + [pl.BlockSpec((1, ct), lambda i, j: (0, j)) for _ in extras])
    out_specs = ([pl.BlockSpec((steps, nbt, ct), lambda i, j: (0, i, j))]
                 + [pl.BlockSpec((past, nbt, ct), lambda i, j: (0, i, j)) for _ in range(n_s)])
    out_shape = ([jax.ShapeDtypeStruct((steps, nb, out_w), out_dtype)]
                 + [jax.ShapeDtypeStruct((past, nb, out_w), F32) for _ in range(n_s)])
    res = pl.pallas_call(
        kern,
        grid=(nb // nbt, nj),
        in_specs=in_specs,
        out_specs=out_specs,
        out_shape=out_shape,
        compiler_params=_params("parallel", "parallel"),
        name=name,
    )(*([u] * n_s), *([state] * n_s), *([w] * n_s), *([bias] * n_s), *extras)
    return res[0], res[1:]


def _ssd_kernel(*refs, q, conv):
    xbc_ref, dt_ref, zs_ref, h0_ref, alog_ref, dsk_ref, gn_ref, e_ref, et_ref = refs[:9]
    refs = refs[9:]
    if conv:
        stb_ref, cw_ref, cb_ref, y_ref, ht_ref, nsb_ref, h_scr, y_scr, win = refs[:9]
        pads = refs[9:]
    else:
        y_ref, ht_ref, h_scr, y_scr = refs[:4]
        pads = refs[4:]
    qk = SSD_KEYS
    c = pl.program_id(1)

    @pl.when(c == 0)
    def _():
        h_scr[...] = h0_ref[0]

    if conv:
        past = CONV_B - 1
        hist = SUBLANES

        @pl.when(c == 0)
        def _():
            win[hist - past:hist, :] = stb_ref[0]

        @pl.when(c > 0)
        def _():
            win[0:hist, :] = win[q:q + hist, :]

        win[hist:hist + q, :] = xbc_ref[0]
        pre = cb_ref[...] + cw_ref[0:1, :] * win[hist - past:hist - past + q, :]
        for k in range(1, CONV_B):
            pre = pre + cw_ref[k:k + 1, :] * win[hist - past + k:hist - past + k + q, :]
        xq = _silu(pre)
        nsb_ref[0] = win[hist + q - past:hist + q, :]
    else:
        xq = xbc_ref[0]

    if q == qk:
        xbc = xq
        dt = dt_ref[0]
    else:
        xpad, dtpad = pads
        xpad[...] = jnp.zeros_like(xpad)
        dtpad[...] = jnp.zeros_like(dtpad)
        xpad[0:q, :] = xq
        dtpad[0:q, :] = dt_ref[0]
        xbc = xpad[...]
        dt = dtpad[...]

    x = xbc[:, :D_B]
    bm = xbc[:, D_B:D_B + N_GROUPS * D_STATE].astype(BF16)
    cm = xbc[:q, D_B + N_GROUPS * D_STATE:].astype(BF16)
    a = -jnp.exp(alog_ref[...])
    d_a = dt * a

    row = lax.broadcasted_iota(jnp.int32, (qk, qk), 0)
    col = lax.broadcasted_iota(jnp.int32, (qk, qk), 1)
    tril = jnp.where(row >= col, 1.0, 0.0).astype(BF16)
    cum = sum(_dot(tril, p) for p in _split_bf16(d_a, 3))
    cum_t = cum.T
    cum_last = cum[qk - 1:qk, :]

    e = e_ref[...]

    def expand(v):
        return sum(_dot(p, e) for p in _split_bf16(v, 2))

    xdt = x * expand(dt)
    xw = (xdt * expand(jnp.exp(cum_last - cum))).astype(BF16)
    xdt = xdt.astype(BF16)
    ecum = expand(jnp.exp(cum[:q, :]))
    decay_t = jnp.broadcast_to(jnp.exp(cum_t[:, qk - 1:qk]), (LANES, D_STATE))
    decay_rows = sum(_dot(et_ref[...], p) for p in _split_bf16(decay_t, 2))

    causal = row[:q, :] >= col[:q, :]
    lane_head = lax.broadcasted_iota(jnp.int32, (q, GROUP_W), 1) // HEAD_DIM
    for g in range(N_GROUPS):
        bg = bm[:, g * D_STATE:(g + 1) * D_STATE]
        cg = cm[:, g * D_STATE:(g + 1) * D_STATE]
        cb = lax.dot_general(cg, bg, (((1,), (1,)), ((), ())), preferred_element_type=F32)
        xdt_g = xdt[:, g * GROUP_W:(g + 1) * GROUP_W]
        y_g = jnp.zeros((q, GROUP_W), F32)
        for r in range(HEADS_PER_GROUP):
            hd = g * HEADS_PER_GROUP + r
            seg = cum[:q, hd:hd + 1] - cum_t[hd:hd + 1, :]
            decay = jnp.where(causal, jnp.exp(jnp.minimum(seg, 0.0)), 0.0)
            y_r = _dot((cb * decay).astype(BF16), xdt_g)
            y_g = y_g + jnp.where(lane_head == r, y_r, 0.0)
        h_g = h_scr[g * GROUP_W:(g + 1) * GROUP_W, :]
        y_inter = lax.dot_general(cg, h_g.astype(BF16), (((1,), (1,)), ((), ())), preferred_element_type=F32)
        y_scr[:, g * GROUP_W:(g + 1) * GROUP_W] = y_g + y_inter * ecum[:, g * GROUP_W:(g + 1) * GROUP_W]
        s_g = lax.dot_general(xw[:, g * GROUP_W:(g + 1) * GROUP_W], bg, (((0,), (0,)), ((), ())),
                              preferred_element_type=F32)
        h_scr[g * GROUP_W:(g + 1) * GROUP_W, :] = decay_rows[g * GROUP_W:(g + 1) * GROUP_W, :] * h_g + s_g

    y = y_scr[...] + dsk_ref[...] * x[:q, :]
    y_ref[0] = _rms(y * zs_ref[0], gn_ref[...]).astype(y_ref.dtype)

    @pl.when(c == pl.num_programs(1) - 1)
    def _():
        ht_ref[0] = h_scr[...]


def _ssd(xbc, dt, zs, h0, alog, dsk, gn, e, et, conv_b=None, *, q):
    b, L, _ = xbc.shape
    seq = lambda w: pl.BlockSpec((1, q, w), lambda i, c: (i, c, 0))
    full = lambda arr: pl.BlockSpec(arr.shape, lambda i, c: (0,) * arr.ndim)
    state = pl.BlockSpec((1, D_B, D_STATE), lambda i, c: (i, 0, 0))
    cstate = pl.BlockSpec((1, CONV_B - 1, D_XBC), lambda i, c: (i, 0, 0))
    conv = conv_b is not None
    pads = [] if q == SSD_KEYS else [pltpu.VMEM((SSD_KEYS, D_XBC), F32), pltpu.VMEM((SSD_KEYS, LANES), F32)]
    return pl.pallas_call(
        functools.partial(_ssd_kernel, q=q, conv=conv),
        grid=(b, L // q),
        in_specs=[seq(D_XBC), seq(LANES), seq(D_B), state,
                  full(alog), full(dsk), full(gn), full(e), full(et)]
                 + ([cstate, full(conv_b[1]), full(conv_b[2])] if conv else []),
        out_specs=[seq(D_B), state] + ([cstate] if conv else []),
        out_shape=[jax.ShapeDtypeStruct((b, L, D_B), BF16),
                   jax.ShapeDtypeStruct((b, D_B, D_STATE), F32)]
                  + ([jax.ShapeDtypeStruct((b, CONV_B - 1, D_XBC), F32)] if conv else []),
        scratch_shapes=[pltpu.VMEM((D_B, D_STATE), F32), pltpu.VMEM((q, D_B), F32)]
                       + ([pltpu.VMEM((SUBLANES + q, D_XBC), F32)] if conv else []) + pads,
        compiler_params=_params("parallel", "arbitrary"),
        name="ssd",
    )(xbc, dt, zs, h0, alog, dsk, gn, e, et, *(conv_b or ()))


def _layer(x, wts, conv_a, ssd_fn, ffn_fn):
    h = _norm(x, wts["g_pre1"])
    uglu = _proj(h, [wts["w_val"], wts["w_gate"]], [], _epi_glu, 512, F32, "in_glu")
    dt = _proj(h, [wts["w_dt"]], [wts["dt_bias"]], _epi_dt, LANES, F32, "in_dt")
    zs = _proj(h, [wts["w_z"]], [], _epi_silu, 512, F32, "in_z")
    xbc = _proj(h, [wts["w_xbc"]], [], _epi_id, 512, F32, "in_xbc")
    gates = _proj(h, [wts["w_g"]], [wts["b_gate"]], _epi_gate, 512, F32, "in_gates")

    ua, st_a = conv_a(uglu)
    yb, st_b, st_h = ssd_fn(xbc, dt, zs)

    x1 = _merge(ua, yb, gates, x, wts["w_a_out"], wts["b_a_out"], wts["w_b_out"], wts["w_o"], wts["g_post1"])
    x2, st_f = ffn_fn(x1)
    return x2, st_a, st_b, st_h, st_f


def _seq_group(x, wts, st_a, st_b, st_h, st_f, tl, q):
    b, L, _ = x.shape

    def conv_a(uglu):
        ua, (ns,) = _conv_seq(uglu.reshape(b, L, D_A), st_a, wts["w_dw_a"], wts["b_dw_a"],
                              [wts["g_ln_a"], wts["b_ln_a"]], _post_ln_silu, width=CONV_A, ct=D_A,
                              tl=min(tl, 256), n_s=1, out_w=D_A, out_dtype=BF16, name="conv_a")
        return ua.reshape(b * L, D_A), ns

    def ssd_fn(xbc, dt, zs):
        yb, ht, ns = _ssd(xbc.reshape(b, L, D_XBC), dt.reshape(b, L, LANES), zs.reshape(b, L, D_B), st_h,
                          wts["a_log"], wts["d_skip"], wts["g_norm_b"], wts["expand"], wts["expand_t"],
                          (st_b, wts["w_dw_b"], wts["b_dw_b"]), q=q)
        return yb.reshape(b * L, D_B), ns, ht

    def ffn_fn(x1):
        x2, ns = _ffn(x1.reshape(b, L, D_MODEL), st_f, wts, tm=tl, sh=1)
        return x2.reshape(b * L, D_MODEL), ns

    x2, ns_a, ns_b, ns_h, ns_f = _layer(x.reshape(b * L, D_MODEL), wts, conv_a, ssd_fn, ffn_fn)
    return x2.reshape(b, L, D_MODEL), ns_a, ns_b, ns_h, ns_f


def _step_group(x, wts, st_a, st_b, st_h, st_f):
    nb, L, _ = x.shape
    tmaj = lambda s: jnp.transpose(s, (1, 0, 2))
    lpad = -(-L // SUBLANES) * SUBLANES

    def conv_a(uglu):
        ua, (ns,) = _conv_slab(uglu.reshape(L, nb, D_A), tmaj(st_a), wts["w_dw_a"], wts["b_dw_a"],
                               [wts["g_ln_a"], wts["b_ln_a"]], _post_ln_silu, width=CONV_A, ct=D_A,
                               nbt=32, n_s=1, out_w=D_A, out_dtype=BF16, name="conv_a_step")
        return ua.reshape(L * nb, D_A), tmaj(ns)

    def bmaj_pad(v):
        return jnp.pad(tmaj(v), ((0, 0), (0, lpad - L), (0, 0)))

    def ssd_fn(xbc, dt, zs):
        act, (ns,) = _conv_slab(xbc.reshape(L, nb, D_XBC), tmaj(st_b), wts["w_dw_b"], wts["b_dw_b"], [],
                                _post_silu, width=CONV_B, ct=512, nbt=nb, n_s=1, out_w=D_XBC,
                                out_dtype=F32, name="conv_b_step")
        yb, ht = _ssd(bmaj_pad(act), bmaj_pad(dt.reshape(L, nb, LANES)), bmaj_pad(zs.reshape(L, nb, D_B)),
                      st_h, wts["a_log"], wts["d_skip"], wts["g_norm_b"], wts["expand"], wts["expand_t"], q=lpad)
        return tmaj(yb[:, :L]).reshape(L * nb, D_B), tmaj(ns), ht

    def ffn_fn(x1):
        past = CONV_F - 1
        x2, ns = _ffn(x1.reshape(1, L * nb, D_MODEL), tmaj(st_f).reshape(1, past * nb, 2 * D_FF), wts,
                      tm=L * nb, sh=nb)
        return x2.reshape(L * nb, D_MODEL), tmaj(ns.reshape(past, nb, 2 * D_FF))

    x2, ns_a, ns_b, ns_h, ns_f = _layer(tmaj(x).reshape(L * nb, D_MODEL), wts, conv_a, ssd_fn, ffn_fn)
    return tmaj(x2.reshape(L, nb, D_MODEL)), ns_a, ns_b, ns_h, ns_f


def _layer_weights(l, g_pre1, g_post1, w_in, b_gate, w_dw_a, b_dw_a, g_ln_a, b_ln_a, w_a_out, b_a_out,
                   w_dw_b, b_dw_b, dt_bias, a_log, d_skip, g_norm_b, w_b_out, w_o,
                   g_pre2, g_post2, w_up, w_dw_f, b_dw_f, w_down):
    row = lambda v: v[l].reshape(1, -1)
    lane_pad = lambda v: jnp.pad(v, ((0, 0), (0, LANES - v.shape[1])))
    w = w_in[l]
    head = jnp.arange(LANES, dtype=jnp.int32)[:, None]
    chan_head = (jnp.arange(D_B, dtype=jnp.int32) // HEAD_DIM)[None, :]
    expand = (head == chan_head).astype(BF16)
    return {
        "g_pre1": row(g_pre1), "g_post1": row(g_post1), "g_pre2": row(g_pre2), "g_post2": row(g_post2),
        "w_val": w[:, :D_A].astype(BF16), "w_gate": w[:, D_A:COL_Z].astype(BF16),
        "w_z": w[:, COL_Z:COL_XBC].astype(BF16), "w_xbc": w[:, COL_XBC:COL_DT].astype(BF16),
        "w_dt": lane_pad(w[:, COL_DT:COL_GATE]).astype(BF16), "w_g": w[:, COL_GATE:].astype(BF16),
        "dt_bias": lane_pad(row(dt_bias)), "b_gate": row(b_gate),
        "w_dw_a": w_dw_a[l], "b_dw_a": row(b_dw_a), "g_ln_a": row(g_ln_a), "b_ln_a": row(b_ln_a),
        "w_a_out": w_a_out[l].astype(BF16), "b_a_out": row(b_a_out),
        "w_dw_b": w_dw_b[l], "b_dw_b": row(b_dw_b),
        "a_log": lane_pad(row(a_log)), "d_skip": jnp.repeat(d_skip[l], HEAD_DIM).reshape(1, D_B),
        "g_norm_b": row(g_norm_b), "expand": expand, "expand_t": expand.T,
        "w_b_out": w_b_out[l].astype(BF16), "w_o": w_o[l].astype(BF16),
        "w_up": w_up[l].astype(BF16), "w_dw_f": w_dw_f[l], "b_dw_f": row(b_dw_f),
        "w_down": w_down[l].astype(BF16),
    }


def kernel(x_prompt, x_sample, state_conv_a, state_conv_b, state_ssm, state_conv_ffn, meta_tokens, g_pre1, g_post1, w_in, b_gate, w_dw_a, b_dw_a, g_ln_a, b_ln_a, w_a_out, b_a_out, w_dw_b, b_dw_b, dt_bias, a_log, d_skip, g_norm_b, w_b_out, w_o, g_pre2, g_post2, w_up, w_dw_f, b_dw_f, w_down):
    depth = w_in.shape[0]
    bp = x_prompt.shape[0]
    nb = x_sample.shape[0]
    xm = meta_tokens.astype(x_prompt.dtype)[None]
    xp, xs = x_prompt, x_sample
    pa, pb, ph, pf = [], [], [], []
    sa, sb, sh, sf = [], [], [], []
    for l in range(depth):
        wts = _layer_weights(l, g_pre1, g_post1, w_in, b_gate, w_dw_a, b_dw_a, g_ln_a, b_ln_a, w_a_out, b_a_out,
                             w_dw_b, b_dw_b, dt_bias, a_log, d_skip, g_norm_b, w_b_out, w_o,
                             g_pre2, g_post2, w_up, w_dw_f, b_dw_f, w_down)
        xm, m_a, m_b, m_h, m_f = _seq_group(
            xm, wts, jnp.zeros((1, CONV_A - 1, D_A), F32), jnp.zeros((1, CONV_B - 1, D_XBC), F32),
            jnp.zeros((1, D_B, D_STATE), F32), jnp.zeros((1, CONV_F - 1, 2 * D_FF), F32), tl=N_META, q=SUBLANES)
        rep = lambda s: jnp.broadcast_to(s, (bp,) + s.shape[1:])
        xp, c_a, c_b, c_h, c_f = _seq_group(xp, wts, rep(m_a), rep(m_b), rep(m_h), rep(m_f), tl=512, q=SSD_KEYS)
        pa.append(c_a); pb.append(c_b); ph.append(c_h.reshape(bp, N_HEADS, HEAD_DIM, D_STATE)); pf.append(c_f)
        xs, d_a, d_b, d_h, d_f = _step_group(
            xs, wts, state_conv_a[l], state_conv_b[l], state_ssm[l].reshape(nb, D_B, D_STATE), state_conv_ffn[l])
        sa.append(d_a); sb.append(d_b); sh.append(d_h.reshape(nb, N_HEADS, HEAD_DIM, D_STATE)); sf.append(d_f)
    return (xp, xs, jnp.stack(pa), jnp.stack(pb), jnp.stack(ph), jnp.stack(pf),
            jnp.stack(sa), jnp.stack(sb), jnp.stack(sh), jnp.stack(sf))
```

```python
import functools

import jax
import jax.numpy as jnp
from jax import lax
from jax.experimental import pallas as pl
from jax.experimental.pallas import tpu as pltpu

D_MODEL = 2048
N_META = 16
D_A = 1024
CONV_A = 31
D_B = 2048
HEAD_DIM = 64
N_HEADS = D_B // HEAD_DIM
N_GROUPS = 8
HEADS_PER_GROUP = N_HEADS // N_GROUPS
GROUP_W = HEADS_PER_GROUP * HEAD_DIM
D_STATE = 128
CONV_B = 4
D_FF = 5632
CONV_F = 3
EPS = 1e-6
D_XBC = D_B + 2 * N_GROUPS * D_STATE
COL_Z = 2 * D_A
COL_XBC = COL_Z + D_B
COL_DT = COL_XBC + D_XBC
COL_GATE = COL_DT + N_HEADS

LANES = 128
SUBLANES = 8
SSD_KEYS = 128
VMEM_LIMIT = 56 * 1024 * 1024

F32 = jnp.float32
BF16 = jnp.bfloat16


def _params(*sem):
    return pltpu.CompilerParams(dimension_semantics=sem, vmem_limit_bytes=VMEM_LIMIT)


def _sigmoid(x):
    return 1.0 / (1.0 + jnp.exp(-x))


def _silu(x):
    return x * _sigmoid(x)


def _softplus(x):
    return jnp.maximum(x, 0.0) + jnp.log1p(jnp.exp(-jnp.abs(x)))


def _gelu_tanh(x):
    return 0.5 * x * (1.0 + jnp.tanh(0.7978845608028654 * (x + 0.044715 * (x * x * x))))


def _rms(x, g):
    r = lax.rsqrt(jnp.mean(x * x, axis=-1, keepdims=True) + EPS)
    return x * r * g


def _dot(a, b):
    return jnp.dot(a, b, preferred_element_type=F32)


def _split_bf16(v, parts):
    out = []
    for _ in range(parts):
        p = v.astype(BF16)
        out.append(p)
        v = v - p.astype(F32)
    return out


def _norm_kernel(x_ref, g_ref, o_ref):
    o_ref[...] = _rms(x_ref[...], g_ref[...]).astype(o_ref.dtype)


def _norm(x, g):
    rows, d = x.shape
    tm = min(rows, 512)
    return pl.pallas_call(
        _norm_kernel,
        grid=(rows // tm,),
        in_specs=[pl.BlockSpec((tm, d), lambda i: (i, 0)), pl.BlockSpec((1, d), lambda i: (0, 0))],
        out_specs=pl.BlockSpec((tm, d), lambda i: (i, 0)),
        out_shape=jax.ShapeDtypeStruct((rows, d), BF16),
        compiler_params=_params("parallel"),
        name="norm",
    )(x, g)


def _proj_kernel(*refs, n_w, n_b, epilogue):
    h_ref = refs[0]
    w_refs = refs[1:1 + n_w]
    b_refs = refs[1 + n_w:1 + n_w + n_b]
    o_ref = refs[-1]
    h = h_ref[...]
    accs = [_dot(h, w[...]) for w in w_refs]
    o_ref[...] = epilogue(accs, [b[...] for b in b_refs]).astype(o_ref.dtype)


def _proj(h, ws, bs, epilogue, tn, out_dtype, name, n=None, cols=None):
    rows, d = h.shape
    n = n or ws[0].shape[1]
    cols = cols or [0] * len(ws)
    tm = min(rows, 1024)
    kern = functools.partial(_proj_kernel, n_w=len(ws), n_b=len(bs), epilogue=epilogue)
    return pl.pallas_call(
        kern,
        grid=(rows // tm, n // tn),
        in_specs=[pl.BlockSpec((tm, d), lambda i, j: (i, 0))]
                 + [pl.BlockSpec((d, tn), lambda i, j, c=c // tn: (0, j + c)) for c in cols]
                 + [pl.BlockSpec((1, tn), lambda i, j: (0, j)) for _ in bs],
        out_specs=pl.BlockSpec((tm, tn), lambda i, j: (i, j)),
        out_shape=jax.ShapeDtypeStruct((rows, n), out_dtype),
        compiler_params=_params("parallel", "parallel"),
        name=name,
    )(h, *ws, *bs)


def _epi_glu(a, b):
    return a[0] * _sigmoid(a[1])


def _epi_dt(a, b):
    lane = lax.broadcasted_iota(jnp.int32, a[0].shape, 1)
    return jnp.where(lane < N_HEADS, _softplus(a[0] + b[0]), 0.0)


def _epi_silu(a, b):
    return _silu(a[0])


def _epi_id(a, b):
    return a[0]


def _epi_gate(a, b):
    return _sigmoid(a[0] + b[0])


def _merge_kernel(ua_ref, yb_ref, gt_ref, x_ref, wa_ref, ba_ref, wb_ref, wo_ref, gp_ref, o_ref):
    ya = _dot(ua_ref[...], wa_ref[...]) + ba_ref[...]
    yb = _dot(yb_ref[...], wb_ref[...])
    mix = gt_ref[:, :D_MODEL] * ya + gt_ref[:, D_MODEL:] * yb
    m = _dot(mix.astype(BF16), wo_ref[...])
    o_ref[...] = x_ref[...] + _rms(m, gp_ref[...])


def _merge(ua, yb, gates, x, wa, ba, wb, wo, gp):
    rows = x.shape[0]
    tm = min(rows, 256)
    row_spec = lambda w: pl.BlockSpec((tm, w), lambda i: (i, 0))
    res_spec = lambda a: pl.BlockSpec(a.shape, lambda i: (0, 0), pipeline_mode=pl.Buffered(1))
    return pl.pallas_call(
        _merge_kernel,
        grid=(rows // tm,),
        in_specs=[row_spec(D_A), row_spec(D_B), row_spec(2 * D_MODEL), row_spec(D_MODEL),
                  res_spec(wa), res_spec(ba), res_spec(wb), res_spec(wo), res_spec(gp)],
        out_specs=row_spec(D_MODEL),
        out_shape=jax.ShapeDtypeStruct((rows, D_MODEL), F32),
        compiler_params=_params("parallel"),
        name="merge",
    )(ua, yb, gates, x, wa, ba, wb, wo, gp)


def _ffn_kernel(x1_ref, gpre_ref, wg_ref, wv_ref, wd_ref, stg_ref, stv_ref, cwg_ref, cwv_ref, cbg_ref, cbv_ref,
                gpost_ref, o_ref, nsg_ref, nsv_ref, h_scr, acc_scr, act_scr, win_g, win_v, *carry,
                tm, sh, hist, rc):
    l = pl.program_id(1)
    j = pl.program_id(2)
    past = (CONV_F - 1) * sh

    @pl.when(j == 0)
    def _():
        h_scr[...] = _rms(x1_ref[0], gpre_ref[...]).astype(BF16)
        acc_scr[...] = jnp.zeros_like(acc_scr)

    h = h_scr[...]
    for s, (st_ref, win) in enumerate(((stg_ref, win_g), (stv_ref, win_v))):
        if carry:
            @pl.when(l == 0)
            def _(win=win, st_ref=st_ref):
                win[hist - past:hist, :] = st_ref[0]

            @pl.when(l > 0)
            def _(win=win, s=s):
                win[0:hist, :] = carry[0][j, s]
        else:
            win[hist - past:hist, :] = st_ref[0]

    def conv(win, cw_ref, cb_ref, r0, cs):
        out = cb_ref[:, cs]
        for k in range(CONV_F):
            lo = hist - (CONV_F - 1 - k) * sh + r0
            out = out + cw_ref[k:k + 1, cs] * win[lo:lo + rc, cs]
        return out

    tf = win_g.shape[1]
    cw = min(tf, 256)
    for c0 in range(0, tf, cw):
        cs = slice(c0, c0 + cw)
        win_g[hist:hist + tm, cs] = _dot(h, wg_ref[:, cs])
        win_v[hist:hist + tm, cs] = _dot(h, wv_ref[:, cs])
        for r0 in range(0, tm, rc):
            gate = _gelu_tanh(conv(win_g, cwg_ref, cbg_ref, r0, cs))
            act_scr[r0:r0 + rc, cs] = (gate * conv(win_v, cwv_ref, cbv_ref, r0, cs)).astype(BF16)
    acc_scr[...] += _dot(act_scr[...], wd_ref[...])

    for s, (ns_ref, win) in enumerate(((nsg_ref, win_g), (nsv_ref, win_v))):
        ns_ref[0, 0] = win[hist + tm - past:hist + tm, :]
        if carry:
            carry[0][j, s] = win[tm:tm + hist, :]

    @pl.when(j == pl.num_programs(2) - 1)
    def _():
        o_ref[0] = x1_ref[0] + _rms(acc_scr[...], gpost_ref[...])


def _ffn(x1, state, wts, *, tm, sh):
    b, L, _ = x1.shape
    tf = 512
    nj = D_FF // tf
    past = (CONV_F - 1) * sh
    hist = -(-past // SUBLANES) * SUBLANES
    n_l = L // tm
    carry = [pltpu.VMEM((nj, 2, hist, tf), F32)] if n_l > 1 else []
    col = lambda rows, off: pl.BlockSpec((rows, tf), lambda i, l, j: (0, j + off))
    st = lambda off: pl.BlockSpec((1, past, tf), lambda i, l, j: (i, 0, j + off))
    tail = pl.BlockSpec((1, 1, past, tf), lambda i, l, j: (i, l, 0, j))
    vec = pl.BlockSpec((1, D_MODEL), lambda i, l, j: (0, 0))
    xblk = pl.BlockSpec((1, tm, D_MODEL), lambda i, l, j: (i, l, 0))
    out, ns_g, ns_v = pl.pallas_call(
        functools.partial(_ffn_kernel, tm=tm, sh=sh, hist=hist, rc=min(tm, 32)),
        grid=(b, n_l, nj),
        in_specs=[xblk, vec, col(D_MODEL, 0), col(D_MODEL, nj),
                  pl.BlockSpec((tf, D_MODEL), lambda i, l, j: (j, 0)),
                  st(0), st(nj), col(CONV_F, 0), col(CONV_F, nj), col(1, 0), col(1, nj), vec],
        out_specs=[xblk, tail, tail],
        out_shape=[jax.ShapeDtypeStruct((b, L, D_MODEL), F32),
                   jax.ShapeDtypeStruct((b, n_l, past, D_FF), F32),
                   jax.ShapeDtypeStruct((b, n_l, past, D_FF), F32)],
        scratch_shapes=[pltpu.VMEM((tm, D_MODEL), BF16), pltpu.VMEM((tm, D_MODEL), F32),
                        pltpu.VMEM((tm, tf), BF16),
                        pltpu.VMEM((hist + tm, tf), F32), pltpu.VMEM((hist + tm, tf), F32)] + carry,
        compiler_params=_params("parallel", "arbitrary", "arbitrary"),
        name="ffn",
    )(x1, wts["g_pre2"], wts["w_up"], wts["w_up"], wts["w_down"], state, state,
      wts["w_dw_f"], wts["w_dw_f"], wts["b_dw_f"], wts["b_dw_f"], wts["g_post2"])
    return out, jnp.concatenate([ns_g[:, -1], ns_v[:, -1]], axis=-1)


def _post_ln_silu(outs, extras):
    u = outs[0]
    mu = jnp.mean(u, axis=-1, keepdims=True)
    xc = u - mu
    r = lax.rsqrt(jnp.mean(xc * xc, axis=-1, keepdims=True) + EPS)
    return _silu(xc * r * extras[0] + extras[1])


def _post_silu(outs, extras):
    return _silu(outs[0])


def _conv_seq_kernel(*refs, width, tl, n_s, n_x, post, hist, rc):
    u_refs = refs[0:n_s]
    st_refs = refs[n_s:2 * n_s]
    w_refs = refs[2 * n_s:3 * n_s]
    b_refs = refs[3 * n_s:4 * n_s]
    x_refs = refs[4 * n_s:4 * n_s + n_x]
    o_ref = refs[4 * n_s + n_x]
    ns_refs = refs[4 * n_s + n_x + 1:4 * n_s + n_x + 1 + n_s]
    win_refs = refs[4 * n_s + n_x + 1 + n_s:]
    past = width - 1
    l = pl.program_id(2)

    for s in range(n_s):
        win = win_refs[s]

        @pl.when(l == 0)
        def _(win=win, s=s):
            if hist > past:
                win[0:hist - past, :] = jnp.zeros((hist - past, win.shape[1]), F32)
            win[hist - past:hist, :] = st_refs[s][0]

        @pl.when(l > 0)
        def _(win=win):
            win[0:hist, :] = win[tl:tl + hist, :]

        win[hist:hist + tl, :] = u_refs[s][0]

    extras = [x[...] for x in x_refs]
    for c in range(tl // rc):
        outs = []
        for s in range(n_s):
            acc = b_refs[s][...] + w_refs[s][0:1, :] * win_refs[s][c * rc + hist - past:c * rc + hist - past + rc, :]
            for k in range(1, width):
                lo = c * rc + hist - past + k
                acc = acc + w_refs[s][k:k + 1, :] * win_refs[s][lo:lo + rc, :]
            outs.append(acc)
        o_ref[0, c * rc:(c + 1) * rc, :] = post(outs, extras).astype(o_ref.dtype)

    @pl.when(l == pl.num_programs(2) - 1)
    def _():
        for s in range(n_s):
            ns_refs[s][0] = win_refs[s][hist + tl - past:hist + tl, :]


def _conv_seq(u, state, w, bias, extras, post, *, width, ct, tl, n_s, out_w, out_dtype, name):
    b, L, _ = u.shape
    past = width - 1
    hist = -(-past // SUBLANES) * SUBLANES
    nj = out_w // ct
    rc = min(tl, 32)
    kern = functools.partial(_conv_seq_kernel, width=width, tl=tl, n_s=n_s, n_x=len(extras),
                             post=post, hist=hist, rc=rc)
    in_specs = ([pl.BlockSpec((1, tl, ct), lambda i, j, l, s=s: (i, l, j + s * nj)) for s in range(n_s)]
                + [pl.BlockSpec((1, past, ct), lambda i, j, l, s=s: (i, 0, j + s * nj)) for s in range(n_s)]
                + [pl.BlockSpec((width, ct), lambda i, j, l, s=s: (0, j + s * nj)) for s in range(n_s)]
                + [pl.BlockSpec((1, ct), lambda i, j, l, s=s: (0, j + s * nj)) for s in range(n_s)]
                + [pl.BlockSpec((1, ct), lambda i, j, l: (0, j)) for _ in extras])
    out_specs = ([pl.BlockSpec((1, tl, ct), lambda i, j, l: (i, l, j))]
                 + [pl.BlockSpec((1, past, ct), lambda i, j, l: (i, 0, j)) for _ in range(n_s)])
    out_shape = ([jax.ShapeDtypeStruct((b, L, out_w), out_dtype)]
                 + [jax.ShapeDtypeStruct((b, past, out_w), F32) for _ in range(n_s)])
    res = pl.pallas_call(
        kern,
        grid=(b, nj, L // tl),
        in_specs=in_specs,
        out_specs=out_specs,
        out_shape=out_shape,
        scratch_shapes=[pltpu.VMEM((hist + tl, ct), F32) for _ in range(n_s)],
        compiler_params=_params("parallel", "parallel", "arbitrary"),
        name=name,
    )(*([u] * n_s), *([state] * n_s), *([w] * n_s), *([bias] * n_s), *extras)
    return res[0], res[1:]


def _conv_slab_kernel(*refs, width, steps, n_s, n_x, post):
    u_refs = refs[0:n_s]
    st_refs = refs[n_s:2 * n_s]
    w_refs = refs[2 * n_s:3 * n_s]
    b_refs = refs[3 * n_s:4 * n_s]
    x_refs = refs[4 * n_s:4 * n_s + n_x]
    o_ref = refs[4 * n_s + n_x]
    ns_refs = refs[4 * n_s + n_x + 1:]
    past = width - 1

    def slab(s, i):
        return st_refs[s][i] if i < past else u_refs[s][i - past]

    extras = [x[...] for x in x_refs]
    for t in range(steps):
        outs = []
        for s in range(n_s):
            acc = b_refs[s][...] + w_refs[s][0:1, :] * slab(s, t)
            for k in range(1, width):
                acc = acc + w_refs[s][k:k + 1, :] * slab(s, t + k)
            outs.append(acc)
        o_ref[t] = post(outs, extras).astype(o_ref.dtype)
    for s in range(n_s):
        for i in range(past):
            ns_refs[s][i] = slab(s, i + steps)


def _conv_slab(u, state, w, bias, extras, post, *, width, ct, nbt, n_s, out_w, out_dtype, name):
    steps, nb, _ = u.shape
    past = width - 1
    nj = out_w // ct
    kern = functools.partial(_conv_slab_kernel, width=width, steps=steps, n_s=n_s, n_x=len(extras), post=post)
    in_specs = ([pl.BlockSpec((steps, nbt, ct), lambda i, j, s=s: (0, i, j + s * nj)) for s in range(n_s)]
                + [pl.BlockSpec((past, nbt, ct), lambda i, j, s=s: (0, i, j + s * nj)) for s in range(n_s)]
                + [pl.BlockSpec((width, ct), lambda i, j, s=s: (0, j + s * nj)) for s in range(n_s)]
                + [pl.BlockSpec((1, ct), lambda i, j, s=s: (0, j + s * nj)) for s in range(n_s)]
                + [pl.BlockSpec((1, ct), lambda i, j: (0, j)) for _ in extras])
    out_specs = ([pl.BlockSpec((steps, nbt, ct), lambda i, j: (0, i, j))]
                 + [pl.BlockSpec((past, nbt, ct), lambda i, j: (0, i, j)) for _ in range(n_s)])
    out_shape = ([jax.ShapeDtypeStruct((steps, nb, out_w), out_dtype)]
                 + [jax.ShapeDtypeStruct((past, nb, out_w), F32) for _ in range(n_s)])
    res = pl.pallas_call(
        kern,
        grid=(nb // nbt, nj),
        in_specs=in_specs,
        out_specs=out_specs,
        out_shape=out_shape,
        compiler_params=_params("parallel", "parallel"),
        name=name,
    )(*([u] * n_s), *([state] * n_s), *([w] * n_s), *([bias] * n_s), *extras)
    return res[0], res[1:]


def _ssd_kernel(*refs, q, bb, conv):
    xbc_ref, dt_ref, zs_ref, h0_ref, alog_ref, dsk_ref, gn_ref, e_ref, et_ref = refs[:9]
    if conv:
        stb_ref, cw_ref, cb_ref, y_ref, ht_ref, nsb_ref, h_scr, y_scr, win = refs[9:]
    else:
        y_ref, ht_ref, h_scr, y_scr = refs[9:]
    qk = SSD_KEYS
    c = pl.program_id(1)

    @pl.when(c == 0)
    def _():
        h_scr[...] = h0_ref[...]

    a = -jnp.exp(alog_ref[...])
    e = e_ref[...]
    row = lax.broadcasted_iota(jnp.int32, (qk, qk), 0)
    col = lax.broadcasted_iota(jnp.int32, (qk, qk), 1)
    tril = jnp.where(row >= col, 1.0, 0.0).astype(BF16)
    causal = row[:q, :] >= col[:q, :]
    key_head = lax.broadcasted_iota(jnp.int32, (qk, GROUP_W), 1) // HEAD_DIM

    def expand(v):
        return sum(_dot(p, e) for p in _split_bf16(v, 2))

    def pad_keys(v):
        if q == qk:
            return v
        return jnp.concatenate([v, jnp.zeros((qk - q, v.shape[1]), v.dtype)], axis=0)

    for s in range(bb):
        if conv:
            past = CONV_B - 1
            hist = SUBLANES

            @pl.when(c == 0)
            def _(s=s):
                win[s, hist - past:hist, :] = stb_ref[s]

            @pl.when(c > 0)
            def _(s=s):
                win[s, 0:hist, :] = win[s, q:q + hist, :]

            win[s, hist:hist + q, :] = xbc_ref[s]
            pre = cb_ref[...] + cw_ref[0:1, :] * win[s, hist - past:hist - past + q, :]
            for k in range(1, CONV_B):
                pre = pre + cw_ref[k:k + 1, :] * win[s, hist - past + k:hist - past + k + q, :]
            xq = _silu(pre)
            nsb_ref[s] = win[s, hist + q - past:hist + q, :]
        else:
            xq = xbc_ref[s]

        x = xq[:, :D_B]
        bm = pad_keys(xq[:, D_B:D_B + N_GROUPS * D_STATE]).astype(BF16)
        cm = xq[:, D_B + N_GROUPS * D_STATE:].astype(BF16)
        dt = dt_ref[s]
        d_a = pad_keys(dt * a)
        cum = sum(_dot(tril, p) for p in _split_bf16(d_a, 3))
        cum_t = cum.T
        cum_last = cum[qk - 1:qk, :]
        cum_q = cum[:q, :]

        xdt = x * expand(dt)
        xw = pad_keys(xdt * expand(jnp.exp(cum_last - cum_q))).astype(BF16)
        xdt = pad_keys(xdt).astype(BF16)
        ecum = expand(jnp.exp(cum_q))
        decay_t = jnp.broadcast_to(jnp.exp(cum_t[:, qk - 1:qk]), (LANES, D_STATE))
        decay_rows = sum(_dot(et_ref[...], p) for p in _split_bf16(decay_t, 2))

        for g in range(N_GROUPS):
            cols = slice(g * GROUP_W, (g + 1) * GROUP_W)
            bg = bm[:, g * D_STATE:(g + 1) * D_STATE]
            cg = cm[:, g * D_STATE:(g + 1) * D_STATE]
            cb = lax.dot_general(cg, bg, (((1,), (1,)), ((), ())), preferred_element_type=F32)
            xdt_g = xdt[:, cols]
            scores, keys = [], []
            for r in range(HEADS_PER_GROUP):
                hd = g * HEADS_PER_GROUP + r
                seg = jnp.where(causal, cum_q[:, hd:hd + 1] - cum_t[hd:hd + 1, :], -1e30)
                scores.append((cb * jnp.exp(seg)).astype(BF16))
                keys.append(jnp.where(key_head == r, xdt_g, jnp.zeros_like(xdt_g)))
            y_g = _dot(jnp.concatenate(scores, axis=1), jnp.concatenate(keys, axis=0))
            h_g = h_scr[s, cols, :]
            y_inter = lax.dot_general(cg, h_g.astype(BF16), (((1,), (1,)), ((), ())),
                                      preferred_element_type=F32)
            y_scr[s, :, cols] = y_g + y_inter * ecum[:, cols]
            s_g = lax.dot_general(xw[:, cols], bg, (((0,), (0,)), ((), ())), preferred_element_type=F32)
            h_scr[s, cols, :] = decay_rows[cols, :] * h_g + s_g

        y = y_scr[s] + dsk_ref[...] * x
        y_ref[s] = _rms(y * zs_ref[s], gn_ref[...]).astype(y_ref.dtype)

    @pl.when(c == pl.num_programs(1) - 1)
    def _():
        ht_ref[...] = h_scr[...]


def _ssd(xbc, dt, zs, h0, alog, dsk, gn, e, et, conv_b=None, *, q, bb):
    b, L, _ = xbc.shape
    seq = lambda w: pl.BlockSpec((bb, q, w), lambda i, c: (i, c, 0))
    full = lambda arr: pl.BlockSpec(arr.shape, lambda i, c: (0,) * arr.ndim)
    state = pl.BlockSpec((bb, D_B, D_STATE), lambda i, c: (i, 0, 0))
    cstate = pl.BlockSpec((bb, CONV_B - 1, D_XBC), lambda i, c: (i, 0, 0))
    conv = conv_b is not None
    return pl.pallas_call(
        functools.partial(_ssd_kernel, q=q, bb=bb, conv=conv),
        grid=(b // bb, L // q),
        in_specs=[seq(D_XBC), seq(LANES), seq(D_B), state,
                  full(alog), full(dsk), full(gn), full(e), full(et)]
                 + ([cstate, full(conv_b[1]), full(conv_b[2])] if conv else []),
        out_specs=[seq(D_B), state] + ([cstate] if conv else []),
        out_shape=[jax.ShapeDtypeStruct((b, L, D_B), BF16),
                   jax.ShapeDtypeStruct((b, D_B, D_STATE), F32)]
                  + ([jax.ShapeDtypeStruct((b, CONV_B - 1, D_XBC), F32)] if conv else []),
        scratch_shapes=[pltpu.VMEM((bb, D_B, D_STATE), F32), pltpu.VMEM((bb, q, D_B), F32)]
                       + ([pltpu.VMEM((bb, SUBLANES + q, D_XBC), F32)] if conv else []),
        compiler_params=_params("parallel", "arbitrary"),
        name="ssd",
    )(xbc, dt, zs, h0, alog, dsk, gn, e, et, *(conv_b or ()))


def _layer(x, wts, conv_a, ssd_fn, ffn_fn):
    h = _norm(x, wts["g_pre1"])
    w_main = wts["w_main"]
    uglu = _proj(h, [w_main, w_main], [], _epi_glu, 512, F32, "in_glu", n=D_A, cols=[0, D_A])
    dt = _proj(h, [wts["w_dt"]], [wts["dt_bias"]], _epi_dt, LANES, F32, "in_dt")
    zs = _proj(h, [w_main], [], _epi_silu, 512, F32, "in_z", n=D_B, cols=[COL_Z])
    xbc = _proj(h, [w_main], [], _epi_id, 512, F32, "in_xbc", n=D_XBC, cols=[COL_XBC])
    gates = _proj(h, [wts["w_g"]], [wts["b_gate"]], _epi_gate, 512, F32, "in_gates")

    ua, st_a = conv_a(uglu)
    yb, st_b, st_h = ssd_fn(xbc, dt, zs)

    x1 = _merge(ua, yb, gates, x, wts["w_a_out"], wts["b_a_out"], wts["w_b_out"], wts["w_o"], wts["g_post1"])
    x2, st_f = ffn_fn(x1)
    return x2, st_a, st_b, st_h, st_f


def _seq_group(x, wts, st_a, st_b, st_h, st_f, tl, q):
    b, L, _ = x.shape

    def conv_a(uglu):
        ua, (ns,) = _conv_seq(uglu.reshape(b, L, D_A), st_a, wts["w_dw_a"], wts["b_dw_a"],
                              [wts["g_ln_a"], wts["b_ln_a"]], _post_ln_silu, width=CONV_A, ct=D_A,
                              tl=min(tl, 256), n_s=1, out_w=D_A, out_dtype=BF16, name="conv_a")
        return ua.reshape(b * L, D_A), ns

    def ssd_fn(xbc, dt, zs):
        yb, ht, ns = _ssd(xbc.reshape(b, L, D_XBC), dt.reshape(b, L, LANES), zs.reshape(b, L, D_B), st_h,
                          wts["a_log"], wts["d_skip"], wts["g_norm_b"], wts["expand"], wts["expand_t"],
                          (st_b, wts["w_dw_b"], wts["b_dw_b"]), q=q, bb=1)
        return yb.reshape(b * L, D_B), ns, ht

    def ffn_fn(x1):
        x2, ns = _ffn(x1.reshape(b, L, D_MODEL), st_f, wts, tm=tl, sh=1)
        return x2.reshape(b * L, D_MODEL), ns

    x2, ns_a, ns_b, ns_h, ns_f = _layer(x.reshape(b * L, D_MODEL), wts, conv_a, ssd_fn, ffn_fn)
    return x2.reshape(b, L, D_MODEL), ns_a, ns_b, ns_h, ns_f


def _step_group(x, wts, st_a, st_b, st_h, st_f):
    nb, L, _ = x.shape
    tmaj = lambda s: jnp.transpose(s, (1, 0, 2))
    lpad = -(-L // SUBLANES) * SUBLANES

    def conv_a(uglu):
        ua, (ns,) = _conv_slab(uglu.reshape(L, nb, D_A), tmaj(st_a), wts["w_dw_a"], wts["b_dw_a"],
                               [wts["g_ln_a"], wts["b_ln_a"]], _post_ln_silu, width=CONV_A, ct=D_A,
                               nbt=32, n_s=1, out_w=D_A, out_dtype=BF16, name="conv_a_step")
        return ua.reshape(L * nb, D_A), tmaj(ns)

    def bmaj_pad(v):
        return jnp.pad(tmaj(v), ((0, 0), (0, lpad - L), (0, 0)))

    def ssd_fn(xbc, dt, zs):
        act, (ns,) = _conv_slab(xbc.reshape(L, nb, D_XBC), tmaj(st_b), wts["w_dw_b"], wts["b_dw_b"], [],
                                _post_silu, width=CONV_B, ct=512, nbt=nb, n_s=1, out_w=D_XBC,
                                out_dtype=F32, name="conv_b_step")
        yb, ht = _ssd(bmaj_pad(act), bmaj_pad(dt.reshape(L, nb, LANES)), bmaj_pad(zs.reshape(L, nb, D_B)),
                      st_h, wts["a_log"], wts["d_skip"], wts["g_norm_b"], wts["expand"], wts["expand_t"], q=lpad, bb=4)
        return tmaj(yb[:, :L]).reshape(L * nb, D_B), tmaj(ns), ht

    def ffn_fn(x1):
        past = CONV_F - 1
        x2, ns = _ffn(x1.reshape(1, L * nb, D_MODEL), tmaj(st_f).reshape(1, past * nb, 2 * D_FF), wts,
                      tm=L * nb, sh=nb)
        return x2.reshape(L * nb, D_MODEL), tmaj(ns.reshape(past, nb, 2 * D_FF))

    x2, ns_a, ns_b, ns_h, ns_f = _layer(tmaj(x).reshape(L * nb, D_MODEL), wts, conv_a, ssd_fn, ffn_fn)
    return tmaj(x2.reshape(L, nb, D_MODEL)), ns_a, ns_b, ns_h, ns_f


def _layer_weights(l, g_pre1, g_post1, w_in, b_gate, w_dw_a, b_dw_a, g_ln_a, b_ln_a, w_a_out, b_a_out,
                   w_dw_b, b_dw_b, dt_bias, a_log, d_skip, g_norm_b, w_b_out, w_o,
                   g_pre2, g_post2, w_up, w_dw_f, b_dw_f, w_down):
    row = lambda v: v[l].reshape(1, -1)
    lane_pad = lambda v: jnp.pad(v, ((0, 0), (0, LANES - v.shape[1])))
    w = w_in[l]
    head = jnp.arange(LANES, dtype=jnp.int32)[:, None]
    chan_head = (jnp.arange(D_B, dtype=jnp.int32) // HEAD_DIM)[None, :]
    expand = (head == chan_head).astype(BF16)
    return {
        "g_pre1": row(g_pre1), "g_post1": row(g_post1), "g_pre2": row(g_pre2), "g_post2": row(g_post2),
        "w_main": w[:, :COL_DT].astype(BF16),
        "w_dt": lane_pad(w[:, COL_DT:COL_GATE]).astype(BF16), "w_g": w[:, COL_GATE:].astype(BF16),
        "dt_bias": lane_pad(row(dt_bias)), "b_gate": row(b_gate),
        "w_dw_a": w_dw_a[l], "b_dw_a": row(b_dw_a), "g_ln_a": row(g_ln_a), "b_ln_a": row(b_ln_a),
        "w_a_out": w_a_out[l].astype(BF16), "b_a_out": row(b_a_out),
        "w_dw_b": w_dw_b[l], "b_dw_b": row(b_dw_b),
        "a_log": lane_pad(row(a_log)), "d_skip": jnp.repeat(d_skip[l], HEAD_DIM).reshape(1, D_B),
        "g_norm_b": row(g_norm_b), "expand": expand, "expand_t": expand.T,
        "w_b_out": w_b_out[l].astype(BF16), "w_o": w_o[l].astype(BF16),
        "w_up": w_up[l].astype(BF16), "w_dw_f": w_dw_f[l], "b_dw_f": row(b_dw_f),
        "w_down": w_down[l].astype(BF16),
    }


def kernel(x_prompt, x_sample, state_conv_a, state_conv_b, state_ssm, state_conv_ffn, meta_tokens, g_pre1, g_post1, w_in, b_gate, w_dw_a, b_dw_a, g_ln_a, b_ln_a, w_a_out, b_a_out, w_dw_b, b_dw_b, dt_bias, a_log, d_skip, g_norm_b, w_b_out, w_o, g_pre2, g_post2, w_up, w_dw_f, b_dw_f, w_down):
    depth = w_in.shape[0]
    bp = x_prompt.shape[0]
    nb = x_sample.shape[0]
    xm = meta_tokens.astype(x_prompt.dtype)[None]
    xp, xs = x_prompt, x_sample
    pa, pb, ph, pf = [], [], [], []
    sa, sb, sh, sf = [], [], [], []
    for l in range(depth):
        wts = _layer_weights(l, g_pre1, g_post1, w_in, b_gate, w_dw_a, b_dw_a, g_ln_a, b_ln_a, w_a_out, b_a_out,
                             w_dw_b, b_dw_b, dt_bias, a_log, d_skip, g_norm_b, w_b_out, w_o,
                             g_pre2, g_post2, w_up, w_dw_f, b_dw_f, w_down)
        xm, m_a, m_b, m_h, m_f = _seq_group(
            xm, wts, jnp.zeros((1, CONV_A - 1, D_A), F32), jnp.zeros((1, CONV_B - 1, D_XBC), F32),
            jnp.zeros((1, D_B, D_STATE), F32), jnp.zeros((1, CONV_F - 1, 2 * D_FF), F32), tl=N_META, q=SUBLANES)
        rep = lambda s: jnp.broadcast_to(s, (bp,) + s.shape[1:])
        xp, c_a, c_b, c_h, c_f = _seq_group(xp, wts, rep(m_a), rep(m_b), rep(m_h), rep(m_f), tl=512, q=SSD_KEYS)
        pa.append(c_a); pb.append(c_b); ph.append(c_h.reshape(bp, N_HEADS, HEAD_DIM, D_STATE)); pf.append(c_f)
        xs, d_a, d_b, d_h, d_f = _step_group(
            xs, wts, state_conv_a[l], state_conv_b[l], state_ssm[l].reshape(nb, D_B, D_STATE), state_conv_ffn[l])
        sa.append(d_a); sb.append(d_b); sh.append(d_h.reshape(nb, N_HEADS, HEAD_DIM, D_STATE)); sf.append(d_f)
    return (xp, xs, jnp.stack(pa), jnp.stack(pb), jnp.stack(ph), jnp.stack(pf),
            jnp.stack(sa), jnp.stack(sb), jnp.stack(sh), jnp.stack(sf))
```

```python
import functools

import jax
import jax.numpy as jnp
from jax import lax
from jax.experimental import pallas as pl
from jax.experimental.pallas import tpu as pltpu

D_MODEL = 2048
N_META = 16
D_A = 1024
CONV_A = 31
D_B = 2048
HEAD_DIM = 64
N_HEADS = D_B // HEAD_DIM
N_GROUPS = 8
HEADS_PER_GROUP = N_HEADS // N_GROUPS
GROUP_W = HEADS_PER_GROUP * HEAD_DIM
D_STATE = 128
CONV_B = 4
D_FF = 5632
CONV_F = 3
EPS = 1e-6
D_XBC = D_B + 2 * N_GROUPS * D_STATE
COL_Z = 2 * D_A
COL_XBC = COL_Z + D_B
COL_DT = COL_XBC + D_XBC
COL_GATE = COL_DT + N_HEADS

LANES = 128
SUBLANES = 8
SSD_KEYS = 128
FFN_TILE = 256
FFN_ROWS = 128
VMEM_LIMIT = 56 * 1024 * 1024

F32 = jnp.float32
BF16 = jnp.bfloat16


def _params(*sem):
    return pltpu.CompilerParams(dimension_semantics=sem, vmem_limit_bytes=VMEM_LIMIT)


def _sigmoid(x):
    return 1.0 / (1.0 + jnp.exp(-x))


def _silu(x):
    return x * _sigmoid(x)


def _softplus(x):
    return jnp.maximum(x, 0.0) + jnp.log1p(jnp.exp(-jnp.abs(x)))


def _gelu_tanh(x):
    return 0.5 * x * (1.0 + jnp.tanh(0.7978845608028654 * (x + 0.044715 * (x * x * x))))


def _rms(x, g):
    r = lax.rsqrt(jnp.mean(x * x, axis=-1, keepdims=True) + EPS)
    return x * r * g


def _dot(a, b):
    return jnp.dot(a, b, preferred_element_type=F32)


def _split_bf16(v, parts):
    out = []
    for _ in range(parts):
        p = v.astype(BF16)
        out.append(p)
        v = v - p.astype(F32)
    return out


def _norm_kernel(x_ref, g_ref, o_ref):
    o_ref[...] = _rms(x_ref[...], g_ref[...]).astype(o_ref.dtype)


def _norm(x, g):
    rows, d = x.shape
    tm = min(rows, 512)
    return pl.pallas_call(
        _norm_kernel,
        grid=(rows // tm,),
        in_specs=[pl.BlockSpec((tm, d), lambda i: (i, 0)), pl.BlockSpec((1, d), lambda i: (0, 0))],
        out_specs=pl.BlockSpec((tm, d), lambda i: (i, 0)),
        out_shape=jax.ShapeDtypeStruct((rows, d), BF16),
        compiler_params=_params("parallel"),
        name="norm",
    )(x, g)


def _proj_kernel(*refs, n_w, n_b, epilogue):
    h_ref = refs[0]
    w_refs = refs[1:1 + n_w]
    b_refs = refs[1 + n_w:1 + n_w + n_b]
    o_ref = refs[-1]
    h = h_ref[...]
    accs = [_dot(h, w[...]) for w in w_refs]
    o_ref[...] = epilogue(accs, [b[...] for b in b_refs]).astype(o_ref.dtype)


def _proj(h, ws, bs, epilogue, tn, out_dtype, name, n=None, cols=None):
    rows, d = h.shape
    n = n or ws[0].shape[1]
    cols = cols or [0] * len(ws)
    tm = min(rows, 1024)
    kern = functools.partial(_proj_kernel, n_w=len(ws), n_b=len(bs), epilogue=epilogue)
    return pl.pallas_call(
        kern,
        grid=(rows // tm, n // tn),
        in_specs=[pl.BlockSpec((tm, d), lambda i, j: (i, 0))]
                 + [pl.BlockSpec((d, tn), lambda i, j, c=c // tn: (0, j + c)) for c in cols]
                 + [pl.BlockSpec((1, tn), lambda i, j: (0, j)) for _ in bs],
        out_specs=pl.BlockSpec((tm, tn), lambda i, j: (i, j)),
        out_shape=jax.ShapeDtypeStruct((rows, n), out_dtype),
        compiler_params=_params("parallel", "parallel"),
        name=name,
    )(h, *ws, *bs)


def _epi_glu(a, b):
    return a[0] * _sigmoid(a[1])


def _epi_dt(a, b):
    lane = lax.broadcasted_iota(jnp.int32, a[0].shape, 1)
    return jnp.where(lane < N_HEADS, _softplus(a[0] + b[0]), 0.0)


def _epi_silu(a, b):
    return _silu(a[0])


def _epi_id(a, b):
    return a[0]


def _epi_gate(a, b):
    return _sigmoid(a[0] + b[0])


def _merge_kernel(ua_ref, yb_ref, gt_ref, x_ref, wa_ref, ba_ref, wb_ref, wo_ref, gp_ref, o_ref):
    ya = _dot(ua_ref[...], wa_ref[...]) + ba_ref[...]
    yb = _dot(yb_ref[...], wb_ref[...])
    mix = gt_ref[:, :D_MODEL] * ya + gt_ref[:, D_MODEL:] * yb
    m = _dot(mix.astype(BF16), wo_ref[...])
    o_ref[...] = x_ref[...] + _rms(m, gp_ref[...])


def _merge(ua, yb, gates, x, wa, ba, wb, wo, gp):
    rows = x.shape[0]
    tm = min(rows, 256)
    row_spec = lambda w: pl.BlockSpec((tm, w), lambda i: (i, 0))
    res_spec = lambda a: pl.BlockSpec(a.shape, lambda i: (0, 0), pipeline_mode=pl.Buffered(1))
    return pl.pallas_call(
        _merge_kernel,
        grid=(rows // tm,),
        in_specs=[row_spec(D_A), row_spec(D_B), row_spec(2 * D_MODEL), row_spec(D_MODEL),
                  res_spec(wa), res_spec(ba), res_spec(wb), res_spec(wo), res_spec(gp)],
        out_specs=row_spec(D_MODEL),
        out_shape=jax.ShapeDtypeStruct((rows, D_MODEL), F32),
        compiler_params=_params("parallel"),
        name="merge",
    )(ua, yb, gates, x, wa, ba, wb, wo, gp)


def _ffn_up_kernel(x1_ref, gpre_ref, wg_ref, wv_ref, stg_ref, stv_ref,
                   cwg_ref, cwv_ref, cbg_ref, cbv_ref, pwg_ref, pwv_ref, pbg_ref, pbv_ref,
                   even_ref, odd_ref, nsg_ref, nsv_ref, h_scr, ag, av, bg, bv, *carry, tm, sh, hist, rc, cw):
    l = pl.program_id(1)
    t = pl.program_id(2)
    n_t = pl.num_programs(2) - 1
    past = (CONV_F - 1) * sh
    mc = min(tm, FFN_ROWS)
    n_m = tm // mc

    @pl.when(t == 0)
    def _():
        h_scr[...] = _rms(x1_ref[0], gpre_ref[...]).astype(BF16)

    if carry:
        @pl.when(jnp.logical_and(t == 0, l == 0))
        def _():
            carry[0][...] = jnp.zeros_like(carry[0])

    def matmuls(half, wins, m):
        cs = slice(half * cw, (half + 1) * cw)
        h = h_scr[m * mc:(m + 1) * mc, :]
        for s, (w_ref, st_ref, ns_ref, win) in enumerate(((wg_ref, stg_ref, nsg_ref, wins[0]),
                                                           (wv_ref, stv_ref, nsv_ref, wins[1]))):
            if m == 0:
                rows = st_ref[0, :, cs]
                if carry:
                    rows = jnp.where(l == 0, rows, carry[0][t, s, hist - past:hist, cs])
                win[hist - past:hist, :] = rows
            win[hist + m * mc:hist + (m + 1) * mc, :] = _dot(h, w_ref[:, cs])
            if m == n_m - 1:
                ns_ref[0, 0, :, cs] = win[hist + tm - past:hist + tm, :]
                if carry:
                    carry[0][t, s, :, cs] = win[tm:tm + hist, :]

    def conv(win, cw_ref, cb_ref, r0, cs):
        out = cb_ref[:, cs]
        for k in range(CONV_F):
            lo = hist - (CONV_F - 1 - k) * sh + r0
            out = out + cw_ref[k:k + 1, cs] * win[lo:lo + rc, :]
        return out

    def gate(half, wins, refs, out_ref, m):
        cs = slice(half * cw, (half + 1) * cw)
        for r0 in range(m * mc, (m + 1) * mc, rc):
            g = _gelu_tanh(conv(wins[0], refs[0], refs[2], r0, cs))
            out_ref[0, r0:r0 + rc, :] = (g * conv(wins[1], refs[1], refs[3], r0, cs)).astype(BF16)

    cur = (cwg_ref, cwv_ref, cbg_ref, cbv_ref)
    prv = (pwg_ref, pwv_ref, pbg_ref, pbv_ref)
    win_a, win_b = (ag, av), (bg, bv)

    @pl.when(t == 0)
    def _():
        for m in range(n_m):
            matmuls(0, win_a, m)
        for m in range(n_m):
            matmuls(1, win_b, m)
            gate(0, win_a, cur, even_ref, m)

    @pl.when(jnp.logical_and(t > 0, t < n_t))
    def _():
        for m in range(n_m):
            matmuls(0, win_a, m)
            gate(1, win_b, prv, odd_ref, m)
        for m in range(n_m):
            matmuls(1, win_b, m)
            gate(0, win_a, cur, even_ref, m)

    @pl.when(t == n_t)
    def _():
        for m in range(n_m):
            gate(1, win_b, prv, odd_ref, m)


def _ffn_up(x1, state, wts, *, tm, sh):
    b, L, _ = x1.shape
    cw = FFN_TILE
    tf = 2 * cw
    nj = D_FF // tf
    past = (CONV_F - 1) * sh
    hist = -(-past // SUBLANES) * SUBLANES
    n_l = L // tm
    carry = [pltpu.VMEM((nj, 2, hist, tf), F32)] if n_l > 1 else []
    cur = lambda t: jnp.minimum(t, nj - 1)
    prv = lambda t: jnp.maximum(t - 1, 0)
    col = lambda rows, off, pick: pl.BlockSpec((rows, tf), lambda i, l, t: (0, pick(t) + off))
    st = lambda off: pl.BlockSpec((1, past, tf), lambda i, l, t: (i, 0, cur(t) + off))
    tail = pl.BlockSpec((1, 1, past, tf), lambda i, l, t: (i, l, 0, cur(t)))
    even, odd, ns_g, ns_v = pl.pallas_call(
        functools.partial(_ffn_up_kernel, tm=tm, sh=sh, hist=hist, rc=min(tm, 32), cw=cw),
        grid=(b, n_l, nj + 1),
        in_specs=[pl.BlockSpec((1, tm, D_MODEL), lambda i, l, t: (i, l, 0)),
                  pl.BlockSpec((1, D_MODEL), lambda i, l, t: (0, 0)),
                  col(D_MODEL, 0, cur), col(D_MODEL, nj, cur), st(0), st(nj),
                  col(CONV_F, 0, cur), col(CONV_F, nj, cur), col(1, 0, cur), col(1, nj, cur),
                  col(CONV_F, 0, prv), col(CONV_F, nj, prv), col(1, 0, prv), col(1, nj, prv)],
        out_specs=[pl.BlockSpec((1, tm, cw), lambda i, l, t: (i, l, cur(t))),
                   pl.BlockSpec((1, tm, cw), lambda i, l, t: (i, l, prv(t))), tail, tail],
        out_shape=[jax.ShapeDtypeStruct((b, L, D_FF // 2), BF16),
                   jax.ShapeDtypeStruct((b, L, D_FF // 2), BF16),
                   jax.ShapeDtypeStruct((b, n_l, past, D_FF), F32),
                   jax.ShapeDtypeStruct((b, n_l, past, D_FF), F32)],
        scratch_shapes=[pltpu.VMEM((tm, D_MODEL), BF16)] + [pltpu.VMEM((hist + tm, cw), F32)] * 4 + carry,
        compiler_params=_params("parallel", "arbitrary", "arbitrary"),
        name="ffn_up",
    )(x1, wts["g_pre2"], wts["w_up"], wts["w_up"], state, state,
      wts["w_dw_f"], wts["w_dw_f"], wts["b_dw_f"], wts["b_dw_f"],
      wts["w_dw_f"], wts["w_dw_f"], wts["b_dw_f"], wts["b_dw_f"])
    return even, odd, jnp.concatenate([ns_g[:, -1], ns_v[:, -1]], axis=-1)


def _ffn_down_kernel(even_ref, odd_ref, we_ref, wo_ref, x1_ref, g_ref, o_ref):
    f = _dot(even_ref[...], we_ref[...]) + _dot(odd_ref[...], wo_ref[...])
    o_ref[...] = x1_ref[...] + _rms(f, g_ref[...])


def _ffn_down(even, odd, w_even, w_odd, x1, g):
    rows = x1.shape[0]
    tm = min(rows, 256)
    res = lambda a: pl.BlockSpec(a.shape, lambda i: (0, 0), pipeline_mode=pl.Buffered(1))
    row = lambda w: pl.BlockSpec((tm, w), lambda i: (i, 0))
    return pl.pallas_call(
        _ffn_down_kernel,
        grid=(rows // tm,),
        in_specs=[row(D_FF // 2), row(D_FF // 2), res(w_even), res(w_odd), row(D_MODEL), res(g)],
        out_specs=row(D_MODEL),
        out_shape=jax.ShapeDtypeStruct((rows, D_MODEL), F32),
        compiler_params=_params("parallel"),
        name="ffn_down",
    )(even, odd, w_even, w_odd, x1, g)


def _post_ln_silu(outs, extras):
    u = outs[0]
    mu = jnp.mean(u, axis=-1, keepdims=True)
    xc = u - mu
    r = lax.rsqrt(jnp.mean(xc * xc, axis=-1, keepdims=True) + EPS)
    return _silu(xc * r * extras[0] + extras[1])


def _post_silu(outs, extras):
    return _silu(outs[0])


def _conv_seq_kernel(*refs, width, tl, n_s, n_x, post, hist, rc):
    u_refs = refs[0:n_s]
    st_refs = refs[n_s:2 * n_s]
    w_refs = refs[2 * n_s:3 * n_s]
    b_refs = refs[3 * n_s:4 * n_s]
    x_refs = refs[4 * n_s:4 * n_s + n_x]
    o_ref = refs[4 * n_s + n_x]
    ns_refs = refs[4 * n_s + n_x + 1:4 * n_s + n_x + 1 + n_s]
    win_refs = refs[4 * n_s + n_x + 1 + n_s:]
    past = width - 1
    l = pl.program_id(2)

    for s in range(n_s):
        win = win_refs[s]

        @pl.when(l == 0)
        def _(win=win, s=s):
            if hist > past:
                win[0:hist - past, :] = jnp.zeros((hist - past, win.shape[1]), F32)
            win[hist - past:hist, :] = st_refs[s][0]

        @pl.when(l > 0)
        def _(win=win):
            win[0:hist, :] = win[tl:tl + hist, :]

        win[hist:hist + tl, :] = u_refs[s][0]

    extras = [x[...] for x in x_refs]
    for c in range(tl // rc):
        outs = []
        for s in range(n_s):
            acc = b_refs[s][...] + w_refs[s][0:1, :] * win_refs[s][c * rc + hist - past:c * rc + hist - past + rc, :]
            for k in range(1, width):
                lo = c * rc + hist - past + k
                acc = acc + w_refs[s][k:k + 1, :] * win_refs[s][lo:lo + rc, :]
            outs.append(acc)
        o_ref[0, c * rc:(c + 1) * rc, :] = post(outs, extras).astype(o_ref.dtype)

    @pl.when(l == pl.num_programs(2) - 1)
    def _():
        for s in range(n_s):
            ns_refs[s][0] = win_refs[s][hist + tl - past:hist + tl, :]


def _conv_seq(u, state, w, bias, extras, post, *, width, ct, tl, n_s, out_w, out_dtype, name):
    b, L, _ = u.shape
    past = width - 1
    hist = -(-past // SUBLANES) * SUBLANES
    nj = out_w // ct
    rc = min(tl, 32)
    kern = functools.partial(_conv_seq_kernel, width=width, tl=tl, n_s=n_s, n_x=len(extras),
                             post=post, hist=hist, rc=rc)
    in_specs = ([pl.BlockSpec((1, tl, ct), lambda i, j, l, s=s: (i, l, j + s * nj)) for s in range(n_s)]
                + [pl.BlockSpec((1, past, ct), lambda i, j, l, s=s: (i, 0, j + s * nj)) for s in range(n_s)]
                + [pl.BlockSpec((width, ct), lambda i, j, l, s=s: (0, j + s * nj)) for s in range(n_s)]
                + [pl.BlockSpec((1, ct), lambda i, j, l, s=s: (0, j + s * nj)) for s in range(n_s)]
                + [pl.BlockSpec((1, ct), lambda i, j, l: (0, j)) for _ in extras])
    out_specs = ([pl.BlockSpec((1, tl, ct), lambda i, j, l: (i, l, j))]
                 + [pl.BlockSpec((1, past, ct), lambda i, j, l: (i, 0, j)) for _ in range(n_s)])
    out_shape = ([jax.ShapeDtypeStruct((b, L, out_w), out_dtype)]
                 + [jax.ShapeDtypeStruct((b, past, out_w), F32) for _ in range(n_s)])
    res = pl.pallas_call(
        kern,
        grid=(b, nj, L // tl),
        in_specs=in_specs,
        out_specs=out_specs,
        out_shape=out_shape,
        scratch_shapes=[pltpu.VMEM((hist + tl, ct), F32) for _ in range(n_s)],
        compiler_params=_params("parallel", "parallel", "arbitrary"),
        name=name,
    )(*([u] * n_s), *([state] * n_s), *([w] * n_s), *([bias] * n_s), *extras)
    return res[0], res[1:]


def _conv_slab_kernel(*refs, width, steps, n_s, n_x, post):
    u_refs = refs[0:n_s]
    st_refs = refs[n_s:2 * n_s]
    w_refs = refs[2 * n_s:3 * n_s]
    b_refs = refs[3 * n_s:4 * n_s]
    x_refs = refs[4 * n_s:4 * n_s + n_x]
    o_ref = refs[4 * n_s + n_x]
    ns_refs = refs[4 * n_s + n_x + 1:]
    past = width - 1

    def slab(s, i):
        return st_refs[s][i] if i < past else u_refs[s][i - past]

    extras = [x[...] for x in x_refs]
    for t in range(steps):
        outs = []
        for s in range(n_s):
            acc = b_refs[s][...] + w_refs[s][0:1, :] * slab(s, t)
            for k in range(1, width):
                acc = acc + w_refs[s][k:k + 1, :] * slab(s, t + k)
            outs.append(acc)
        o_ref[t] = post(outs, extras).astype(o_ref.dtype)
    for s in range(n_s):
        for i in range(past):
            ns_refs[s][i] = slab(s, i + steps)


def _conv_slab(u, state, w, bias, extras, post, *, width, ct, nbt, n_s, out_w, out_dtype, name):
    steps, nb, _ = u.shape
    past = width - 1
    nj = out_w // ct
    kern = functools.partial(_conv_slab_kernel, width=width, steps=steps, n_s=n_s, n_x=len(extras), post=post)
    in_specs = ([pl.BlockSpec((steps, nbt, ct), lambda i, j, s=s: (0, i, j + s * nj)) for s in range(n_s)]
                + [pl.BlockSpec((past, nbt, ct), lambda i, j, s=s: (0, i, j + s * nj)) for s in range(n_s)]
                + [pl.BlockSpec((width, ct), lambda i, j, s=s: (0, j + s * nj)) for s in range(n_s)]
                + [pl.BlockSpec((1, ct), lambda i, j, s=s: (0, j + s * nj)) for s in range(n_s)]
                + [pl.BlockSpec((1, ct), lambda i, j: (0, j)) for _ in extras])
    out_specs = ([pl.BlockSpec((steps, nbt, ct), lambda i, j: (0, i, j))]
                 + [pl.BlockSpec((past, nbt, ct), lambda i, j: (0, i, j)) for _ in range(n_s)])
    out_shape = ([jax.ShapeDtypeStruct((steps, nb, out_w), out_dtype)]
                 + [jax.ShapeDtypeStruct((past, nb, out_w), F32) for _ in range(n_s)])
    res = pl.pallas_call(
        kern,
        grid=(nb // nbt, nj),
        in_specs=in_specs,
        out_specs=out_specs,
        out_shape=out_shape,
        compiler_params=_params("parallel", "parallel"),
        name=name,
    )(*([u] * n_s), *([state] * n_s), *([w] * n_s), *([bias] * n_s), *extras)
    return res[0], res[1:]


def _ssd_kernel(*refs, q, bb, conv):
    xbc_ref, dt_ref, zs_ref, h0_ref, alog_ref, dsk_ref, gn_ref, e_ref, et_ref = refs[:9]
    if conv:
        stb_ref, cw_ref, cb_ref, y_ref, ht_ref, nsb_ref, h_scr, y_scr, win = refs[9:]
    else:
        y_ref, ht_ref, h_scr, y_scr = refs[9:]
    qk = SSD_KEYS
    c = pl.program_id(1)

    @pl.when(c == 0)
    def _():
        h_scr[...] = h0_ref[...]

    a = -jnp.exp(alog_ref[...])
    e = e_ref[...]
    row = lax.broadcasted_iota(jnp.int32, (qk, qk), 0)
    col = lax.broadcasted_iota(jnp.int32, (qk, qk), 1)
    tril = jnp.where(row >= col, 1.0, 0.0).astype(BF16)
    causal = row[:q, :] >= col[:q, :]
    key_head = lax.broadcasted_iota(jnp.int32, (qk, GROUP_W), 1) // HEAD_DIM

    def expand(v):
        return sum(_dot(p, e) for p in _split_bf16(v, 2))

    def pad_keys(v):
        if q == qk:
            return v
        return jnp.concatenate([v, jnp.zeros((qk - q, v.shape[1]), v.dtype)], axis=0)

    for s in range(bb):
        if conv:
            past = CONV_B - 1
            hist = SUBLANES

            @pl.when(c == 0)
            def _(s=s):
                win[s, hist - past:hist, :] = stb_ref[s]

            @pl.when(c > 0)
            def _(s=s):
                win[s, 0:hist, :] = win[s, q:q + hist, :]

            win[s, hist:hist + q, :] = xbc_ref[s]
            pre = cb_ref[...] + cw_ref[0:1, :] * win[s, hist - past:hist - past + q, :]
            for k in range(1, CONV_B):
                pre = pre + cw_ref[k:k + 1, :] * win[s, hist - past + k:hist - past + k + q, :]
            xq = _silu(pre)
            nsb_ref[s] = win[s, hist + q - past:hist + q, :]
        else:
            xq = xbc_ref[s]

        x = xq[:, :D_B]
        bm = pad_keys(xq[:, D_B:D_B + N_GROUPS * D_STATE]).astype(BF16)
        cm = xq[:, D_B + N_GROUPS * D_STATE:].astype(BF16)
        dt = dt_ref[s]
        d_a = pad_keys(dt * a)
        cum = sum(_dot(tril, p) for p in _split_bf16(d_a, 3))
        cum_t = cum.T
        cum_last = cum[qk - 1:qk, :]
        cum_q = cum[:q, :]

        xdt = x * expand(dt)
        xw = pad_keys(xdt * expand(jnp.exp(cum_last - cum_q))).astype(BF16)
        xdt = pad_keys(xdt).astype(BF16)
        ecum = expand(jnp.exp(cum_q))
        decay_t = jnp.broadcast_to(jnp.exp(cum_t[:, qk - 1:qk]), (LANES, D_STATE))
        decay_rows = sum(_dot(et_ref[...], p) for p in _split_bf16(decay_t, 2))

        for g in range(N_GROUPS):
            cols = slice(g * GROUP_W, (g + 1) * GROUP_W)
            bg = bm[:, g * D_STATE:(g + 1) * D_STATE]
            cg = cm[:, g * D_STATE:(g + 1) * D_STATE]
            cb = lax.dot_general(cg, bg, (((1,), (1,)), ((), ())), preferred_element_type=F32)
            xdt_g = xdt[:, cols]
            scores, keys = [], []
            for r in range(HEADS_PER_GROUP):
                hd = g * HEADS_PER_GROUP + r
                seg = jnp.where(causal, cum_q[:, hd:hd + 1] - cum_t[hd:hd + 1, :], -1e30)
                scores.append((cb * jnp.exp(seg)).astype(BF16))
                keys.append(jnp.where(key_head == r, xdt_g, jnp.zeros_like(xdt_g)))
            y_g = _dot(jnp.concatenate(scores, axis=1), jnp.concatenate(keys, axis=0))
            h_g = h_scr[s, cols, :]
            y_inter = lax.dot_general(cg, h_g.astype(BF16), (((1,), (1,)), ((), ())),
                                      preferred_element_type=F32)
            y_scr[s, :, cols] = y_g + y_inter * ecum[:, cols]
            s_g = lax.dot_general(xw[:, cols], bg, (((0,), (0,)), ((), ())), preferred_element_type=F32)
            h_scr[s, cols, :] = decay_rows[cols, :] * h_g + s_g

        y = y_scr[s] + dsk_ref[...] * x
        y_ref[s] = _rms(y * zs_ref[s], gn_ref[...]).astype(y_ref.dtype)

    @pl.when(c == pl.num_programs(1) - 1)
    def _():
        ht_ref[...] = h_scr[...]


def _ssd(xbc, dt, zs, h0, alog, dsk, gn, e, et, conv_b=None, *, q, bb):
    b, L, _ = xbc.shape
    seq = lambda w: pl.BlockSpec((bb, q, w), lambda i, c: (i, c, 0))
    full = lambda arr: pl.BlockSpec(arr.shape, lambda i, c: (0,) * arr.ndim)
    state = pl.BlockSpec((bb, D_B, D_STATE), lambda i, c: (i, 0, 0))
    cstate = pl.BlockSpec((bb, CONV_B - 1, D_XBC), lambda i, c: (i, 0, 0))
    conv = conv_b is not None
    return pl.pallas_call(
        functools.partial(_ssd_kernel, q=q, bb=bb, conv=conv),
        grid=(b // bb, L // q),
        in_specs=[seq(D_XBC), seq(LANES), seq(D_B), state,
                  full(alog), full(dsk), full(gn), full(e), full(et)]
                 + ([cstate, full(conv_b[1]), full(conv_b[2])] if conv else []),
        out_specs=[seq(D_B), state] + ([cstate] if conv else []),
        out_shape=[jax.ShapeDtypeStruct((b, L, D_B), BF16),
                   jax.ShapeDtypeStruct((b, D_B, D_STATE), F32)]
                  + ([jax.ShapeDtypeStruct((b, CONV_B - 1, D_XBC), F32)] if conv else []),
        scratch_shapes=[pltpu.VMEM((bb, D_B, D_STATE), F32), pltpu.VMEM((bb, q, D_B), F32)]
                       + ([pltpu.VMEM((bb, SUBLANES + q, D_XBC), F32)] if conv else []),
        compiler_params=_params("parallel", "arbitrary"),
        name="ssd",
    )(xbc, dt, zs, h0, alog, dsk, gn, e, et, *(conv_b or ()))


def _layer(x, wts, conv_a, ssd_fn, ffn_fn):
    h = _norm(x, wts["g_pre1"])
    w_main = wts["w_main"]
    uglu = _proj(h, [w_main, w_main], [], _epi_glu, 512, F32, "in_glu", n=D_A, cols=[0, D_A])
    dt = _proj(h, [wts["w_dt"]], [wts["dt_bias"]], _epi_dt, LANES, F32, "in_dt")
    zs = _proj(h, [w_main], [], _epi_silu, 512, F32, "in_z", n=D_B, cols=[COL_Z])
    xbc = _proj(h, [w_main], [], _epi_id, 512, F32, "in_xbc", n=D_XBC, cols=[COL_XBC])
    gates = _proj(h, [wts["w_g"]], [wts["b_gate"]], _epi_gate, 512, F32, "in_gates")

    ua, st_a = conv_a(uglu)
    yb, st_b, st_h = ssd_fn(xbc, dt, zs)

    x1 = _merge(ua, yb, gates, x, wts["w_a_out"], wts["b_a_out"], wts["w_b_out"], wts["w_o"], wts["g_post1"])
    x2, st_f = ffn_fn(x1)
    return x2, st_a, st_b, st_h, st_f


def _seq_group(x, wts, st_a, st_b, st_h, st_f, tl, q):
    b, L, _ = x.shape

    def conv_a(uglu):
        ua, (ns,) = _conv_seq(uglu.reshape(b, L, D_A), st_a, wts["w_dw_a"], wts["b_dw_a"],
                              [wts["g_ln_a"], wts["b_ln_a"]], _post_ln_silu, width=CONV_A, ct=D_A,
                              tl=min(tl, 256), n_s=1, out_w=D_A, out_dtype=BF16, name="conv_a")
        return ua.reshape(b * L, D_A), ns

    def ssd_fn(xbc, dt, zs):
        yb, ht, ns = _ssd(xbc.reshape(b, L, D_XBC), dt.reshape(b, L, LANES), zs.reshape(b, L, D_B), st_h,
                          wts["a_log"], wts["d_skip"], wts["g_norm_b"], wts["expand"], wts["expand_t"],
                          (st_b, wts["w_dw_b"], wts["b_dw_b"]), q=q, bb=1)
        return yb.reshape(b * L, D_B), ns, ht

    def ffn_fn(x1):
        even, odd, ns = _ffn_up(x1.reshape(b, L, D_MODEL), st_f, wts, tm=min(L, 1024), sh=1)
        x2 = _ffn_down(even.reshape(b * L, D_FF // 2), odd.reshape(b * L, D_FF // 2),
                       wts["w_down_even"], wts["w_down_odd"], x1, wts["g_post2"])
        return x2, ns

    x2, ns_a, ns_b, ns_h, ns_f = _layer(x.reshape(b * L, D_MODEL), wts, conv_a, ssd_fn, ffn_fn)
    return x2.reshape(b, L, D_MODEL), ns_a, ns_b, ns_h, ns_f


def _step_group(x, wts, st_a, st_b, st_h, st_f):
    nb, L, _ = x.shape
    tmaj = lambda s: jnp.transpose(s, (1, 0, 2))
    lpad = -(-L // SUBLANES) * SUBLANES

    def conv_a(uglu):
        ua, (ns,) = _conv_slab(uglu.reshape(L, nb, D_A), tmaj(st_a), wts["w_dw_a"], wts["b_dw_a"],
                               [wts["g_ln_a"], wts["b_ln_a"]], _post_ln_silu, width=CONV_A, ct=D_A,
                               nbt=32, n_s=1, out_w=D_A, out_dtype=BF16, name="conv_a_step")
        return ua.reshape(L * nb, D_A), tmaj(ns)

    def bmaj_pad(v):
        return jnp.pad(tmaj(v), ((0, 0), (0, lpad - L), (0, 0)))

    def ssd_fn(xbc, dt, zs):
        act, (ns,) = _conv_slab(xbc.reshape(L, nb, D_XBC), tmaj(st_b), wts["w_dw_b"], wts["b_dw_b"], [],
                                _post_silu, width=CONV_B, ct=512, nbt=nb, n_s=1, out_w=D_XBC,
                                out_dtype=F32, name="conv_b_step")
        yb, ht = _ssd(bmaj_pad(act), bmaj_pad(dt.reshape(L, nb, LANES)), bmaj_pad(zs.reshape(L, nb, D_B)),
                      st_h, wts["a_log"], wts["d_skip"], wts["g_norm_b"], wts["expand"], wts["expand_t"], q=lpad, bb=4)
        return tmaj(yb[:, :L]).reshape(L * nb, D_B), tmaj(ns), ht

    def ffn_fn(x1):
        past = CONV_F - 1
        even, odd, ns = _ffn_up(x1.reshape(1, L * nb, D_MODEL), tmaj(st_f).reshape(1, past * nb, 2 * D_FF), wts,
                                tm=L * nb, sh=nb)
        x2 = _ffn_down(even.reshape(L * nb, D_FF // 2), odd.reshape(L * nb, D_FF // 2),
                       wts["w_down_even"], wts["w_down_odd"], x1, wts["g_post2"])
        return x2, tmaj(ns.reshape(past, nb, 2 * D_FF))

    x2, ns_a, ns_b, ns_h, ns_f = _layer(tmaj(x).reshape(L * nb, D_MODEL), wts, conv_a, ssd_fn, ffn_fn)
    return tmaj(x2.reshape(L, nb, D_MODEL)), ns_a, ns_b, ns_h, ns_f


def _layer_weights(l, g_pre1, g_post1, w_in, b_gate, w_dw_a, b_dw_a, g_ln_a, b_ln_a, w_a_out, b_a_out,
                   w_dw_b, b_dw_b, dt_bias, a_log, d_skip, g_norm_b, w_b_out, w_o,
                   g_pre2, g_post2, w_up, w_dw_f, b_dw_f, w_down):
    row = lambda v: v[l].reshape(1, -1)
    lane_pad = lambda v: jnp.pad(v, ((0, 0), (0, LANES - v.shape[1])))
    w = w_in[l]
    head = jnp.arange(LANES, dtype=jnp.int32)[:, None]
    chan_head = (jnp.arange(D_B, dtype=jnp.int32) // HEAD_DIM)[None, :]
    expand = (head == chan_head).astype(BF16)
    wd = w_down[l].astype(BF16).reshape(D_FF // (2 * FFN_TILE), 2, FFN_TILE, D_MODEL)
    return {
        "g_pre1": row(g_pre1), "g_post1": row(g_post1), "g_pre2": row(g_pre2), "g_post2": row(g_post2),
        "w_main": w[:, :COL_DT].astype(BF16),
        "w_dt": lane_pad(w[:, COL_DT:COL_GATE]).astype(BF16), "w_g": w[:, COL_GATE:].astype(BF16),
        "dt_bias": lane_pad(row(dt_bias)), "b_gate": row(b_gate),
        "w_dw_a": w_dw_a[l], "b_dw_a": row(b_dw_a), "g_ln_a": row(g_ln_a), "b_ln_a": row(b_ln_a),
        "w_a_out": w_a_out[l].astype(BF16), "b_a_out": row(b_a_out),
        "w_dw_b": w_dw_b[l], "b_dw_b": row(b_dw_b),
        "a_log": lane_pad(row(a_log)), "d_skip": jnp.repeat(d_skip[l], HEAD_DIM).reshape(1, D_B),
        "g_norm_b": row(g_norm_b), "expand": expand, "expand_t": expand.T,
        "w_b_out": w_b_out[l].astype(BF16), "w_o": w_o[l].astype(BF16),
        "w_up": w_up[l].astype(BF16), "w_dw_f": w_dw_f[l], "b_dw_f": row(b_dw_f),
        "w_down_even": wd[:, 0].reshape(D_FF // 2, D_MODEL), "w_down_odd": wd[:, 1].reshape(D_FF // 2, D_MODEL),
    }


def kernel(x_prompt, x_sample, state_conv_a, state_conv_b, state_ssm, state_conv_ffn, meta_tokens, g_pre1, g_post1, w_in, b_gate, w_dw_a, b_dw_a, g_ln_a, b_ln_a, w_a_out, b_a_out, w_dw_b, b_dw_b, dt_bias, a_log, d_skip, g_norm_b, w_b_out, w_o, g_pre2, g_post2, w_up, w_dw_f, b_dw_f, w_down):
    depth = w_in.shape[0]
    bp = x_prompt.shape[0]
    nb = x_sample.shape[0]
    xm = meta_tokens.astype(x_prompt.dtype)[None]
    xp, xs = x_prompt, x_sample
    pa, pb, ph, pf = [], [], [], []
    sa, sb, sh, sf = [], [], [], []
    for l in range(depth):
        wts = _layer_weights(l, g_pre1, g_post1, w_in, b_gate, w_dw_a, b_dw_a, g_ln_a, b_ln_a, w_a_out, b_a_out,
                             w_dw_b, b_dw_b, dt_bias, a_log, d_skip, g_norm_b, w_b_out, w_o,
                             g_pre2, g_post2, w_up, w_dw_f, b_dw_f, w_down)
        xm, m_a, m_b, m_h, m_f = _seq_group(
            xm, wts, jnp.zeros((1, CONV_A - 1, D_A), F32), jnp.zeros((1, CONV_B - 1, D_XBC), F32),
            jnp.zeros((1, D_B, D_STATE), F32), jnp.zeros((1, CONV_F - 1, 2 * D_FF), F32), tl=N_META, q=SUBLANES)
        rep = lambda s: jnp.broadcast_to(s, (bp,) + s.shape[1:])
        xp, c_a, c_b, c_h, c_f = _seq_group(xp, wts, rep(m_a), rep(m_b), rep(m_h), rep(m_f), tl=512, q=SSD_KEYS)
        pa.append(c_a); pb.append(c_b); ph.append(c_h.reshape(bp, N_HEADS, HEAD_DIM, D_STATE)); pf.append(c_f)
        xs, d_a, d_b, d_h, d_f = _step_group(
            xs, wts, state_conv_a[l], state_conv_b[l], state_ssm[l].reshape(nb, D_B, D_STATE), state_conv_ffn[l])
        sa.append(d_a); sb.append(d_b); sh.append(d_h.reshape(nb, N_HEADS, HEAD_DIM, D_STATE)); sf.append(d_f)
    return (xp, xs, jnp.stack(pa), jnp.stack(pb), jnp.stack(ph), jnp.stack(pf),
            jnp.stack(sa), jnp.stack(sb), jnp.stack(sh), jnp.stack(sf))
```

```python
import functools

import jax
import jax.numpy as jnp
from jax import lax
from jax.experimental import pallas as pl
from jax.experimental.pallas import tpu as pltpu

D_MODEL = 2048
N_META = 16
D_A = 1024
CONV_A = 31
D_B = 2048
HEAD_DIM = 64
N_HEADS = D_B // HEAD_DIM
N_GROUPS = 8
HEADS_PER_GROUP = N_HEADS // N_GROUPS
GROUP_W = HEADS_PER_GROUP * HEAD_DIM
D_STATE = 128
CONV_B = 4
D_FF = 5632
CONV_F = 3
EPS = 1e-6
D_XBC = D_B + 2 * N_GROUPS * D_STATE
COL_Z = 2 * D_A
COL_XBC = COL_Z + D_B
COL_DT = COL_XBC + D_XBC
COL_GATE = COL_DT + N_HEADS

LANES = 128
SUBLANES = 8
SSD_KEYS = 128
FFN_TILE = 256
FFN_ROWS = 256
VMEM_LIMIT = 56 * 1024 * 1024

F32 = jnp.float32
BF16 = jnp.bfloat16


def _params(*sem):
    return pltpu.CompilerParams(dimension_semantics=sem, vmem_limit_bytes=VMEM_LIMIT)


def _sigmoid(x):
    return 1.0 / (1.0 + jnp.exp(-x))


def _silu(x):
    return x * _sigmoid(x)


def _softplus(x):
    return jnp.maximum(x, 0.0) + jnp.log1p(jnp.exp(-jnp.abs(x)))


def _gelu_tanh(x):
    return 0.5 * x * (1.0 + jnp.tanh(0.7978845608028654 * (x + 0.044715 * (x * x * x))))


def _rms(x, g):
    r = lax.rsqrt(jnp.mean(x * x, axis=-1, keepdims=True) + EPS)
    return x * r * g


def _dot(a, b):
    return jnp.dot(a, b, preferred_element_type=F32)


def _split_bf16(v, parts):
    out = []
    for _ in range(parts):
        p = v.astype(BF16)
        out.append(p)
        v = v - p.astype(F32)
    return out


def _norm_kernel(x_ref, g_ref, o_ref):
    o_ref[...] = _rms(x_ref[...], g_ref[...]).astype(o_ref.dtype)


def _norm(x, g):
    rows, d = x.shape
    tm = min(rows, 512)
    return pl.pallas_call(
        _norm_kernel,
        grid=(rows // tm,),
        in_specs=[pl.BlockSpec((tm, d), lambda i: (i, 0)), pl.BlockSpec((1, d), lambda i: (0, 0))],
        out_specs=pl.BlockSpec((tm, d), lambda i: (i, 0)),
        out_shape=jax.ShapeDtypeStruct((rows, d), BF16),
        compiler_params=_params("parallel"),
        name="norm",
    )(x, g)


def _proj_kernel(*refs, n_w, n_b, epilogue):
    h_ref = refs[0]
    w_refs = refs[1:1 + n_w]
    b_refs = refs[1 + n_w:1 + n_w + n_b]
    o_ref = refs[-1]
    h = h_ref[...]
    accs = [_dot(h, w[...]) for w in w_refs]
    o_ref[...] = epilogue(accs, [b[...] for b in b_refs]).astype(o_ref.dtype)


def _proj(h, ws, bs, epilogue, tn, out_dtype, name, n=None, cols=None):
    rows, d = h.shape
    n = n or ws[0].shape[1]
    cols = cols or [0] * len(ws)
    tm = min(rows, 1024)
    kern = functools.partial(_proj_kernel, n_w=len(ws), n_b=len(bs), epilogue=epilogue)
    return pl.pallas_call(
        kern,
        grid=(rows // tm, n // tn),
        in_specs=[pl.BlockSpec((tm, d), lambda i, j: (i, 0))]
                 + [pl.BlockSpec((d, tn), lambda i, j, c=c // tn: (0, j + c)) for c in cols]
                 + [pl.BlockSpec((1, tn), lambda i, j: (0, j)) for _ in bs],
        out_specs=pl.BlockSpec((tm, tn), lambda i, j: (i, j)),
        out_shape=jax.ShapeDtypeStruct((rows, n), out_dtype),
        compiler_params=_params("parallel", "parallel"),
        name=name,
    )(h, *ws, *bs)


def _epi_glu(a, b):
    return a[0] * _sigmoid(a[1])


def _epi_dt(a, b):
    lane = lax.broadcasted_iota(jnp.int32, a[0].shape, 1)
    return jnp.where(lane < N_HEADS, _softplus(a[0] + b[0]), 0.0)


def _epi_silu(a, b):
    return _silu(a[0])


def _epi_id(a, b):
    return a[0]


def _epi_gate(a, b):
    return _sigmoid(a[0] + b[0])


def _merge_kernel(ua_ref, yb_ref, gt_ref, x_ref, wa_ref, ba_ref, wb_ref, wo_ref, gp_ref, o_ref):
    ya = _dot(ua_ref[...], wa_ref[...]) + ba_ref[...]
    yb = _dot(yb_ref[...], wb_ref[...])
    mix = gt_ref[:, :D_MODEL] * ya + gt_ref[:, D_MODEL:] * yb
    m = _dot(mix.astype(BF16), wo_ref[...])
    o_ref[...] = x_ref[...] + _rms(m, gp_ref[...])


def _merge(ua, yb, gates, x, wa, ba, wb, wo, gp):
    rows = x.shape[0]
    tm = min(rows, 256)
    row_spec = lambda w: pl.BlockSpec((tm, w), lambda i: (i, 0))
    res_spec = lambda a: pl.BlockSpec(a.shape, lambda i: (0, 0), pipeline_mode=pl.Buffered(1))
    return pl.pallas_call(
        _merge_kernel,
        grid=(rows // tm,),
        in_specs=[row_spec(D_A), row_spec(D_B), row_spec(2 * D_MODEL), row_spec(D_MODEL),
                  res_spec(wa), res_spec(ba), res_spec(wb), res_spec(wo), res_spec(gp)],
        out_specs=row_spec(D_MODEL),
        out_shape=jax.ShapeDtypeStruct((rows, D_MODEL), F32),
        compiler_params=_params("parallel"),
        name="merge",
    )(ua, yb, gates, x, wa, ba, wb, wo, gp)


def _ffn_up_kernel(x1_ref, gpre_ref, wg_ref, wv_ref, stg_ref, stv_ref,
                   cwg_ref, cwv_ref, cbg_ref, cbv_ref, pwg_ref, pwv_ref, pbg_ref, pbv_ref,
                   even_ref, odd_ref, nsg_ref, nsv_ref, h_scr, ag, av, bg, bv, *carry, tm, sh, hist, rc, cw):
    l = pl.program_id(1)
    t = pl.program_id(2)
    n_t = pl.num_programs(2) - 1
    past = (CONV_F - 1) * sh
    mc = min(tm, FFN_ROWS)
    n_m = tm // mc

    @pl.when(t == 0)
    def _():
        h_scr[...] = _rms(x1_ref[0], gpre_ref[...]).astype(BF16)

    if carry:
        @pl.when(jnp.logical_and(t == 0, l == 0))
        def _():
            carry[0][...] = jnp.zeros_like(carry[0])

    def matmuls(half, wins, m):
        cs = slice(half * cw, (half + 1) * cw)
        h = h_scr[m * mc:(m + 1) * mc, :]
        for s, (w_ref, st_ref, ns_ref, win) in enumerate(((wg_ref, stg_ref, nsg_ref, wins[0]),
                                                           (wv_ref, stv_ref, nsv_ref, wins[1]))):
            if m == 0:
                rows = st_ref[0, :, cs]
                if carry:
                    rows = jnp.where(l == 0, rows, carry[0][t, s, hist - past:hist, cs])
                win[hist - past:hist, :] = rows
            win[hist + m * mc:hist + (m + 1) * mc, :] = _dot(h, w_ref[:, cs])
            if m == n_m - 1:
                ns_ref[0, 0, :, cs] = win[hist + tm - past:hist + tm, :]
                if carry:
                    carry[0][t, s, :, cs] = win[tm:tm + hist, :]

    def conv(win, cw_ref, cb_ref, r0, cs):
        out = cb_ref[:, cs]
        for k in range(CONV_F):
            lo = hist - (CONV_F - 1 - k) * sh + r0
            out = out + cw_ref[k:k + 1, cs] * win[lo:lo + rc, :]
        return out

    def gate(half, wins, refs, out_ref, m):
        cs = slice(half * cw, (half + 1) * cw)
        for r0 in range(m * mc, (m + 1) * mc, rc):
            g = _gelu_tanh(conv(wins[0], refs[0], refs[2], r0, cs))
            out_ref[0, r0:r0 + rc, :] = (g * conv(wins[1], refs[1], refs[3], r0, cs)).astype(BF16)

    cur = (cwg_ref, cwv_ref, cbg_ref, cbv_ref)
    prv = (pwg_ref, pwv_ref, pbg_ref, pbv_ref)
    win_a, win_b = (ag, av), (bg, bv)

    @pl.when(t == 0)
    def _():
        for m in range(n_m):
            matmuls(0, win_a, m)
        for m in range(n_m):
            matmuls(1, win_b, m)
            gate(0, win_a, cur, even_ref, m)

    @pl.when(jnp.logical_and(t > 0, t < n_t))
    def _():
        for m in range(n_m):
            matmuls(0, win_a, m)
            gate(1, win_b, prv, odd_ref, m)
        for m in range(n_m):
            matmuls(1, win_b, m)
            gate(0, win_a, cur, even_ref, m)

    @pl.when(t == n_t)
    def _():
        for m in range(n_m):
            gate(1, win_b, prv, odd_ref, m)


def _ffn_up(x1, state, wts, *, tm, sh):
    b, L, _ = x1.shape
    cw = FFN_TILE
    tf = 2 * cw
    nj = D_FF // tf
    past = (CONV_F - 1) * sh
    hist = -(-past // SUBLANES) * SUBLANES
    n_l = L // tm
    carry = [pltpu.VMEM((nj, 2, hist, tf), F32)] if n_l > 1 else []
    cur = lambda t: jnp.minimum(t, nj - 1)
    prv = lambda t: jnp.maximum(t - 1, 0)
    col = lambda rows, off, pick: pl.BlockSpec((rows, tf), lambda i, l, t: (0, pick(t) + off))
    st = lambda off: pl.BlockSpec((1, past, tf), lambda i, l, t: (i, 0, cur(t) + off))
    tail = pl.BlockSpec((1, 1, past, tf), lambda i, l, t: (i, l, 0, cur(t)))
    even, odd, ns_g, ns_v = pl.pallas_call(
        functools.partial(_ffn_up_kernel, tm=tm, sh=sh, hist=hist, rc=min(tm, 32), cw=cw),
        grid=(b, n_l, nj + 1),
        in_specs=[pl.BlockSpec((1, tm, D_MODEL), lambda i, l, t: (i, l, 0)),
                  pl.BlockSpec((1, D_MODEL), lambda i, l, t: (0, 0)),
                  col(D_MODEL, 0, cur), col(D_MODEL, nj, cur), st(0), st(nj),
                  col(CONV_F, 0, cur), col(CONV_F, nj, cur), col(1, 0, cur), col(1, nj, cur),
                  col(CONV_F, 0, prv), col(CONV_F, nj, prv), col(1, 0, prv), col(1, nj, prv)],
        out_specs=[pl.BlockSpec((1, tm, cw), lambda i, l, t: (i, l, cur(t))),
                   pl.BlockSpec((1, tm, cw), lambda i, l, t: (i, l, prv(t))), tail, tail],
        out_shape=[jax.ShapeDtypeStruct((b, L, D_FF // 2), BF16),
                   jax.ShapeDtypeStruct((b, L, D_FF // 2), BF16),
                   jax.ShapeDtypeStruct((b, n_l, past, D_FF), F32),
                   jax.ShapeDtypeStruct((b, n_l, past, D_FF), F32)],
        scratch_shapes=[pltpu.VMEM((tm, D_MODEL), BF16)] + [pltpu.VMEM((hist + tm, cw), F32)] * 4 + carry,
        compiler_params=_params("parallel", "arbitrary", "arbitrary"),
        name="ffn_up",
    )(x1, wts["g_pre2"], wts["w_up"], wts["w_up"], state, state,
      wts["w_dw_f"], wts["w_dw_f"], wts["b_dw_f"], wts["b_dw_f"],
      wts["w_dw_f"], wts["w_dw_f"], wts["b_dw_f"], wts["b_dw_f"])
    return even, odd, jnp.concatenate([ns_g[:, -1], ns_v[:, -1]], axis=-1)


def _ffn_down_kernel(even_ref, odd_ref, we_ref, wo_ref, x1_ref, g_ref, o_ref):
    f = _dot(even_ref[...], we_ref[...]) + _dot(odd_ref[...], wo_ref[...])
    o_ref[...] = x1_ref[...] + _rms(f, g_ref[...])


def _ffn_down(even, odd, w_even, w_odd, x1, g):
    rows = x1.shape[0]
    tm = min(rows, 256)
    res = lambda a: pl.BlockSpec(a.shape, lambda i: (0, 0), pipeline_mode=pl.Buffered(1))
    row = lambda w: pl.BlockSpec((tm, w), lambda i: (i, 0))
    return pl.pallas_call(
        _ffn_down_kernel,
        grid=(rows // tm,),
        in_specs=[row(D_FF // 2), row(D_FF // 2), res(w_even), res(w_odd), row(D_MODEL), res(g)],
        out_specs=row(D_MODEL),
        out_shape=jax.ShapeDtypeStruct((rows, D_MODEL), F32),
        compiler_params=_params("parallel"),
        name="ffn_down",
    )(even, odd, w_even, w_odd, x1, g)


def _post_ln_silu(outs, extras):
    u = outs[0]
    mu = jnp.mean(u, axis=-1, keepdims=True)
    xc = u - mu
    r = lax.rsqrt(jnp.mean(xc * xc, axis=-1, keepdims=True) + EPS)
    return _silu(xc * r * extras[0] + extras[1])


def _post_silu(outs, extras):
    return _silu(outs[0])


def _conv_seq_kernel(*refs, width, tl, n_s, n_x, post, hist, rc):
    u_refs = refs[0:n_s]
    st_refs = refs[n_s:2 * n_s]
    w_refs = refs[2 * n_s:3 * n_s]
    b_refs = refs[3 * n_s:4 * n_s]
    x_refs = refs[4 * n_s:4 * n_s + n_x]
    o_ref = refs[4 * n_s + n_x]
    ns_refs = refs[4 * n_s + n_x + 1:4 * n_s + n_x + 1 + n_s]
    win_refs = refs[4 * n_s + n_x + 1 + n_s:4 * n_s + n_x + 1 + 2 * n_s]
    shf_refs = refs[4 * n_s + n_x + 1 + 2 * n_s:]
    past = width - 1
    l = pl.program_id(2)
    n_shf = hist + tl - SUBLANES

    for s in range(n_s):
        win = win_refs[s]

        @pl.when(l == 0)
        def _(win=win, s=s):
            if hist > past:
                win[0:hist - past, :] = jnp.zeros((hist - past, win.shape[1]), F32)
            win[hist - past:hist, :] = st_refs[s][0]

        @pl.when(l > 0)
        def _(win=win):
            win[0:hist, :] = win[tl:tl + hist, :]

        win[hist:hist + tl, :] = u_refs[s][0]
        for r in range(1, SUBLANES):
            shf_refs[s][r - 1, :, :] = win[r:r + n_shf, :]

    def tap(s, lo):
        a, r = divmod(lo, SUBLANES)
        if r == 0:
            return win_refs[s][lo:lo + rc, :]
        return shf_refs[s][r - 1, a * SUBLANES:a * SUBLANES + rc, :]

    extras = [x[...] for x in x_refs]
    for c in range(tl // rc):
        outs = []
        for s in range(n_s):
            acc = b_refs[s][...] + w_refs[s][0:1, :] * tap(s, c * rc + hist - past)
            for k in range(1, width):
                acc = acc + w_refs[s][k:k + 1, :] * tap(s, c * rc + hist - past + k)
            outs.append(acc)
        o_ref[0, c * rc:(c + 1) * rc, :] = post(outs, extras).astype(o_ref.dtype)

    @pl.when(l == pl.num_programs(2) - 1)
    def _():
        for s in range(n_s):
            ns_refs[s][0] = win_refs[s][hist + tl - past:hist + tl, :]


def _conv_seq(u, state, w, bias, extras, post, *, width, ct, tl, n_s, out_w, out_dtype, name):
    b, L, _ = u.shape
    past = width - 1
    hist = -(-past // SUBLANES) * SUBLANES
    nj = out_w // ct
    rc = min(tl, 32)
    kern = functools.partial(_conv_seq_kernel, width=width, tl=tl, n_s=n_s, n_x=len(extras),
                             post=post, hist=hist, rc=rc)
    in_specs = ([pl.BlockSpec((1, tl, ct), lambda i, j, l, s=s: (i, l, j + s * nj)) for s in range(n_s)]
                + [pl.BlockSpec((1, past, ct), lambda i, j, l, s=s: (i, 0, j + s * nj)) for s in range(n_s)]
                + [pl.BlockSpec((width, ct), lambda i, j, l, s=s: (0, j + s * nj)) for s in range(n_s)]
                + [pl.BlockSpec((1, ct), lambda i, j, l, s=s: (0, j + s * nj)) for s in range(n_s)]
                + [pl.BlockSpec((1, ct), lambda i, j, l: (0, j)) for _ in extras])
    out_specs = ([pl.BlockSpec((1, tl, ct), lambda i, j, l: (i, l, j))]
                 + [pl.BlockSpec((1, past, ct), lambda i, j, l: (i, 0, j)) for _ in range(n_s)])
    out_shape = ([jax.ShapeDtypeStruct((b, L, out_w), out_dtype)]
                 + [jax.ShapeDtypeStruct((b, past, out_w), F32) for _ in range(n_s)])
    res = pl.pallas_call(
        kern,
        grid=(b, nj, L // tl),
        in_specs=in_specs,
        out_specs=out_specs,
        out_shape=out_shape,
        scratch_shapes=[pltpu.VMEM((hist + tl, ct), F32) for _ in range(n_s)]
                       + [pltpu.VMEM((SUBLANES - 1, hist + tl - SUBLANES, ct), F32) for _ in range(n_s)],
        compiler_params=_params("parallel", "parallel", "arbitrary"),
        name=name,
    )(*([u] * n_s), *([state] * n_s), *([w] * n_s), *([bias] * n_s), *extras)
    return res[0], res[1:]


def _conv_slab_kernel(*refs, width, steps, n_s, n_x, post):
    u_refs = refs[0:n_s]
    st_refs = refs[n_s:2 * n_s]
    w_refs = refs[2 * n_s:3 * n_s]
    b_refs = refs[3 * n_s:4 * n_s]
    x_refs = refs[4 * n_s:4 * n_s + n_x]
    o_ref = refs[4 * n_s + n_x]
    ns_refs = refs[4 * n_s + n_x + 1:]
    past = width - 1

    def slab(s, i):
        return st_refs[s][i] if i < past else u_refs[s][i - past]

    extras = [x[...] for x in x_refs]
    for t in range(steps):
        outs = []
        for s in range(n_s):
            acc = b_refs[s][...] + w_refs[s][0:1, :] * slab(s, t)
            for k in range(1, width):
                acc = acc + w_refs[s][k:k + 1, :] * slab(s, t + k)
            outs.append(acc)
        o_ref[t] = post(outs, extras).astype(o_ref.dtype)
    for s in range(n_s):
        for i in range(past):
            ns_refs[s][i] = slab(s, i + steps)


def _conv_slab(u, state, w, bias, extras, post, *, width, ct, nbt, n_s, out_w, out_dtype, name):
    steps, nb, _ = u.shape
    past = width - 1
    nj = out_w // ct
    kern = functools.partial(_conv_slab_kernel, width=width, steps=steps, n_s=n_s, n_x=len(extras), post=post)
    in_specs = ([pl.BlockSpec((steps, nbt, ct), lambda i, j, s=s: (0, i, j + s * nj)) for s in range(n_s)]
                + [pl.BlockSpec((past, nbt, ct), lambda i, j, s=s: (0, i, j + s * nj)) for s in range(n_s)]
                + [pl.BlockSpec((width, ct), lambda i, j, s=s: (0, j + s * nj)) for s in range(n_s)]
                + [pl.BlockSpec((1, ct), lambda i, j, s=s: (0, j + s * nj)) for s in range(n_s)]
                + [pl.BlockSpec((1, ct), lambda i, j: (0, j)) for _ in extras])
    out_specs = ([pl.BlockSpec((steps, nbt, ct), lambda i, j: (0, i, j))]
                 + [pl.BlockSpec((past, nbt, ct), lambda i, j: (0, i, j)) for _ in range(n_s)])
    out_shape = ([jax.ShapeDtypeStruct((steps, nb, out_w), out_dtype)]
                 + [jax.ShapeDtypeStruct((past, nb, out_w), F32) for _ in range(n_s)])
    res = pl.pallas_call(
        kern,
        grid=(nb // nbt, nj),
        in_specs=in_specs,
        out_specs=out_specs,
        out_shape=out_shape,
        compiler_params=_params("parallel", "parallel"),
        name=name,
    )(*([u] * n_s), *([state] * n_s), *([w] * n_s), *([bias] * n_s), *extras)
    return res[0], res[1:]


def _ssd_kernel(*refs, q, bb, conv):
    xbc_ref, dt_ref, zs_ref, h0_ref, alog_ref, dsk_ref, gn_ref, e_ref, et_ref = refs[:9]
    if conv:
        stb_ref, cw_ref, cb_ref, y_ref, ht_ref, nsb_ref, h_scr, y_scr, win = refs[9:]
    else:
        y_ref, ht_ref, h_scr, y_scr = refs[9:]
    qk = SSD_KEYS
    c = pl.program_id(1)

    @pl.when(c == 0)
    def _():
        h_scr[...] = h0_ref[...]

    a = -jnp.exp(alog_ref[...])
    e = e_ref[...]
    row = lax.broadcasted_iota(jnp.int32, (qk, qk), 0)
    col = lax.broadcasted_iota(jnp.int32, (qk, qk), 1)
    tril = jnp.where(row >= col, 1.0, 0.0).astype(BF16)
    causal = row[:q, :] >= col[:q, :]
    key_head = lax.broadcasted_iota(jnp.int32, (qk, GROUP_W), 1) // HEAD_DIM

    def expand(v):
        return sum(_dot(p, e) for p in _split_bf16(v, 2))

    def pad_keys(v):
        if q == qk:
            return v
        return jnp.concatenate([v, jnp.zeros((qk - q, v.shape[1]), v.dtype)], axis=0)

    for s in range(bb):
        if conv:
            past = CONV_B - 1
            hist = SUBLANES

            @pl.when(c == 0)
            def _(s=s):
                win[s, hist - past:hist, :] = stb_ref[s]

            @pl.when(c > 0)
            def _(s=s):
                win[s, 0:hist, :] = win[s, q:q + hist, :]

            win[s, hist:hist + q, :] = xbc_ref[s]
            pre = cb_ref[...] + cw_ref[0:1, :] * win[s, hist - past:hist - past + q, :]
            for k in range(1, CONV_B):
                pre = pre + cw_ref[k:k + 1, :] * win[s, hist - past + k:hist - past + k + q, :]
            xq = _silu(pre)
            nsb_ref[s] = win[s, hist + q - past:hist + q, :]
        else:
            xq = xbc_ref[s]

        x = xq[:, :D_B]
        bm = pad_keys(xq[:, D_B:D_B + N_GROUPS * D_STATE]).astype(BF16)
        cm = xq[:, D_B + N_GROUPS * D_STATE:].astype(BF16)
        dt = dt_ref[s]
        d_a = pad_keys(dt * a)
        cum = sum(_dot(tril, p) for p in _split_bf16(d_a, 3))
        cum_t = cum.T
        cum_last = cum[qk - 1:qk, :]
        cum_q = cum[:q, :]

        xdt = x * expand(dt)
        xw = pad_keys(xdt * expand(jnp.exp(cum_last - cum_q))).astype(BF16)
        xdt = pad_keys(xdt).astype(BF16)
        ecum = expand(jnp.exp(cum_q))
        decay_t = jnp.broadcast_to(jnp.exp(cum_t[:, qk - 1:qk]), (LANES, D_STATE))
        decay_rows = sum(_dot(et_ref[...], p) for p in _split_bf16(decay_t, 2))

        for g in range(N_GROUPS):
            cols = slice(g * GROUP_W, (g + 1) * GROUP_W)
            bg = bm[:, g * D_STATE:(g + 1) * D_STATE]
            cg = cm[:, g * D_STATE:(g + 1) * D_STATE]
            cb = lax.dot_general(cg, bg, (((1,), (1,)), ((), ())), preferred_element_type=F32)
            xdt_g = xdt[:, cols]
            scores, keys = [], []
            for r in range(HEADS_PER_GROUP):
                hd = g * HEADS_PER_GROUP + r
                seg = jnp.where(causal, cum_q[:, hd:hd + 1] - cum_t[hd:hd + 1, :], -1e30)
                scores.append((cb * jnp.exp(seg)).astype(BF16))
                keys.append(jnp.where(key_head == r, xdt_g, jnp.zeros_like(xdt_g)))
            y_g = _dot(jnp.concatenate(scores, axis=1), jnp.concatenate(keys, axis=0))
            h_g = h_scr[s, cols, :]
            y_inter = lax.dot_general(cg, h_g.astype(BF16), (((1,), (1,)), ((), ())),
                                      preferred_element_type=F32)
            y_scr[s, :, cols] = y_g + y_inter * ecum[:, cols]
            s_g = lax.dot_general(xw[:, cols], bg, (((0,), (0,)), ((), ())), preferred_element_type=F32)
            h_scr[s, cols, :] = decay_rows[cols, :] * h_g + s_g

        y = y_scr[s] + dsk_ref[...] * x
        y_ref[s] = _rms(y * zs_ref[s], gn_ref[...]).astype(y_ref.dtype)

    @pl.when(c == pl.num_programs(1) - 1)
    def _():
        ht_ref[...] = h_scr[...]


def _ssd(xbc, dt, zs, h0, alog, dsk, gn, e, et, conv_b=None, *, q, bb):
    b, L, _ = xbc.shape
    seq = lambda w: pl.BlockSpec((bb, q, w), lambda i, c: (i, c, 0))
    full = lambda arr: pl.BlockSpec(arr.shape, lambda i, c: (0,) * arr.ndim)
    state = pl.BlockSpec((bb, D_B, D_STATE), lambda i, c: (i, 0, 0))
    cstate = pl.BlockSpec((bb, CONV_B - 1, D_XBC), lambda i, c: (i, 0, 0))
    conv = conv_b is not None
    return pl.pallas_call(
        functools.partial(_ssd_kernel, q=q, bb=bb, conv=conv),
        grid=(b // bb, L // q),
        in_specs=[seq(D_XBC), seq(LANES), seq(D_B), state,
                  full(alog), full(dsk), full(gn), full(e), full(et)]
                 + ([cstate, full(conv_b[1]), full(conv_b[2])] if conv else []),
        out_specs=[seq(D_B), state] + ([cstate] if conv else []),
        out_shape=[jax.ShapeDtypeStruct((b, L, D_B), BF16),
                   jax.ShapeDtypeStruct((b, D_B, D_STATE), F32)]
                  + ([jax.ShapeDtypeStruct((b, CONV_B - 1, D_XBC), F32)] if conv else []),
        scratch_shapes=[pltpu.VMEM((bb, D_B, D_STATE), F32), pltpu.VMEM((bb, q, D_B), F32)]
                       + ([pltpu.VMEM((bb, SUBLANES + q, D_XBC), F32)] if conv else []),
        compiler_params=_params("parallel", "arbitrary"),
        name="ssd",
    )(xbc, dt, zs, h0, alog, dsk, gn, e, et, *(conv_b or ()))


def _layer(x, wts, conv_a, ssd_fn, ffn_fn):
    h = _norm(x, wts["g_pre1"])
    w_main = wts["w_main"]
    uglu = _proj(h, [w_main, w_main], [], _epi_glu, 512, F32, "in_glu", n=D_A, cols=[0, D_A])
    dt = _proj(h, [wts["w_dt"]], [wts["dt_bias"]], _epi_dt, LANES, F32, "in_dt")
    zs = _proj(h, [w_main], [], _epi_silu, 512, F32, "in_z", n=D_B, cols=[COL_Z])
    xbc = _proj(h, [w_main], [], _epi_id, 512, F32, "in_xbc", n=D_XBC, cols=[COL_XBC])
    gates = _proj(h, [wts["w_g"]], [wts["b_gate"]], _epi_gate, 512, F32, "in_gates")

    ua, st_a = conv_a(uglu)
    yb, st_b, st_h = ssd_fn(xbc, dt, zs)

    x1 = _merge(ua, yb, gates, x, wts["w_a_out"], wts["b_a_out"], wts["w_b_out"], wts["w_o"], wts["g_post1"])
    x2, st_f = ffn_fn(x1)
    return x2, st_a, st_b, st_h, st_f


def _seq_group(x, wts, st_a, st_b, st_h, st_f, tl, q):
    b, L, _ = x.shape

    def conv_a(uglu):
        ua, (ns,) = _conv_seq(uglu.reshape(b, L, D_A), st_a, wts["w_dw_a"], wts["b_dw_a"],
                              [wts["g_ln_a"], wts["b_ln_a"]], _post_ln_silu, width=CONV_A, ct=D_A,
                              tl=min(tl, 256), n_s=1, out_w=D_A, out_dtype=BF16, name="conv_a")
        return ua.reshape(b * L, D_A), ns

    def ssd_fn(xbc, dt, zs):
        yb, ht, ns = _ssd(xbc.reshape(b, L, D_XBC), dt.reshape(b, L, LANES), zs.reshape(b, L, D_B), st_h,
                          wts["a_log"], wts["d_skip"], wts["g_norm_b"], wts["expand"], wts["expand_t"],
                          (st_b, wts["w_dw_b"], wts["b_dw_b"]), q=q, bb=1)
        return yb.reshape(b * L, D_B), ns, ht

    def ffn_fn(x1):
        even, odd, ns = _ffn_up(x1.reshape(b, L, D_MODEL), st_f, wts, tm=min(L, 1024), sh=1)
        x2 = _ffn_down(even.reshape(b * L, D_FF // 2), odd.reshape(b * L, D_FF // 2),
                       wts["w_down_even"], wts["w_down_odd"], x1, wts["g_post2"])
        return x2, ns

    x2, ns_a, ns_b, ns_h, ns_f = _layer(x.reshape(b * L, D_MODEL), wts, conv_a, ssd_fn, ffn_fn)
    return x2.reshape(b, L, D_MODEL), ns_a, ns_b, ns_h, ns_f


def _step_group(x, wts, st_a, st_b, st_h, st_f):
    nb, L, _ = x.shape
    tmaj = lambda s: jnp.transpose(s, (1, 0, 2))
    lpad = -(-L // SUBLANES) * SUBLANES

    def conv_a(uglu):
        ua, (ns,) = _conv_slab(uglu.reshape(L, nb, D_A), tmaj(st_a), wts["w_dw_a"], wts["b_dw_a"],
                               [wts["g_ln_a"], wts["b_ln_a"]], _post_ln_silu, width=CONV_A, ct=D_A,
                               nbt=32, n_s=1, out_w=D_A, out_dtype=BF16, name="conv_a_step")
        return ua.reshape(L * nb, D_A), tmaj(ns)

    def bmaj_pad(v):
        return jnp.pad(tmaj(v), ((0, 0), (0, lpad - L), (0, 0)))

    def ssd_fn(xbc, dt, zs):
        act, (ns,) = _conv_slab(xbc.reshape(L, nb, D_XBC), tmaj(st_b), wts["w_dw_b"], wts["b_dw_b"], [],
                                _post_silu, width=CONV_B, ct=512, nbt=nb, n_s=1, out_w=D_XBC,
                                out_dtype=F32, name="conv_b_step")
        yb, ht = _ssd(bmaj_pad(act), bmaj_pad(dt.reshape(L, nb, LANES)), bmaj_pad(zs.reshape(L, nb, D_B)),
                      st_h, wts["a_log"], wts["d_skip"], wts["g_norm_b"], wts["expand"], wts["expand_t"], q=lpad, bb=4)
        return tmaj(yb[:, :L]).reshape(L * nb, D_B), tmaj(ns), ht

    def ffn_fn(x1):
        past = CONV_F - 1
        even, odd, ns = _ffn_up(x1.reshape(1, L * nb, D_MODEL), tmaj(st_f).reshape(1, past * nb, 2 * D_FF), wts,
                                tm=L * nb, sh=nb)
        x2 = _ffn_down(even.reshape(L * nb, D_FF // 2), odd.reshape(L * nb, D_FF // 2),
                       wts["w_down_even"], wts["w_down_odd"], x1, wts["g_post2"])
        return x2, tmaj(ns.reshape(past, nb, 2 * D_FF))

    x2, ns_a, ns_b, ns_h, ns_f = _layer(tmaj(x).reshape(L * nb, D_MODEL), wts, conv_a, ssd_fn, ffn_fn)
    return tmaj(x2.reshape(L, nb, D_MODEL)), ns_a, ns_b, ns_h, ns_f


def _layer_weights(l, g_pre1, g_post1, w_in, b_gate, w_dw_a, b_dw_a, g_ln_a, b_ln_a, w_a_out, b_a_out,
                   w_dw_b, b_dw_b, dt_bias, a_log, d_skip, g_norm_b, w_b_out, w_o,
                   g_pre2, g_post2, w_up, w_dw_f, b_dw_f, w_down):
    row = lambda v: v[l].reshape(1, -1)
    lane_pad = lambda v: jnp.pad(v, ((0, 0), (0, LANES - v.shape[1])))
    w = w_in[l]
    head = jnp.arange(LANES, dtype=jnp.int32)[:, None]
    chan_head = (jnp.arange(D_B, dtype=jnp.int32) // HEAD_DIM)[None, :]
    expand = (head == chan_head).astype(BF16)
    wd = w_down[l].astype(BF16).reshape(D_FF // (2 * FFN_TILE), 2, FFN_TILE, D_MODEL)
    return {
        "g_pre1": row(g_pre1), "g_post1": row(g_post1), "g_pre2": row(g_pre2), "g_post2": row(g_post2),
        "w_main": w[:, :COL_DT].astype(BF16),
        "w_dt": lane_pad(w[:, COL_DT:COL_GATE]).astype(BF16), "w_g": w[:, COL_GATE:].astype(BF16),
        "dt_bias": lane_pad(row(dt_bias)), "b_gate": row(b_gate),
        "w_dw_a": w_dw_a[l], "b_dw_a": row(b_dw_a), "g_ln_a": row(g_ln_a), "b_ln_a": row(b_ln_a),
        "w_a_out": w_a_out[l].astype(BF16), "b_a_out": row(b_a_out),
        "w_dw_b": w_dw_b[l], "b_dw_b": row(b_dw_b),
        "a_log": lane_pad(row(a_log)), "d_skip": jnp.repeat(d_skip[l], HEAD_DIM).reshape(1, D_B),
        "g_norm_b": row(g_norm_b), "expand": expand, "expand_t": expand.T,
        "w_b_out": w_b_out[l].astype(BF16), "w_o": w_o[l].astype(BF16),
        "w_up": w_up[l].astype(BF16), "w_dw_f": w_dw_f[l], "b_dw_f": row(b_dw_f),
        "w_down_even": wd[:, 0].reshape(D_FF // 2, D_MODEL), "w_down_odd": wd[:, 1].reshape(D_FF // 2, D_MODEL),
    }


def kernel(x_prompt, x_sample, state_conv_a, state_conv_b, state_ssm, state_conv_ffn, meta_tokens, g_pre1, g_post1, w_in, b_gate, w_dw_a, b_dw_a, g_ln_a, b_ln_a, w_a_out, b_a_out, w_dw_b, b_dw_b, dt_bias, a_log, d_skip, g_norm_b, w_b_out, w_o, g_pre2, g_post2, w_up, w_dw_f, b_dw_f, w_down):
    depth = w_in.shape[0]
    bp = x_prompt.shape[0]
    nb = x_sample.shape[0]
    xm = meta_tokens.astype(x_prompt.dtype)[None]
    xp, xs = x_prompt, x_sample
    pa, pb, ph, pf = [], [], [], []
    sa, sb, sh, sf = [], [], [], []
    for l in range(depth):
        wts = _layer_weights(l, g_pre1, g_post1, w_in, b_gate, w_dw_a, b_dw_a, g_ln_a, b_ln_a, w_a_out, b_a_out,
                             w_dw_b, b_dw_b, dt_bias, a_log, d_skip, g_norm_b, w_b_out, w_o,
                             g_pre2, g_post2, w_up, w_dw_f, b_dw_f, w_down)
        xm, m_a, m_b, m_h, m_f = _seq_group(
            xm, wts, jnp.zeros((1, CONV_A - 1, D_A), F32), jnp.zeros((1, CONV_B - 1, D_XBC), F32),
            jnp.zeros((1, D_B, D_STATE), F32), jnp.zeros((1, CONV_F - 1, 2 * D_FF), F32), tl=N_META, q=SUBLANES)
        rep = lambda s: jnp.broadcast_to(s, (bp,) + s.shape[1:])
        xp, c_a, c_b, c_h, c_f = _seq_group(xp, wts, rep(m_a), rep(m_b), rep(m_h), rep(m_f), tl=512, q=SSD_KEYS)
        pa.append(c_a); pb.append(c_b); ph.append(c_h.reshape(bp, N_HEADS, HEAD_DIM, D_STATE)); pf.append(c_f)
        xs, d_a, d_b, d_h, d_f = _step_group(
            xs, wts, state_conv_a[l], state_conv_b[l], state_ssm[l].reshape(nb, D_B, D_STATE), state_conv_ffn[l])
        sa.append(d_a); sb.append(d_b); sh.append(d_h.reshape(nb, N_HEADS, HEAD_DIM, D_STATE)); sf.append(d_f)
    return (xp, xs, jnp.stack(pa), jnp.stack(pb), jnp.stack(ph), jnp.stack(pf),
            jnp.stack(sa), jnp.stack(sb), jnp.stack(sh), jnp.stack(sf))
```

```python
import functools

import jax
import jax.numpy as jnp
from jax import lax
from jax.experimental import pallas as pl
from jax.experimental.pallas import tpu as pltpu

D_MODEL = 2048
N_META = 16
D_A = 1024
CONV_A = 31
D_B = 2048
HEAD_DIM = 64
N_HEADS = D_B // HEAD_DIM
N_GROUPS = 8
HEADS_PER_GROUP = N_HEADS // N_GROUPS
GROUP_W = HEADS_PER_GROUP * HEAD_DIM
D_STATE = 128
CONV_B = 4
D_FF = 5632
CONV_F = 3
EPS = 1e-6
D_XBC = D_B + 2 * N_GROUPS * D_STATE
COL_Z = 2 * D_A
COL_XBC = COL_Z + D_B
COL_DT = COL_XBC + D_XBC
COL_GATE = COL_DT + N_HEADS

LANES = 128
SUBLANES = 8
SSD_KEYS = 128
VMEM_LIMIT = 56 * 1024 * 1024

F32 = jnp.float32
BF16 = jnp.bfloat16


def _params(*sem):
    return pltpu.CompilerParams(dimension_semantics=sem, vmem_limit_bytes=VMEM_LIMIT)


def _sigmoid(x):
    return 1.0 / (1.0 + jnp.exp(-x))


def _silu(x):
    return x * _sigmoid(x)


def _softplus(x):
    return jnp.maximum(x, 0.0) + jnp.log1p(jnp.exp(-jnp.abs(x)))


def _gelu_tanh(x):
    return 0.5 * x * (1.0 + jnp.tanh(0.7978845608028654 * (x + 0.044715 * (x * x * x))))


def _rms(x, g):
    r = lax.rsqrt(jnp.mean(x * x, axis=-1, keepdims=True) + EPS)
    return x * r * g


def _dot(a, b):
    return jnp.dot(a, b, preferred_element_type=F32)


def _split_bf16(v, parts):
    out = []
    for _ in range(parts):
        p = v.astype(BF16)
        out.append(p)
        v = v - p.astype(F32)
    return out


def _norm_kernel(x_ref, g_ref, o_ref):
    o_ref[...] = _rms(x_ref[...], g_ref[...]).astype(o_ref.dtype)


def _norm(x, g):
    rows, d = x.shape
    tm = min(rows, 512)
    return pl.pallas_call(
        _norm_kernel,
        grid=(rows // tm,),
        in_specs=[pl.BlockSpec((tm, d), lambda i: (i, 0)), pl.BlockSpec((1, d), lambda i: (0, 0))],
        out_specs=pl.BlockSpec((tm, d), lambda i: (i, 0)),
        out_shape=jax.ShapeDtypeStruct((rows, d), BF16),
        compiler_params=_params("parallel"),
        name="norm",
    )(x, g)


def _proj_kernel(*refs, n_w, n_b, epilogue):
    h_ref = refs[0]
    w_refs = refs[1:1 + n_w]
    b_refs = refs[1 + n_w:1 + n_w + n_b]
    o_ref = refs[-1]
    h = h_ref[...]
    accs = [_dot(h, w[...]) for w in w_refs]
    o_ref[...] = epilogue(accs, [b[...] for b in b_refs]).astype(o_ref.dtype)


def _proj(h, ws, bs, epilogue, tn, out_dtype, name, n=None, cols=None):
    rows, d = h.shape
    n = n or ws[0].shape[1]
    cols = cols or [0] * len(ws)
    tm = min(rows, 1024)
    kern = functools.partial(_proj_kernel, n_w=len(ws), n_b=len(bs), epilogue=epilogue)
    return pl.pallas_call(
        kern,
        grid=(rows // tm, n // tn),
        in_specs=[pl.BlockSpec((tm, d), lambda i, j: (i, 0))]
                 + [pl.BlockSpec((d, tn), lambda i, j, c=c // tn: (0, j + c)) for c in cols]
                 + [pl.BlockSpec((1, tn), lambda i, j: (0, j)) for _ in bs],
        out_specs=pl.BlockSpec((tm, tn), lambda i, j: (i, j)),
        out_shape=jax.ShapeDtypeStruct((rows, n), out_dtype),
        compiler_params=_params("parallel", "parallel"),
        name=name,
    )(h, *ws, *bs)


def _epi_glu(a, b):
    return a[0] * _sigmoid(a[1])


def _epi_dt(a, b):
    lane = lax.broadcasted_iota(jnp.int32, a[0].shape, 1)
    return jnp.where(lane < N_HEADS, _softplus(a[0] + b[0]), 0.0)


def _epi_silu(a, b):
    return _silu(a[0])


def _epi_id(a, b):
    return a[0]


def _epi_gate(a, b):
    return _sigmoid(a[0] + b[0])


def _merge_kernel(ua_ref, yb_ref, gt_ref, x_ref, wa_ref, ba_ref, wb_ref, wo_ref, gp_ref, o_ref):
    ya = _dot(ua_ref[...], wa_ref[...]) + ba_ref[...]
    yb = _dot(yb_ref[...], wb_ref[...])
    mix = gt_ref[:, :D_MODEL] * ya + gt_ref[:, D_MODEL:] * yb
    m = _dot(mix.astype(BF16), wo_ref[...])
    o_ref[...] = x_ref[...] + _rms(m, gp_ref[...])


def _merge(ua, yb, gates, x, wa, ba, wb, wo, gp):
    rows = x.shape[0]
    tm = min(rows, 256)
    row_spec = lambda w: pl.BlockSpec((tm, w), lambda i: (i, 0))
    res_spec = lambda a: pl.BlockSpec(a.shape, lambda i: (0, 0), pipeline_mode=pl.Buffered(1))
    return pl.pallas_call(
        _merge_kernel,
        grid=(rows // tm,),
        in_specs=[row_spec(D_A), row_spec(D_B), row_spec(2 * D_MODEL), row_spec(D_MODEL),
                  res_spec(wa), res_spec(ba), res_spec(wb), res_spec(wo), res_spec(gp)],
        out_specs=row_spec(D_MODEL),
        out_shape=jax.ShapeDtypeStruct((rows, D_MODEL), F32),
        compiler_params=_params("parallel"),
        name="merge",
    )(ua, yb, gates, x, wa, ba, wb, wo, gp)


def _ffn_kernel(x1_ref, gpre_ref, wg_ref, wv_ref, wd_ref, stg_ref, stv_ref, cwg_ref, cwv_ref, cbg_ref, cbv_ref,
                gpost_ref, o_ref, nsg_ref, nsv_ref, h_scr, acc_scr, act_scr, win_g, win_v, *carry,
                tm, sh, hist, rc):
    l = pl.program_id(1)
    j = pl.program_id(2)
    past = (CONV_F - 1) * sh

    @pl.when(j == 0)
    def _():
        h_scr[...] = _rms(x1_ref[0], gpre_ref[...]).astype(BF16)
        acc_scr[...] = jnp.zeros_like(acc_scr)

    h = h_scr[...]
    for s, (st_ref, win) in enumerate(((stg_ref, win_g), (stv_ref, win_v))):
        if carry:
            @pl.when(l == 0)
            def _(win=win, st_ref=st_ref):
                win[hist - past:hist, :] = st_ref[0]

            @pl.when(l > 0)
            def _(win=win, s=s):
                win[0:hist, :] = carry[0][j, s]
        else:
            win[hist - past:hist, :] = st_ref[0]

    def conv(win, cw_ref, cb_ref, r0, cs):
        out = cb_ref[:, cs]
        for k in range(CONV_F):
            lo = hist - (CONV_F - 1 - k) * sh + r0
            out = out + cw_ref[k:k + 1, cs] * win[lo:lo + rc, cs]
        return out

    tf = win_g.shape[1]
    cw = min(tf, 256)
    for c0 in range(0, tf, cw):
        cs = slice(c0, c0 + cw)
        win_g[hist:hist + tm, cs] = _dot(h, wg_ref[:, cs])
        win_v[hist:hist + tm, cs] = _dot(h, wv_ref[:, cs])
        for r0 in range(0, tm, rc):
            gate = _gelu_tanh(conv(win_g, cwg_ref, cbg_ref, r0, cs))
            act_scr[r0:r0 + rc, cs] = (gate * conv(win_v, cwv_ref, cbv_ref, r0, cs)).astype(BF16)
    acc_scr[...] += _dot(act_scr[...], wd_ref[...])

    for s, (ns_ref, win) in enumerate(((nsg_ref, win_g), (nsv_ref, win_v))):
        ns_ref[0, 0] = win[hist + tm - past:hist + tm, :]
        if carry:
            carry[0][j, s] = win[tm:tm + hist, :]

    @pl.when(j == pl.num_programs(2) - 1)
    def _():
        o_ref[0] = x1_ref[0] + _rms(acc_scr[...], gpost_ref[...])


def _ffn(x1, state, wts, *, tm, sh):
    b, L, _ = x1.shape
    tf = 512
    nj = D_FF // tf
    past = (CONV_F - 1) * sh
    hist = -(-past // SUBLANES) * SUBLANES
    n_l = L // tm
    carry = [pltpu.VMEM((nj, 2, hist, tf), F32)] if n_l > 1 else []
    col = lambda rows, off: pl.BlockSpec((rows, tf), lambda i, l, j: (0, j + off))
    st = lambda off: pl.BlockSpec((1, past, tf), lambda i, l, j: (i, 0, j + off))
    tail = pl.BlockSpec((1, 1, past, tf), lambda i, l, j: (i, l, 0, j))
    vec = pl.BlockSpec((1, D_MODEL), lambda i, l, j: (0, 0))
    xblk = pl.BlockSpec((1, tm, D_MODEL), lambda i, l, j: (i, l, 0))
    out, ns_g, ns_v = pl.pallas_call(
        functools.partial(_ffn_kernel, tm=tm, sh=sh, hist=hist, rc=min(tm, 32)),
        grid=(b, n_l, nj),
        in_specs=[xblk, vec, col(D_MODEL, 0), col(D_MODEL, nj),
                  pl.BlockSpec((tf, D_MODEL), lambda i, l, j: (j, 0)),
                  st(0), st(nj), col(CONV_F, 0), col(CONV_F, nj), col(1, 0), col(1, nj), vec],
        out_specs=[xblk, tail, tail],
        out_shape=[jax.ShapeDtypeStruct((b, L, D_MODEL), F32),
                   jax.ShapeDtypeStruct((b, n_l, past, D_FF), F32),
                   jax.ShapeDtypeStruct((b, n_l, past, D_FF), F32)],
        scratch_shapes=[pltpu.VMEM((tm, D_MODEL), BF16), pltpu.VMEM((tm, D_MODEL), F32),
                        pltpu.VMEM((tm, tf), BF16),
                        pltpu.VMEM((hist + tm, tf), F32), pltpu.VMEM((hist + tm, tf), F32)] + carry,
        compiler_params=_params("parallel", "arbitrary", "arbitrary"),
        name="ffn",
    )(x1, wts["g_pre2"], wts["w_up"], wts["w_up"], wts["w_down"], state, state,
      wts["w_dw_f"], wts["w_dw_f"], wts["b_dw_f"], wts["b_dw_f"], wts["g_post2"])
    return out, jnp.concatenate([ns_g[:, -1], ns_v[:, -1]], axis=-1)


def _post_ln_silu(outs, extras):
    u = outs[0]
    mu = jnp.mean(u, axis=-1, keepdims=True)
    xc = u - mu
    r = lax.rsqrt(jnp.mean(xc * xc, axis=-1, keepdims=True) + EPS)
    return _silu(xc * r * extras[0] + extras[1])


def _post_silu(outs, extras):
    return _silu(outs[0])


def _conv_seq_kernel(*refs, width, tl, n_s, n_x, post, hist, rc):
    u_refs = refs[0:n_s]
    st_refs = refs[n_s:2 * n_s]
    w_refs = refs[2 * n_s:3 * n_s]
    b_refs = refs[3 * n_s:4 * n_s]
    x_refs = refs[4 * n_s:4 * n_s + n_x]
    o_ref = refs[4 * n_s + n_x]
    ns_refs = refs[4 * n_s + n_x + 1:4 * n_s + n_x + 1 + n_s]
    win_refs = refs[4 * n_s + n_x + 1 + n_s:4 * n_s + n_x + 1 + 2 * n_s]
    shf_refs = refs[4 * n_s + n_x + 1 + 2 * n_s:]
    past = width - 1
    l = pl.program_id(2)
    n_shf = hist + tl - SUBLANES

    for s in range(n_s):
        win = win_refs[s]

        @pl.when(l == 0)
        def _(win=win, s=s):
            if hist > past:
                win[0:hist - past, :] = jnp.zeros((hist - past, win.shape[1]), F32)
            win[hist - past:hist, :] = st_refs[s][0]

        @pl.when(l > 0)
        def _(win=win):
            win[0:hist, :] = win[tl:tl + hist, :]

        win[hist:hist + tl, :] = u_refs[s][0]
        for r in range(1, SUBLANES):
            shf_refs[s][r - 1, :, :] = win[r:r + n_shf, :]

    def tap(s, lo):
        a, r = divmod(lo, SUBLANES)
        if r == 0:
            return win_refs[s][lo:lo + rc, :]
        return shf_refs[s][r - 1, a * SUBLANES:a * SUBLANES + rc, :]

    extras = [x[...] for x in x_refs]
    for c in range(tl // rc):
        outs = []
        for s in range(n_s):
            acc = b_refs[s][...] + w_refs[s][0:1, :] * tap(s, c * rc + hist - past)
            for k in range(1, width):
                acc = acc + w_refs[s][k:k + 1, :] * tap(s, c * rc + hist - past + k)
            outs.append(acc)
        o_ref[0, c * rc:(c + 1) * rc, :] = post(outs, extras).astype(o_ref.dtype)

    @pl.when(l == pl.num_programs(2) - 1)
    def _():
        for s in range(n_s):
            ns_refs[s][0] = win_refs[s][hist + tl - past:hist + tl, :]


def _conv_seq(u, state, w, bias, extras, post, *, width, ct, tl, n_s, out_w, out_dtype, name):
    b, L, _ = u.shape
    past = width - 1
    hist = -(-past // SUBLANES) * SUBLANES
    nj = out_w // ct
    rc = min(tl, 32)
    kern = functools.partial(_conv_seq_kernel, width=width, tl=tl, n_s=n_s, n_x=len(extras),
                             post=post, hist=hist, rc=rc)
    in_specs = ([pl.BlockSpec((1, tl, ct), lambda i, j, l, s=s: (i, l, j + s * nj)) for s in range(n_s)]
                + [pl.BlockSpec((1, past, ct), lambda i, j, l, s=s: (i, 0, j + s * nj)) for s in range(n_s)]
                + [pl.BlockSpec((width, ct), lambda i, j, l, s=s: (0, j + s * nj)) for s in range(n_s)]
                + [pl.BlockSpec((1, ct), lambda i, j, l, s=s: (0, j + s * nj)) for s in range(n_s)]
                + [pl.BlockSpec((1, ct), lambda i, j, l: (0, j)) for _ in extras])
    out_specs = ([pl.BlockSpec((1, tl, ct), lambda i, j, l: (i, l, j))]
                 + [pl.BlockSpec((1, past, ct), lambda i, j, l: (i, 0, j)) for _ in range(n_s)])
    out_shape = ([jax.ShapeDtypeStruct((b, L, out_w), out_dtype)]
                 + [jax.ShapeDtypeStruct((b, past, out_w), F32) for _ in range(n_s)])
    res = pl.pallas_call(
        kern,
        grid=(b, nj, L // tl),
        in_specs=in_specs,
        out_specs=out_specs,
        out_shape=out_shape,
        scratch_shapes=[pltpu.VMEM((hist + tl, ct), F32) for _ in range(n_s)]
                       + [pltpu.VMEM((SUBLANES - 1, hist + tl - SUBLANES, ct), F32) for _ in range(n_s)],
        compiler_params=_params("parallel", "parallel", "arbitrary"),
        name=name,
    )(*([u] * n_s), *([state] * n_s), *([w] * n_s), *([bias] * n_s), *extras)
    return res[0], res[1:]


def _conv_slab_kernel(*refs, width, steps, n_s, n_x, post):
    u_refs = refs[0:n_s]
    st_refs = refs[n_s:2 * n_s]
    w_refs = refs[2 * n_s:3 * n_s]
    b_refs = refs[3 * n_s:4 * n_s]
    x_refs = refs[4 * n_s:4 * n_s + n_x]
    o_ref = refs[4 * n_s + n_x]
    ns_refs = refs[4 * n_s + n_x + 1:]
    past = width - 1

    def slab(s, i):
        return st_refs[s][i] if i < past else u_refs[s][i - past]

    extras = [x[...] for x in x_refs]
    for t in range(steps):
        outs = []
        for s in range(n_s):
            acc = b_refs[s][...] + w_refs[s][0:1, :] * slab(s, t)
            for k in range(1, width):
                acc = acc + w_refs[s][k:k + 1, :] * slab(s, t + k)
            outs.append(acc)
        o_ref[t] = post(outs, extras).astype(o_ref.dtype)
    for s in range(n_s):
        for i in range(past):
            ns_refs[s][i] = slab(s, i + steps)


def _conv_slab(u, state, w, bias, extras, post, *, width, ct, nbt, n_s, out_w, out_dtype, name):
    steps, nb, _ = u.shape
    past = width - 1
    nj = out_w // ct
    kern = functools.partial(_conv_slab_kernel, width=width, steps=steps, n_s=n_s, n_x=len(extras), post=post)
    in_specs = ([pl.BlockSpec((steps, nbt, ct), lambda i, j, s=s: (0, i, j + s * nj)) for s in range(n_s)]
                + [pl.BlockSpec((past, nbt, ct), lambda i, j, s=s: (0, i, j + s * nj)) for s in range(n_s)]
                + [pl.BlockSpec((width, ct), lambda i, j, s=s: (0, j + s * nj)) for s in range(n_s)]
                + [pl.BlockSpec((1, ct), lambda i, j, s=s: (0, j + s * nj)) for s in range(n_s)]
                + [pl.BlockSpec((1, ct), lambda i, j: (0, j)) for _ in extras])
    out_specs = ([pl.BlockSpec((steps, nbt, ct), lambda i, j: (0, i, j))]
                 + [pl.BlockSpec((past, nbt, ct), lambda i, j: (0, i, j)) for _ in range(n_s)])
    out_shape = ([jax.ShapeDtypeStruct((steps, nb, out_w), out_dtype)]
                 + [jax.ShapeDtypeStruct((past, nb, out_w), F32) for _ in range(n_s)])
    res = pl.pallas_call(
        kern,
        grid=(nb // nbt, nj),
        in_specs=in_specs,
        out_specs=out_specs,
        out_shape=out_shape,
        compiler_params=_params("parallel", "parallel"),
        name=name,
    )(*([u] * n_s), *([state] * n_s), *([w] * n_s), *([bias] * n_s), *extras)
    return res[0], res[1:]


def _ssd_kernel(*refs, q, bb, conv):
    xbc_ref, dt_ref, zs_ref, h0_ref, alog_ref, dsk_ref, gn_ref, e_ref = refs[:8]
    if conv:
        stb_ref, cw_ref, cb_ref, y_ref, ht_ref, nsb_ref, h_scr, y_scr, win = refs[8:]
    else:
        y_ref, ht_ref, h_scr, y_scr = refs[8:]
    qk = SSD_KEYS
    c = pl.program_id(1)

    @pl.when(c == 0)
    def _():
        h_scr[...] = h0_ref[...]

    a = -jnp.exp(alog_ref[...])
    e = e_ref[...]
    row = lax.broadcasted_iota(jnp.int32, (qk, qk), 0)
    col = lax.broadcasted_iota(jnp.int32, (qk, qk), 1)
    tril = jnp.where(row >= col, 1.0, 0.0).astype(BF16)
    causal = row[:q, :] >= col[:q, :]
    key_head = lax.broadcasted_iota(jnp.int32, (qk, GROUP_W), 1) // HEAD_DIM

    def expand(vs):
        v = jnp.concatenate(vs, axis=0) if len(vs) > 1 else vs[0]
        hi = v.astype(BF16)
        lo = (v - hi.astype(F32)).astype(BF16)
        out = _dot(hi, e) + _dot(lo, e)
        n = vs[0].shape[0]
        return [out[i * n:(i + 1) * n, :] for i in range(len(vs))]

    def pad_keys(v):
        if q == qk:
            return v
        return jnp.concatenate([v, jnp.zeros((qk - q, v.shape[1]), v.dtype)], axis=0)

    seqs = range(bb)
    xqs = []
    for s in seqs:
        if conv:
            past = CONV_B - 1
            hist = SUBLANES

            @pl.when(c == 0)
            def _(s=s):
                win[s, hist - past:hist, :] = stb_ref[s]

            @pl.when(c > 0)
            def _(s=s):
                win[s, 0:hist, :] = win[s, q:q + hist, :]

            win[s, hist:hist + q, :] = xbc_ref[s]
            pre = cb_ref[...] + cw_ref[0:1, :] * win[s, hist - past:hist - past + q, :]
            for k in range(1, CONV_B):
                pre = pre + cw_ref[k:k + 1, :] * win[s, hist - past + k:hist - past + k + q, :]
            xqs.append(_silu(pre))
            nsb_ref[s] = win[s, hist + q - past:hist + q, :]
        else:
            xqs.append(xbc_ref[s])

    xs = [xq[:, :D_B] for xq in xqs]
    bms = [pad_keys(xq[:, D_B:D_B + N_GROUPS * D_STATE]).astype(BF16) for xq in xqs]
    cms = [xq[:, D_B + N_GROUPS * D_STATE:].astype(BF16) for xq in xqs]
    dts = [dt_ref[s] for s in seqs]
    cums = [sum(_dot(tril, p) for p in _split_bf16(pad_keys(dt * a), 3)) for dt in dts]
    cum_ts = [cum.T for cum in cums]
    cum_qs = [cum[:q, :] for cum in cums]
    xdts = [x * d for x, d in zip(xs, expand(dts))]
    chunk_decay = [jnp.exp(cum[qk - 1:qk, :]) for cum in cums]
    to_end = [jnp.exp(cum[qk - 1:qk, :] - cum_q) for cum, cum_q in zip(cums, cum_qs)]
    wide = expand(to_end + [jnp.exp(cum_q) for cum_q in cum_qs])
    xws = [pad_keys(xdt * w).astype(BF16) for xdt, w in zip(xdts, wide[:bb])]
    xdts = [pad_keys(xdt).astype(BF16) for xdt in xdts]
    ecums = wide[bb:]

    for g in range(N_GROUPS):
        cols = slice(g * GROUP_W, (g + 1) * GROUP_W)
        for s in seqs:
            bg = bms[s][:, g * D_STATE:(g + 1) * D_STATE]
            cg = cms[s][:, g * D_STATE:(g + 1) * D_STATE]
            cb = lax.dot_general(cg, bg, (((1,), (1,)), ((), ())), preferred_element_type=F32)
            xdt_g = xdts[s][:, cols]
            scores, keys = [], []
            for r in range(HEADS_PER_GROUP):
                hd = g * HEADS_PER_GROUP + r
                seg = jnp.where(causal, cum_qs[s][:, hd:hd + 1] - cum_ts[s][hd:hd + 1, :], -1e30)
                scores.append((cb * jnp.exp(seg)).astype(BF16))
                keys.append(jnp.where(key_head == r, xdt_g, jnp.zeros_like(xdt_g)))
            y_g = _dot(jnp.concatenate(scores, axis=1), jnp.concatenate(keys, axis=0))
            h_g = h_scr[s, cols, :]
            y_inter = lax.dot_general(cg, h_g.astype(BF16), (((1,), (1,)), ((), ())),
                                      preferred_element_type=F32)
            y_scr[s, :, cols] = y_g + y_inter * ecums[s][:, cols]
            s_g = lax.dot_general(xws[s][:, cols], bg, (((0,), (0,)), ((), ())), preferred_element_type=F32)
            decay = jnp.concatenate(
                [jnp.broadcast_to(chunk_decay[s][:, g * HEADS_PER_GROUP + r:g * HEADS_PER_GROUP + r + 1],
                                  (HEAD_DIM, D_STATE)) for r in range(HEADS_PER_GROUP)], axis=0)
            h_scr[s, cols, :] = decay * h_g + s_g

    for s in seqs:
        y = y_scr[s] + dsk_ref[...] * xs[s]
        y_ref[s] = _rms(y * zs_ref[s], gn_ref[...]).astype(y_ref.dtype)

    @pl.when(c == pl.num_programs(1) - 1)
    def _():
        ht_ref[...] = h_scr[...]


def _ssd(xbc, dt, zs, h0, alog, dsk, gn, e, conv_b=None, *, q, bb):
    b, L, _ = xbc.shape
    seq = lambda w: pl.BlockSpec((bb, q, w), lambda i, c: (i, c, 0))
    full = lambda arr: pl.BlockSpec(arr.shape, lambda i, c: (0,) * arr.ndim)
    state = pl.BlockSpec((bb, D_B, D_STATE), lambda i, c: (i, 0, 0))
    cstate = pl.BlockSpec((bb, CONV_B - 1, D_XBC), lambda i, c: (i, 0, 0))
    conv = conv_b is not None
    return pl.pallas_call(
        functools.partial(_ssd_kernel, q=q, bb=bb, conv=conv),
        grid=(b // bb, L // q),
        in_specs=[seq(D_XBC), seq(LANES), seq(D_B), state,
                  full(alog), full(dsk), full(gn), full(e)]
                 + ([cstate, full(conv_b[1]), full(conv_b[2])] if conv else []),
        out_specs=[seq(D_B), state] + ([cstate] if conv else []),
        out_shape=[jax.ShapeDtypeStruct((b, L, D_B), BF16),
                   jax.ShapeDtypeStruct((b, D_B, D_STATE), F32)]
                  + ([jax.ShapeDtypeStruct((b, CONV_B - 1, D_XBC), F32)] if conv else []),
        scratch_shapes=[pltpu.VMEM((bb, D_B, D_STATE), F32), pltpu.VMEM((bb, q, D_B), F32)]
                       + ([pltpu.VMEM((bb, SUBLANES + q, D_XBC), F32)] if conv else []),
        compiler_params=_params("parallel", "arbitrary"),
        name="ssd",
    )(xbc, dt, zs, h0, alog, dsk, gn, e, *(conv_b or ()))


def _layer(x, wts, conv_a, ssd_fn, ffn_fn):
    h = _norm(x, wts["g_pre1"])
    w_main = wts["w_main"]
    uglu = _proj(h, [w_main, w_main], [], _epi_glu, 512, F32, "in_glu", n=D_A, cols=[0, D_A])
    dt = _proj(h, [wts["w_dt"]], [wts["dt_bias"]], _epi_dt, LANES, F32, "in_dt")
    zs = _proj(h, [w_main], [], _epi_silu, 512, F32, "in_z", n=D_B, cols=[COL_Z])
    xbc = _proj(h, [w_main], [], _epi_id, 512, F32, "in_xbc", n=D_XBC, cols=[COL_XBC])
    gates = _proj(h, [wts["w_g"]], [wts["b_gate"]], _epi_gate, 512, F32, "in_gates")

    ua, st_a = conv_a(uglu)
    yb, st_b, st_h = ssd_fn(xbc, dt, zs)

    x1 = _merge(ua, yb, gates, x, wts["w_a_out"], wts["b_a_out"], wts["w_b_out"], wts["w_o"], wts["g_post1"])
    x2, st_f = ffn_fn(x1)
    return x2, st_a, st_b, st_h, st_f


def _seq_group(x, wts, st_a, st_b, st_h, st_f, tl, q):
    b, L, _ = x.shape

    def conv_a(uglu):
        ua, (ns,) = _conv_seq(uglu.reshape(b, L, D_A), st_a, wts["w_dw_a"], wts["b_dw_a"],
                              [wts["g_ln_a"], wts["b_ln_a"]], _post_ln_silu, width=CONV_A, ct=D_A,
                              tl=min(tl, 256), n_s=1, out_w=D_A, out_dtype=BF16, name="conv_a")
        return ua.reshape(b * L, D_A), ns

    def ssd_fn(xbc, dt, zs):
        yb, ht, ns = _ssd(xbc.reshape(b, L, D_XBC), dt.reshape(b, L, LANES), zs.reshape(b, L, D_B), st_h,
                          wts["a_log"], wts["d_skip"], wts["g_norm_b"], wts["expand"],
                          (st_b, wts["w_dw_b"], wts["b_dw_b"]), q=q, bb=1)
        return yb.reshape(b * L, D_B), ns, ht

    def ffn_fn(x1):
        x2, ns = _ffn(x1.reshape(b, L, D_MODEL), st_f, wts, tm=min(L, 512), sh=1)
        return x2.reshape(b * L, D_MODEL), ns

    x2, ns_a, ns_b, ns_h, ns_f = _layer(x.reshape(b * L, D_MODEL), wts, conv_a, ssd_fn, ffn_fn)
    return x2.reshape(b, L, D_MODEL), ns_a, ns_b, ns_h, ns_f


def _step_group(x, wts, st_a, st_b, st_h, st_f):
    nb, L, _ = x.shape
    tmaj = lambda s: jnp.transpose(s, (1, 0, 2))
    lpad = -(-L // SUBLANES) * SUBLANES

    def conv_a(uglu):
        ua, (ns,) = _conv_slab(uglu.reshape(L, nb, D_A), tmaj(st_a), wts["w_dw_a"], wts["b_dw_a"],
                               [wts["g_ln_a"], wts["b_ln_a"]], _post_ln_silu, width=CONV_A, ct=D_A,
                               nbt=32, n_s=1, out_w=D_A, out_dtype=BF16, name="conv_a_step")
        return ua.reshape(L * nb, D_A), tmaj(ns)

    def bmaj_pad(v):
        return jnp.pad(tmaj(v), ((0, 0), (0, lpad - L), (0, 0)))

    def ssd_fn(xbc, dt, zs):
        act, (ns,) = _conv_slab(xbc.reshape(L, nb, D_XBC), tmaj(st_b), wts["w_dw_b"], wts["b_dw_b"], [],
                                _post_silu, width=CONV_B, ct=512, nbt=nb, n_s=1, out_w=D_XBC,
                                out_dtype=F32, name="conv_b_step")
        yb, ht = _ssd(bmaj_pad(act), bmaj_pad(dt.reshape(L, nb, LANES)), bmaj_pad(zs.reshape(L, nb, D_B)),
                      st_h, wts["a_log"], wts["d_skip"], wts["g_norm_b"], wts["expand"], q=lpad, bb=4)
        return tmaj(yb[:, :L]).reshape(L * nb, D_B), tmaj(ns), ht

    def ffn_fn(x1):
        past = CONV_F - 1
        x2, ns = _ffn(x1.reshape(1, L * nb, D_MODEL), tmaj(st_f).reshape(1, past * nb, 2 * D_FF), wts,
                      tm=L * nb, sh=nb)
        return x2.reshape(L * nb, D_MODEL), tmaj(ns.reshape(past, nb, 2 * D_FF))

    x2, ns_a, ns_b, ns_h, ns_f = _layer(tmaj(x).reshape(L * nb, D_MODEL), wts, conv_a, ssd_fn, ffn_fn)
    return tmaj(x2.reshape(L, nb, D_MODEL)), ns_a, ns_b, ns_h, ns_f


def _layer_weights(l, g_pre1, g_post1, w_in, b_gate, w_dw_a, b_dw_a, g_ln_a, b_ln_a, w_a_out, b_a_out,
                   w_dw_b, b_dw_b, dt_bias, a_log, d_skip, g_norm_b, w_b_out, w_o,
                   g_pre2, g_post2, w_up, w_dw_f, b_dw_f, w_down):
    row = lambda v: v[l].reshape(1, -1)
    lane_pad = lambda v: jnp.pad(v, ((0, 0), (0, LANES - v.shape[1])))
    w = w_in[l]
    head = jnp.arange(LANES, dtype=jnp.int32)[:, None]
    chan_head = (jnp.arange(D_B, dtype=jnp.int32) // HEAD_DIM)[None, :]
    expand = (head == chan_head).astype(BF16)
    return {
        "g_pre1": row(g_pre1), "g_post1": row(g_post1), "g_pre2": row(g_pre2), "g_post2": row(g_post2),
        "w_main": w[:, :COL_DT].astype(BF16),
        "w_dt": lane_pad(w[:, COL_DT:COL_GATE]).astype(BF16), "w_g": w[:, COL_GATE:].astype(BF16),
        "dt_bias": lane_pad(row(dt_bias)), "b_gate": row(b_gate),
        "w_dw_a": w_dw_a[l], "b_dw_a": row(b_dw_a), "g_ln_a": row(g_ln_a), "b_ln_a": row(b_ln_a),
        "w_a_out": w_a_out[l].astype(BF16), "b_a_out": row(b_a_out),
        "w_dw_b": w_dw_b[l], "b_dw_b": row(b_dw_b),
        "a_log": lane_pad(row(a_log)), "d_skip": jnp.repeat(d_skip[l], HEAD_DIM).reshape(1, D_B),
        "g_norm_b": row(g_norm_b), "expand": expand,
        "w_b_out": w_b_out[l].astype(BF16), "w_o": w_o[l].astype(BF16),
        "w_up": w_up[l].astype(BF16), "w_dw_f": w_dw_f[l], "b_dw_f": row(b_dw_f),
        "w_down": w_down[l].astype(BF16),
    }


def kernel(x_prompt, x_sample, state_conv_a, state_conv_b, state_ssm, state_conv_ffn, meta_tokens, g_pre1, g_post1, w_in, b_gate, w_dw_a, b_dw_a, g_ln_a, b_ln_a, w_a_out, b_a_out, w_dw_b, b_dw_b, dt_bias, a_log, d_skip, g_norm_b, w_b_out, w_o, g_pre2, g_post2, w_up, w_dw_f, b_dw_f, w_down):
    depth = w_in.shape[0]
    bp = x_prompt.shape[0]
    nb = x_sample.shape[0]
    xm = meta_tokens.astype(x_prompt.dtype)[None]
    xp, xs = x_prompt, x_sample
    pa, pb, ph, pf = [], [], [], []
    sa, sb, sh, sf = [], [], [], []
    for l in range(depth):
        wts = _layer_weights(l, g_pre1, g_post1, w_in, b_gate, w_dw_a, b_dw_a, g_ln_a, b_ln_a, w_a_out, b_a_out,
                             w_dw_b, b_dw_b, dt_bias, a_log, d_skip, g_norm_b, w_b_out, w_o,
                             g_pre2, g_post2, w_up, w_dw_f, b_dw_f, w_down)
        xm, m_a, m_b, m_h, m_f = _seq_group(
            xm, wts, jnp.zeros((1, CONV_A - 1, D_A), F32), jnp.zeros((1, CONV_B - 1, D_XBC), F32),
            jnp.zeros((1, D_B, D_STATE), F32), jnp.zeros((1, CONV_F - 1, 2 * D_FF), F32), tl=N_META, q=SUBLANES)
        rep = lambda s: jnp.broadcast_to(s, (bp,) + s.shape[1:])
        xp, c_a, c_b, c_h, c_f = _seq_group(xp, wts, rep(m_a), rep(m_b), rep(m_h), rep(m_f), tl=512, q=SSD_KEYS)
        pa.append(c_a); pb.append(c_b); ph.append(c_h.reshape(bp, N_HEADS, HEAD_DIM, D_STATE)); pf.append(c_f)
        xs, d_a, d_b, d_h, d_f = _step_group(
            xs, wts, state_conv_a[l], state_conv_b[l], state_ssm[l].reshape(nb, D_B, D_STATE), state_conv_ffn[l])
        sa.append(d_a); sb.append(d_b); sh.append(d_h.reshape(nb, N_HEADS, HEAD_DIM, D_STATE)); sf.append(d_f)
    return (xp, xs, jnp.stack(pa), jnp.stack(pb), jnp.stack(ph), jnp.stack(pf),
            jnp.stack(sa), jnp.stack(sb), jnp.stack(sh), jnp.stack(sf))
```

```python
import functools

import jax
import jax.numpy as jnp
from jax import lax
from jax.experimental import pallas as pl
from jax.experimental.pallas import tpu as pltpu

D_MODEL = 2048
N_META = 16
D_A = 1024
CONV_A = 31
D_B = 2048
HEAD_DIM = 64
N_HEADS = D_B // HEAD_DIM
N_GROUPS = 8
HEADS_PER_GROUP = N_HEADS // N_GROUPS
GROUP_W = HEADS_PER_GROUP * HEAD_DIM
D_STATE = 128
CONV_B = 4
D_FF = 5632
CONV_F = 3
EPS = 1e-6
D_XBC = D_B + 2 * N_GROUPS * D_STATE
COL_Z = 2 * D_A
COL_XBC = COL_Z + D_B
COL_DT = COL_XBC + D_XBC
COL_GATE = COL_DT + N_HEADS

LANES = 128
SUBLANES = 8
SSD_KEYS = 128
PROJ_ROWS = 256
VMEM_LIMIT = 56 * 1024 * 1024

F32 = jnp.float32
BF16 = jnp.bfloat16


def _params(*sem):
    return pltpu.CompilerParams(dimension_semantics=sem, vmem_limit_bytes=VMEM_LIMIT)


def _sigmoid(x):
    return 1.0 / (1.0 + jnp.exp(-x))


def _silu(x):
    return x * _sigmoid(x)


def _softplus(x):
    return jnp.maximum(x, 0.0) + jnp.log1p(jnp.exp(-jnp.abs(x)))


def _gelu_tanh(x):
    return 0.5 * x * (1.0 + jnp.tanh(0.7978845608028654 * (x + 0.044715 * (x * x * x))))


def _rms(x, g):
    r = lax.rsqrt(jnp.mean(x * x, axis=-1, keepdims=True) + EPS)
    return x * r * g


def _dot(a, b):
    return jnp.dot(a, b, preferred_element_type=F32)


def _split_bf16(v, parts):
    out = []
    for _ in range(parts):
        p = v.astype(BF16)
        out.append(p)
        v = v - p.astype(F32)
    return out


def _cast_kernel(a_ref, b_ref, o_ref, *, shift):
    if shift:
        o_ref[...] = jnp.concatenate([a_ref[:, shift:], b_ref[:, :shift]], axis=1).astype(o_ref.dtype)
    else:
        o_ref[...] = a_ref[...].astype(o_ref.dtype)


def _cast_cols(w, col0, n):
    d = w.shape[0]
    tn = 512
    blk0, shift = divmod(col0, tn)
    return pl.pallas_call(
        functools.partial(_cast_kernel, shift=shift),
        grid=(n // tn,),
        in_specs=[pl.BlockSpec((d, tn), lambda j: (0, j + blk0)),
                  pl.BlockSpec((d, tn), lambda j: (0, j + blk0 + (1 if shift else 0)))],
        out_specs=pl.BlockSpec((d, tn), lambda j: (0, j)),
        out_shape=jax.ShapeDtypeStruct((d, n), BF16),
        compiler_params=_params("parallel"),
        name="cast_cols",
    )(w, w)


def _norm_kernel(x_ref, g_ref, o_ref):
    o_ref[...] = _rms(x_ref[...], g_ref[...]).astype(o_ref.dtype)


def _norm(x, g):
    rows, d = x.shape
    tm = min(rows, 512)
    return pl.pallas_call(
        _norm_kernel,
        grid=(rows // tm,),
        in_specs=[pl.BlockSpec((tm, d), lambda i: (i, 0)), pl.BlockSpec((1, d), lambda i: (0, 0))],
        out_specs=pl.BlockSpec((tm, d), lambda i: (i, 0)),
        out_shape=jax.ShapeDtypeStruct((rows, d), BF16),
        compiler_params=_params("parallel"),
        name="norm",
    )(x, g)


def _proj_kernel(*refs, n_w, n_b, epilogue):
    h_ref = refs[0]
    w_refs = refs[1:1 + n_w]
    b_refs = refs[1 + n_w:1 + n_w + n_b]
    o_ref = refs[-1]
    bs = [b[...] for b in b_refs]
    rows = h_ref.shape[0]
    mc = min(rows, PROJ_ROWS)
    for r0 in range(0, rows, mc):
        h = h_ref[r0:r0 + mc, :]
        accs = [_dot(h, w[...]) for w in w_refs]
        o_ref[r0:r0 + mc, :] = epilogue(accs, bs).astype(o_ref.dtype)


def _proj(h, ws, bs, epilogue, tn, out_dtype, name, n=None, cols=None):
    rows, d = h.shape
    n = n or ws[0].shape[1]
    cols = cols or [0] * len(ws)
    tm = min(rows, 1024)
    kern = functools.partial(_proj_kernel, n_w=len(ws), n_b=len(bs), epilogue=epilogue)
    return pl.pallas_call(
        kern,
        grid=(rows // tm, n // tn),
        in_specs=[pl.BlockSpec((tm, d), lambda i, j: (i, 0))]
                 + [pl.BlockSpec((d, tn), lambda i, j, c=c // tn: (0, j + c)) for c in cols]
                 + [pl.BlockSpec((1, tn), lambda i, j: (0, j)) for _ in bs],
        out_specs=pl.BlockSpec((tm, tn), lambda i, j: (i, j)),
        out_shape=jax.ShapeDtypeStruct((rows, n), out_dtype),
        compiler_params=_params("parallel", "parallel"),
        name=name,
    )(h, *ws, *bs)


def _epi_glu(a, b):
    return a[0] * _sigmoid(a[1])


def _epi_dt(a, b):
    lane = lax.broadcasted_iota(jnp.int32, a[0].shape, 1)
    return jnp.where(lane < N_HEADS, _softplus(a[0] + b[0]), 0.0)


def _epi_silu(a, b):
    return _silu(a[0])


def _epi_id(a, b):
    return a[0]


def _epi_gate(a, b):
    return _sigmoid(a[0] + b[0])


def _merge_kernel(ua_ref, yb_ref, gt_ref, x_ref, wa_ref, ba_ref, wb_ref, wo_ref, gp_ref, o_ref):
    ya = _dot(ua_ref[...], wa_ref[...]) + ba_ref[...]
    yb = _dot(yb_ref[...], wb_ref[...])
    mix = gt_ref[:, :D_MODEL] * ya + gt_ref[:, D_MODEL:] * yb
    m = _dot(mix.astype(BF16), wo_ref[...])
    o_ref[...] = x_ref[...] + _rms(m, gp_ref[...])


def _merge(ua, yb, gates, x, wa, ba, wb, wo, gp):
    rows = x.shape[0]
    tm = min(rows, 256)
    row_spec = lambda w: pl.BlockSpec((tm, w), lambda i: (i, 0))
    res_spec = lambda a: pl.BlockSpec(a.shape, lambda i: (0, 0), pipeline_mode=pl.Buffered(1))
    return pl.pallas_call(
        _merge_kernel,
        grid=(rows // tm,),
        in_specs=[row_spec(D_A), row_spec(D_B), row_spec(2 * D_MODEL), row_spec(D_MODEL),
                  res_spec(wa), res_spec(ba), res_spec(wb), res_spec(wo), res_spec(gp)],
        out_specs=row_spec(D_MODEL),
        out_shape=jax.ShapeDtypeStruct((rows, D_MODEL), F32),
        compiler_params=_params("parallel"),
        name="merge",
    )(ua, yb, gates, x, wa, ba, wb, wo, gp)


def _ffn_kernel(x1_ref, gpre_ref, wg_ref, wv_ref, wd_ref, stg_ref, stv_ref, cwg_ref, cwv_ref, cbg_ref, cbv_ref,
                gpost_ref, o_ref, nsg_ref, nsv_ref, h_scr, acc_scr, act_scr, win_g, win_v, *carry,
                tm, sh, hist, rc):
    l = pl.program_id(1)
    j = pl.program_id(2)
    past = (CONV_F - 1) * sh

    @pl.when(j == 0)
    def _():
        h_scr[...] = _rms(x1_ref[0], gpre_ref[...]).astype(BF16)
        acc_scr[...] = jnp.zeros_like(acc_scr)

    for s, (st_ref, win) in enumerate(((stg_ref, win_g), (stv_ref, win_v))):
        if carry:
            @pl.when(l == 0)
            def _(win=win, st_ref=st_ref):
                win[hist - past:hist, :] = st_ref[0]

            @pl.when(l > 0)
            def _(win=win, s=s):
                win[0:hist, :] = carry[0][j, s]
        else:
            win[hist - past:hist, :] = st_ref[0]

    def conv(win, cw_ref, cb_ref, r0, cs):
        out = cb_ref[:, cs]
        for k in range(CONV_F):
            lo = hist - (CONV_F - 1 - k) * sh + r0
            out = out + cw_ref[k:k + 1, cs] * win[lo:lo + rc, cs]
        return out

    tf = win_g.shape[1]
    cw = min(tf, 256)
    mc = min(tm, PROJ_ROWS)
    for c0 in range(0, tf, cw):
        cs = slice(c0, c0 + cw)
        for m0 in range(0, tm, mc):
            win_g[hist + m0:hist + m0 + mc, cs] = _dot(h_scr[m0:m0 + mc, :], wg_ref[:, cs])
            win_v[hist + m0:hist + m0 + mc, cs] = _dot(h_scr[m0:m0 + mc, :], wv_ref[:, cs])
            for r0 in range(m0, m0 + mc, rc):
                gate = _gelu_tanh(conv(win_g, cwg_ref, cbg_ref, r0, cs))
                act_scr[r0:r0 + rc, cs] = (gate * conv(win_v, cwv_ref, cbv_ref, r0, cs)).astype(BF16)
    acc_scr[...] += _dot(act_scr[...], wd_ref[...])

    for s, (ns_ref, win) in enumerate(((nsg_ref, win_g), (nsv_ref, win_v))):
        ns_ref[0, 0] = win[hist + tm - past:hist + tm, :]
        if carry:
            carry[0][j, s] = win[tm:tm + hist, :]

    @pl.when(j == pl.num_programs(2) - 1)
    def _():
        o_ref[0] = x1_ref[0] + _rms(acc_scr[...], gpost_ref[...])


def _ffn(x1, state, wts, *, tm, sh):
    b, L, _ = x1.shape
    tf = 512
    nj = D_FF // tf
    past = (CONV_F - 1) * sh
    hist = -(-past // SUBLANES) * SUBLANES
    n_l = L // tm
    carry = [pltpu.VMEM((nj, 2, hist, tf), F32)] if n_l > 1 else []
    col = lambda rows, off: pl.BlockSpec((rows, tf), lambda i, l, j: (0, j + off))
    st = lambda off: pl.BlockSpec((1, past, tf), lambda i, l, j: (i, 0, j + off))
    tail = pl.BlockSpec((1, 1, past, tf), lambda i, l, j: (i, l, 0, j))
    vec = pl.BlockSpec((1, D_MODEL), lambda i, l, j: (0, 0))
    xblk = pl.BlockSpec((1, tm, D_MODEL), lambda i, l, j: (i, l, 0))
    out, ns_g, ns_v = pl.pallas_call(
        functools.partial(_ffn_kernel, tm=tm, sh=sh, hist=hist, rc=min(tm, 32)),
        grid=(b, n_l, nj),
        in_specs=[xblk, vec, col(D_MODEL, 0), col(D_MODEL, nj),
                  pl.BlockSpec((tf, D_MODEL), lambda i, l, j: (j, 0)),
                  st(0), st(nj), col(CONV_F, 0), col(CONV_F, nj), col(1, 0), col(1, nj), vec],
        out_specs=[xblk, tail, tail],
        out_shape=[jax.ShapeDtypeStruct((b, L, D_MODEL), F32),
                   jax.ShapeDtypeStruct((b, n_l, past, D_FF), F32),
                   jax.ShapeDtypeStruct((b, n_l, past, D_FF), F32)],
        scratch_shapes=[pltpu.VMEM((tm, D_MODEL), BF16), pltpu.VMEM((tm, D_MODEL), F32),
                        pltpu.VMEM((tm, tf), BF16),
                        pltpu.VMEM((hist + tm, tf), F32), pltpu.VMEM((hist + tm, tf), F32)] + carry,
        compiler_params=_params("parallel", "arbitrary", "arbitrary"),
        name="ffn",
    )(x1, wts["g_pre2"], wts["w_up"], wts["w_up"], wts["w_down"], state, state,
      wts["w_dw_f"], wts["w_dw_f"], wts["b_dw_f"], wts["b_dw_f"], wts["g_post2"])
    return out, jnp.concatenate([ns_g[:, -1], ns_v[:, -1]], axis=-1)


def _post_ln_silu(outs, extras):
    u = outs[0]
    mu = jnp.mean(u, axis=-1, keepdims=True)
    xc = u - mu
    r = lax.rsqrt(jnp.mean(xc * xc, axis=-1, keepdims=True) + EPS)
    return _silu(xc * r * extras[0] + extras[1])


def _post_silu(outs, extras):
    return _silu(outs[0])


def _conv_seq_kernel(*refs, width, tl, n_s, n_x, post, hist, rc):
    u_refs = refs[0:n_s]
    st_refs = refs[n_s:2 * n_s]
    w_refs = refs[2 * n_s:3 * n_s]
    b_refs = refs[3 * n_s:4 * n_s]
    x_refs = refs[4 * n_s:4 * n_s + n_x]
    o_ref = refs[4 * n_s + n_x]
    ns_refs = refs[4 * n_s + n_x + 1:4 * n_s + n_x + 1 + n_s]
    win_refs = refs[4 * n_s + n_x + 1 + n_s:4 * n_s + n_x + 1 + 2 * n_s]
    shf_refs = refs[4 * n_s + n_x + 1 + 2 * n_s:]
    past = width - 1
    l = pl.program_id(2)
    n_shf = hist + tl - SUBLANES

    for s in range(n_s):
        win = win_refs[s]

        @pl.when(l == 0)
        def _(win=win, s=s):
            if hist > past:
                win[0:hist - past, :] = jnp.zeros((hist - past, win.shape[1]), F32)
            win[hist - past:hist, :] = st_refs[s][0]

        @pl.when(l > 0)
        def _(win=win):
            win[0:hist, :] = win[tl:tl + hist, :]

        win[hist:hist + tl, :] = u_refs[s][0]
        for r in range(1, SUBLANES):
            shf_refs[s][r - 1, :, :] = win[r:r + n_shf, :]

    def tap(s, lo):
        a, r = divmod(lo, SUBLANES)
        if r == 0:
            return win_refs[s][lo:lo + rc, :]
        return shf_refs[s][r - 1, a * SUBLANES:a * SUBLANES + rc, :]

    extras = [x[...] for x in x_refs]
    for c in range(tl // rc):
        outs = []
        for s in range(n_s):
            acc = b_refs[s][...] + w_refs[s][0:1, :] * tap(s, c * rc + hist - past)
            for k in range(1, width):
                acc = acc + w_refs[s][k:k + 1, :] * tap(s, c * rc + hist - past + k)
            outs.append(acc)
        o_ref[0, c * rc:(c + 1) * rc, :] = post(outs, extras).astype(o_ref.dtype)

    @pl.when(l == pl.num_programs(2) - 1)
    def _():
        for s in range(n_s):
            ns_refs[s][0] = win_refs[s][hist + tl - past:hist + tl, :]


def _conv_seq(u, state, w, bias, extras, post, *, width, ct, tl, n_s, out_w, out_dtype, name):
    b, L, _ = u.shape
    past = width - 1
    hist = -(-past // SUBLANES) * SUBLANES
    nj = out_w // ct
    rc = min(tl, 32)
    kern = functools.partial(_conv_seq_kernel, width=width, tl=tl, n_s=n_s, n_x=len(extras),
                             post=post, hist=hist, rc=rc)
    in_specs = ([pl.BlockSpec((1, tl, ct), lambda i, j, l, s=s: (i, l, j + s * nj)) for s in range(n_s)]
                + [pl.BlockSpec((1, past, ct), lambda i, j, l, s=s: (i, 0, j + s * nj)) for s in range(n_s)]
                + [pl.BlockSpec((width, ct), lambda i, j, l, s=s: (0, j + s * nj)) for s in range(n_s)]
                + [pl.BlockSpec((1, ct), lambda i, j, l, s=s: (0, j + s * nj)) for s in range(n_s)]
                + [pl.BlockSpec((1, ct), lambda i, j, l: (0, j)) for _ in extras])
    out_specs = ([pl.BlockSpec((1, tl, ct), lambda i, j, l: (i, l, j))]
                 + [pl.BlockSpec((1, past, ct), lambda i, j, l: (i, 0, j)) for _ in range(n_s)])
    out_shape = ([jax.ShapeDtypeStruct((b, L, out_w), out_dtype)]
                 + [jax.ShapeDtypeStruct((b, past, out_w), F32) for _ in range(n_s)])
    res = pl.pallas_call(
        kern,
        grid=(b, nj, L // tl),
        in_specs=in_specs,
        out_specs=out_specs,
        out_shape=out_shape,
        scratch_shapes=[pltpu.VMEM((hist + tl, ct), F32) for _ in range(n_s)]
                       + [pltpu.VMEM((SUBLANES - 1, hist + tl - SUBLANES, ct), F32) for _ in range(n_s)],
        compiler_params=_params("parallel", "parallel", "arbitrary"),
        name=name,
    )(*([u] * n_s), *([state] * n_s), *([w] * n_s), *([bias] * n_s), *extras)
    return res[0], res[1:]


def _conv_slab_kernel(*refs, width, steps, n_s, n_x, post):
    u_refs = refs[0:n_s]
    st_refs = refs[n_s:2 * n_s]
    w_refs = refs[2 * n_s:3 * n_s]
    b_refs = refs[3 * n_s:4 * n_s]
    x_refs = refs[4 * n_s:4 * n_s + n_x]
    o_ref = refs[4 * n_s + n_x]
    ns_refs = refs[4 * n_s + n_x + 1:]
    past = width - 1

    def slab(s, i):
        return st_refs[s][i] if i < past else u_refs[s][i - past]

    extras = [x[...] for x in x_refs]
    for t in range(steps):
        outs = []
        for s in range(n_s):
            acc = b_refs[s][...] + w_refs[s][0:1, :] * slab(s, t)
            for k in range(1, width):
                acc = acc + w_refs[s][k:k + 1, :] * slab(s, t + k)
            outs.append(acc)
        o_ref[t] = post(outs, extras).astype(o_ref.dtype)
    for s in range(n_s):
        for i in range(past):
            ns_refs[s][i] = slab(s, i + steps)


def _conv_slab(u, state, w, bias, extras, post, *, width, ct, nbt, n_s, out_w, out_dtype, name):
    steps, nb, _ = u.shape
    past = width - 1
    nj = out_w // ct
    kern = functools.partial(_conv_slab_kernel, width=width, steps=steps, n_s=n_s, n_x=len(extras), post=post)
    in_specs = ([pl.BlockSpec((steps, nbt, ct), lambda i, j, s=s: (0, i, j + s * nj)) for s in range(n_s)]
                + [pl.BlockSpec((past, nbt, ct), lambda i, j, s=s: (0, i, j + s * nj)) for s in range(n_s)]
                + [pl.BlockSpec((width, ct), lambda i, j, s=s: (0, j + s * nj)) for s in range(n_s)]
                + [pl.BlockSpec((1, ct), lambda i, j, s=s: (0, j + s * nj)) for s in range(n_s)]
                + [pl.BlockSpec((1, ct), lambda i, j: (0, j)) for _ in extras])
    out_specs = ([pl.BlockSpec((steps, nbt, ct), lambda i, j: (0, i, j))]
                 + [pl.BlockSpec((past, nbt, ct), lambda i, j: (0, i, j)) for _ in range(n_s)])
    out_shape = ([jax.ShapeDtypeStruct((steps, nb, out_w), out_dtype)]
                 + [jax.ShapeDtypeStruct((past, nb, out_w), F32) for _ in range(n_s)])
    res = pl.pallas_call(
        kern,
        grid=(nb // nbt, nj),
        in_specs=in_specs,
        out_specs=out_specs,
        out_shape=out_shape,
        compiler_params=_params("parallel", "parallel"),
        name=name,
    )(*([u] * n_s), *([state] * n_s), *([w] * n_s), *([bias] * n_s), *extras)
    return res[0], res[1:]


def _ssd_kernel(*refs, q, bb, conv):
    xbc_ref, dt_ref, zs_ref, h0_ref, alog_ref, dsk_ref, gn_ref, e_ref = refs[:8]
    if conv:
        stb_ref, cw_ref, cb_ref, y_ref, ht_ref, nsb_ref, h_scr, y_scr, win = refs[8:]
    else:
        y_ref, ht_ref, h_scr, y_scr = refs[8:]
    qk = SSD_KEYS
    c = pl.program_id(1)

    @pl.when(c == 0)
    def _():
        h_scr[...] = h0_ref[...]

    a = -jnp.exp(alog_ref[...])
    e = e_ref[...]
    row = lax.broadcasted_iota(jnp.int32, (qk, qk), 0)
    col = lax.broadcasted_iota(jnp.int32, (qk, qk), 1)
    tril = jnp.where(row >= col, 1.0, 0.0).astype(BF16)
    causal = row[:q, :] >= col[:q, :]
    key_head = lax.broadcasted_iota(jnp.int32, (qk, GROUP_W), 1) // HEAD_DIM

    def expand(vs):
        v = jnp.concatenate(vs, axis=0) if len(vs) > 1 else vs[0]
        hi = v.astype(BF16)
        lo = (v - hi.astype(F32)).astype(BF16)
        out = _dot(hi, e) + _dot(lo, e)
        n = vs[0].shape[0]
        return [out[i * n:(i + 1) * n, :] for i in range(len(vs))]

    def pad_keys(v):
        if q == qk:
            return v
        return jnp.concatenate([v, jnp.zeros((qk - q, v.shape[1]), v.dtype)], axis=0)

    seqs = range(bb)
    xqs = []
    for s in seqs:
        if conv:
            past = CONV_B - 1
            hist = SUBLANES

            @pl.when(c == 0)
            def _(s=s):
                win[s, hist - past:hist, :] = stb_ref[s]

            @pl.when(c > 0)
            def _(s=s):
                win[s, 0:hist, :] = win[s, q:q + hist, :]

            win[s, hist:hist + q, :] = xbc_ref[s]
            pre = cb_ref[...] + cw_ref[0:1, :] * win[s, hist - past:hist - past + q, :]
            for k in range(1, CONV_B):
                pre = pre + cw_ref[k:k + 1, :] * win[s, hist - past + k:hist - past + k + q, :]
            xqs.append(_silu(pre))
            nsb_ref[s] = win[s, hist + q - past:hist + q, :]
        else:
            xqs.append(xbc_ref[s])

    xs = [xq[:, :D_B] for xq in xqs]
    bms = [pad_keys(xq[:, D_B:D_B + N_GROUPS * D_STATE]).astype(BF16) for xq in xqs]
    cms = [xq[:, D_B + N_GROUPS * D_STATE:].astype(BF16) for xq in xqs]
    dts = [dt_ref[s] for s in seqs]
    cums = [sum(_dot(tril, p) for p in _split_bf16(pad_keys(dt * a), 3)) for dt in dts]
    cum_ts = [cum.T for cum in cums]
    cum_qs = [cum[:q, :] for cum in cums]
    xdts = [x * d for x, d in zip(xs, expand(dts))]
    chunk_decay = [jnp.exp(cum[qk - 1:qk, :]) for cum in cums]
    to_end = [jnp.exp(cum[qk - 1:qk, :] - cum_q) for cum, cum_q in zip(cums, cum_qs)]
    wide = expand(to_end + [jnp.exp(cum_q) for cum_q in cum_qs])
    xws = [pad_keys(xdt * w).astype(BF16) for xdt, w in zip(xdts, wide[:bb])]
    xdts = [pad_keys(xdt).astype(BF16) for xdt in xdts]
    ecums = wide[bb:]

    for g in range(N_GROUPS):
        cols = slice(g * GROUP_W, (g + 1) * GROUP_W)
        for s in seqs:
            bg = bms[s][:, g * D_STATE:(g + 1) * D_STATE]
            cg = cms[s][:, g * D_STATE:(g + 1) * D_STATE]
            cb = lax.dot_general(cg, bg, (((1,), (1,)), ((), ())), preferred_element_type=F32)
            xdt_g = xdts[s][:, cols]
            scores, keys = [], []
            for r in range(HEADS_PER_GROUP):
                hd = g * HEADS_PER_GROUP + r
                seg = jnp.where(causal, cum_qs[s][:, hd:hd + 1] - cum_ts[s][hd:hd + 1, :], -1e30)
                scores.append((cb * jnp.exp(seg)).astype(BF16))
                keys.append(jnp.where(key_head == r, xdt_g, jnp.zeros_like(xdt_g)))
            y_g = _dot(jnp.concatenate(scores, axis=1), jnp.concatenate(keys, axis=0))
            h_g = h_scr[s, cols, :]
            y_inter = lax.dot_general(cg, h_g.astype(BF16), (((1,), (1,)), ((), ())),
                                      preferred_element_type=F32)
            y_scr[s, :, cols] = y_g + y_inter * ecums[s][:, cols]
            s_g = lax.dot_general(xws[s][:, cols], bg, (((0,), (0,)), ((), ())), preferred_element_type=F32)
            decay = jnp.concatenate(
                [jnp.broadcast_to(chunk_decay[s][:, g * HEADS_PER_GROUP + r:g * HEADS_PER_GROUP + r + 1],
                                  (HEAD_DIM, D_STATE)) for r in range(HEADS_PER_GROUP)], axis=0)
            h_scr[s, cols, :] = decay * h_g + s_g

    for s in seqs:
        y = y_scr[s] + dsk_ref[...] * xs[s]
        y_ref[s] = _rms(y * zs_ref[s], gn_ref[...]).astype(y_ref.dtype)

    @pl.when(c == pl.num_programs(1) - 1)
    def _():
        ht_ref[...] = h_scr[...]


def _ssd(xbc, dt, zs, h0, alog, dsk, gn, e, conv_b=None, *, q, bb):
    b, L, _ = xbc.shape
    seq = lambda w: pl.BlockSpec((bb, q, w), lambda i, c: (i, c, 0))
    full = lambda arr: pl.BlockSpec(arr.shape, lambda i, c: (0,) * arr.ndim)
    state = pl.BlockSpec((bb, D_B, D_STATE), lambda i, c: (i, 0, 0))
    cstate = pl.BlockSpec((bb, CONV_B - 1, D_XBC), lambda i, c: (i, 0, 0))
    conv = conv_b is not None
    return pl.pallas_call(
        functools.partial(_ssd_kernel, q=q, bb=bb, conv=conv),
        grid=(b // bb, L // q),
        in_specs=[seq(D_XBC), seq(LANES), seq(D_B), state,
                  full(alog), full(dsk), full(gn), full(e)]
                 + ([cstate, full(conv_b[1]), full(conv_b[2])] if conv else []),
        out_specs=[seq(D_B), state] + ([cstate] if conv else []),
        out_shape=[jax.ShapeDtypeStruct((b, L, D_B), BF16),
                   jax.ShapeDtypeStruct((b, D_B, D_STATE), F32)]
                  + ([jax.ShapeDtypeStruct((b, CONV_B - 1, D_XBC), F32)] if conv else []),
        scratch_shapes=[pltpu.VMEM((bb, D_B, D_STATE), F32), pltpu.VMEM((bb, q, D_B), F32)]
                       + ([pltpu.VMEM((bb, SUBLANES + q, D_XBC), F32)] if conv else []),
        compiler_params=_params("parallel", "arbitrary"),
        name="ssd",
    )(xbc, dt, zs, h0, alog, dsk, gn, e, *(conv_b or ()))


def _layer(x, wts, conv_a, ssd_fn, ffn_fn):
    h = _norm(x, wts["g_pre1"])
    w_main = wts["w_main"]
    uglu = _proj(h, [w_main, w_main], [], _epi_glu, 512, F32, "in_glu", n=D_A, cols=[0, D_A])
    dt = _proj(h, [wts["w_dt"]], [wts["dt_bias"]], _epi_dt, LANES, F32, "in_dt")
    zs = _proj(h, [w_main], [], _epi_silu, 512, F32, "in_z", n=D_B, cols=[COL_Z])
    xbc = _proj(h, [w_main], [], _epi_id, 512, F32, "in_xbc", n=D_XBC, cols=[COL_XBC])
    gates = _proj(h, [wts["w_g"]], [wts["b_gate"]], _epi_gate, 512, F32, "in_gates")

    ua, st_a = conv_a(uglu)
    yb, st_b, st_h = ssd_fn(xbc, dt, zs)

    x1 = _merge(ua, yb, gates, x, wts["w_a_out"], wts["b_a_out"], wts["w_b_out"], wts["w_o"], wts["g_post1"])
    x2, st_f = ffn_fn(x1)
    return x2, st_a, st_b, st_h, st_f


def _seq_group(x, wts, st_a, st_b, st_h, st_f, tl, q):
    b, L, _ = x.shape

    def conv_a(uglu):
        ua, (ns,) = _conv_seq(uglu.reshape(b, L, D_A), st_a, wts["w_dw_a"], wts["b_dw_a"],
                              [wts["g_ln_a"], wts["b_ln_a"]], _post_ln_silu, width=CONV_A, ct=D_A,
                              tl=min(tl, 256), n_s=1, out_w=D_A, out_dtype=BF16, name="conv_a")
        return ua.reshape(b * L, D_A), ns

    def ssd_fn(xbc, dt, zs):
        yb, ht, ns = _ssd(xbc.reshape(b, L, D_XBC), dt.reshape(b, L, LANES), zs.reshape(b, L, D_B), st_h,
                          wts["a_log"], wts["d_skip"], wts["g_norm_b"], wts["expand"],
                          (st_b, wts["w_dw_b"], wts["b_dw_b"]), q=q, bb=1)
        return yb.reshape(b * L, D_B), ns, ht

    def ffn_fn(x1):
        x2, ns = _ffn(x1.reshape(b, L, D_MODEL), st_f, wts, tm=min(L, 512), sh=1)
        return x2.reshape(b * L, D_MODEL), ns

    x2, ns_a, ns_b, ns_h, ns_f = _layer(x.reshape(b * L, D_MODEL), wts, conv_a, ssd_fn, ffn_fn)
    return x2.reshape(b, L, D_MODEL), ns_a, ns_b, ns_h, ns_f


def _step_group(x, wts, st_a, st_b, st_h, st_f):
    nb, L, _ = x.shape
    tmaj = lambda s: jnp.transpose(s, (1, 0, 2))
    lpad = -(-L // SUBLANES) * SUBLANES

    def conv_a(uglu):
        ua, (ns,) = _conv_slab(uglu.reshape(L, nb, D_A), tmaj(st_a), wts["w_dw_a"], wts["b_dw_a"],
                               [wts["g_ln_a"], wts["b_ln_a"]], _post_ln_silu, width=CONV_A, ct=D_A,
                               nbt=32, n_s=1, out_w=D_A, out_dtype=BF16, name="conv_a_step")
        return ua.reshape(L * nb, D_A), tmaj(ns)

    def bmaj_pad(v):
        return jnp.pad(tmaj(v), ((0, 0), (0, lpad - L), (0, 0)))

    def ssd_fn(xbc, dt, zs):
        act, (ns,) = _conv_slab(xbc.reshape(L, nb, D_XBC), tmaj(st_b), wts["w_dw_b"], wts["b_dw_b"], [],
                                _post_silu, width=CONV_B, ct=512, nbt=nb, n_s=1, out_w=D_XBC,
                                out_dtype=F32, name="conv_b_step")
        yb, ht = _ssd(bmaj_pad(act), bmaj_pad(dt.reshape(L, nb, LANES)), bmaj_pad(zs.reshape(L, nb, D_B)),
                      st_h, wts["a_log"], wts["d_skip"], wts["g_norm_b"], wts["expand"], q=lpad, bb=4)
        return tmaj(yb[:, :L]).reshape(L * nb, D_B), tmaj(ns), ht

    def ffn_fn(x1):
        past = CONV_F - 1
        x2, ns = _ffn(x1.reshape(1, L * nb, D_MODEL), tmaj(st_f).reshape(1, past * nb, 2 * D_FF), wts,
                      tm=L * nb, sh=nb)
        return x2.reshape(L * nb, D_MODEL), tmaj(ns.reshape(past, nb, 2 * D_FF))

    x2, ns_a, ns_b, ns_h, ns_f = _layer(tmaj(x).reshape(L * nb, D_MODEL), wts, conv_a, ssd_fn, ffn_fn)
    return tmaj(x2.reshape(L, nb, D_MODEL)), ns_a, ns_b, ns_h, ns_f


def _layer_weights(l, g_pre1, g_post1, w_in, b_gate, w_dw_a, b_dw_a, g_ln_a, b_ln_a, w_a_out, b_a_out,
                   w_dw_b, b_dw_b, dt_bias, a_log, d_skip, g_norm_b, w_b_out, w_o,
                   g_pre2, g_post2, w_up, w_dw_f, b_dw_f, w_down):
    row = lambda v: v[l].reshape(1, -1)
    lane_pad = lambda v: jnp.pad(v, ((0, 0), (0, LANES - v.shape[1])))
    w = w_in[l]
    head = jnp.arange(LANES, dtype=jnp.int32)[:, None]
    chan_head = (jnp.arange(D_B, dtype=jnp.int32) // HEAD_DIM)[None, :]
    expand = (head == chan_head).astype(BF16)
    return {
        "g_pre1": row(g_pre1), "g_post1": row(g_post1), "g_pre2": row(g_pre2), "g_post2": row(g_post2),
        "w_main": _cast_cols(w, 0, COL_DT),
        "w_dt": lane_pad(w[:, COL_DT:COL_GATE]).astype(BF16), "w_g": _cast_cols(w, COL_GATE, 2 * D_MODEL),
        "dt_bias": lane_pad(row(dt_bias)), "b_gate": row(b_gate),
        "w_dw_a": w_dw_a[l], "b_dw_a": row(b_dw_a), "g_ln_a": row(g_ln_a), "b_ln_a": row(b_ln_a),
        "w_a_out": w_a_out[l].astype(BF16), "b_a_out": row(b_a_out),
        "w_dw_b": w_dw_b[l], "b_dw_b": row(b_dw_b),
        "a_log": lane_pad(row(a_log)), "d_skip": jnp.repeat(d_skip[l], HEAD_DIM).reshape(1, D_B),
        "g_norm_b": row(g_norm_b), "expand": expand,
        "w_b_out": w_b_out[l].astype(BF16), "w_o": w_o[l].astype(BF16),
        "w_up": w_up[l].astype(BF16), "w_dw_f": w_dw_f[l], "b_dw_f": row(b_dw_f),
        "w_down": w_down[l].astype(BF16),
    }


def kernel(x_prompt, x_sample, state_conv_a, state_conv_b, state_ssm, state_conv_ffn, meta_tokens, g_pre1, g_post1, w_in, b_gate, w_dw_a, b_dw_a, g_ln_a, b_ln_a, w_a_out, b_a_out, w_dw_b, b_dw_b, dt_bias, a_log, d_skip, g_norm_b, w_b_out, w_o, g_pre2, g_post2, w_up, w_dw_f, b_dw_f, w_down):
    depth = w_in.shape[0]
    bp = x_prompt.shape[0]
    nb = x_sample.shape[0]
    xm = meta_tokens.astype(x_prompt.dtype)[None]
    xp, xs = x_prompt, x_sample
    pa, pb, ph, pf = [], [], [], []
    sa, sb, sh, sf = [], [], [], []
    for l in range(depth):
        wts = _layer_weights(l, g_pre1, g_post1, w_in, b_gate, w_dw_a, b_dw_a, g_ln_a, b_ln_a, w_a_out, b_a_out,
                             w_dw_b, b_dw_b, dt_bias, a_log, d_skip, g_norm_b, w_b_out, w_o,
                             g_pre2, g_post2, w_up, w_dw_f, b_dw_f, w_down)
        xm, m_a, m_b, m_h, m_f = _seq_group(
            xm, wts, jnp.zeros((1, CONV_A - 1, D_A), F32), jnp.zeros((1, CONV_B - 1, D_XBC), F32),
            jnp.zeros((1, D_B, D_STATE), F32), jnp.zeros((1, CONV_F - 1, 2 * D_FF), F32), tl=N_META, q=SUBLANES)
        rep = lambda s: jnp.broadcast_to(s, (bp,) + s.shape[1:])
        xp, c_a, c_b, c_h, c_f = _seq_group(xp, wts, rep(m_a), rep(m_b), rep(m_h), rep(m_f), tl=512, q=SSD_KEYS)
        pa.append(c_a); pb.append(c_b); ph.append(c_h.reshape(bp, N_HEADS, HEAD_DIM, D_STATE)); pf.append(c_f)
        xs, d_a, d_b, d_h, d_f = _step_group(
            xs, wts, state_conv_a[l], state_conv_b[l], state_ssm[l].reshape(nb, D_B, D_STATE), state_conv_ffn[l])
        sa.append(d_a); sb.append(d_b); sh.append(d_h.reshape(nb, N_HEADS, HEAD_DIM, D_STATE)); sf.append(d_f)
    return (xp, xs, jnp.stack(pa), jnp.stack(pb), jnp.stack(ph), jnp.stack(pf),
            jnp.stack(sa), jnp.stack(sb), jnp.stack(sh), jnp.stack(sf))
```

```python
import functools

import jax
import jax.numpy as jnp
from jax import lax
from jax.experimental import pallas as pl
from jax.experimental.pallas import tpu as pltpu

D_MODEL = 2048
N_META = 16
D_A = 1024
CONV_A = 31
D_B = 2048
HEAD_DIM = 64
N_HEADS = D_B // HEAD_DIM
N_GROUPS = 8
HEADS_PER_GROUP = N_HEADS // N_GROUPS
GROUP_W = HEADS_PER_GROUP * HEAD_DIM
D_STATE = 128
CONV_B = 4
D_FF = 5632
CONV_F = 3
EPS = 1e-6
D_XBC = D_B + 2 * N_GROUPS * D_STATE
COL_Z = 2 * D_A
COL_XBC = COL_Z + D_B
COL_DT = COL_XBC + D_XBC
COL_GATE = COL_DT + N_HEADS

LANES = 128
SUBLANES = 8
SSD_KEYS = 128
PROJ_ROWS = 256
VMEM_LIMIT = 56 * 1024 * 1024

F32 = jnp.float32
BF16 = jnp.bfloat16


def _params(*sem):
    return pltpu.CompilerParams(dimension_semantics=sem, vmem_limit_bytes=VMEM_LIMIT)


def _sigmoid(x):
    return 1.0 / (1.0 + jnp.exp(-x))


def _silu(x):
    return x * _sigmoid(x)


def _softplus(x):
    return jnp.maximum(x, 0.0) + jnp.log1p(jnp.exp(-jnp.abs(x)))


def _gelu_tanh(x):
    return 0.5 * x * (1.0 + jnp.tanh(0.7978845608028654 * (x + 0.044715 * (x * x * x))))


def _rms(x, g):
    r = lax.rsqrt(jnp.mean(x * x, axis=-1, keepdims=True) + EPS)
    return x * r * g


def _dot(a, b):
    return jnp.dot(a, b, preferred_element_type=F32)


def _split_bf16(v, parts):
    out = []
    for _ in range(parts):
        p = v.astype(BF16)
        out.append(p)
        v = v - p.astype(F32)
    return out


def _norm_kernel(x_ref, g_ref, o_ref):
    o_ref[...] = _rms(x_ref[...], g_ref[...]).astype(o_ref.dtype)


def _norm(x, g):
    rows, d = x.shape
    tm = min(rows, 512)
    return pl.pallas_call(
        _norm_kernel,
        grid=(rows // tm,),
        in_specs=[pl.BlockSpec((tm, d), lambda i: (i, 0)), pl.BlockSpec((1, d), lambda i: (0, 0))],
        out_specs=pl.BlockSpec((tm, d), lambda i: (i, 0)),
        out_shape=jax.ShapeDtypeStruct((rows, d), BF16),
        compiler_params=_params("parallel"),
        name="norm",
    )(x, g)


def _proj_kernel(*refs, n_w, n_b, epilogue):
    h_ref = refs[0]
    w_refs = refs[1:1 + n_w]
    b_refs = refs[1 + n_w:1 + n_w + n_b]
    o_ref = refs[-1]
    bs = [b[...] for b in b_refs]
    rows = h_ref.shape[0]
    mc = min(rows, PROJ_ROWS)
    for r0 in range(0, rows, mc):
        h = h_ref[r0:r0 + mc, :]
        accs = [_dot(h, w[...]) for w in w_refs]
        o_ref[r0:r0 + mc, :] = epilogue(accs, bs).astype(o_ref.dtype)


def _proj(h, ws, bs, epilogue, tn, out_dtype, name, n=None, cols=None):
    rows, d = h.shape
    n = n or ws[0].shape[1]
    cols = cols or [0] * len(ws)
    tm = min(rows, 1024)
    kern = functools.partial(_proj_kernel, n_w=len(ws), n_b=len(bs), epilogue=epilogue)
    return pl.pallas_call(
        kern,
        grid=(rows // tm, n // tn),
        in_specs=[pl.BlockSpec((tm, d), lambda i, j: (i, 0))]
                 + [pl.BlockSpec((d, tn), lambda i, j, c=c // tn: (0, j + c)) for c in cols]
                 + [pl.BlockSpec((1, tn), lambda i, j: (0, j)) for _ in bs],
        out_specs=pl.BlockSpec((tm, tn), lambda i, j: (i, j)),
        out_shape=jax.ShapeDtypeStruct((rows, n), out_dtype),
        compiler_params=_params("parallel", "parallel"),
        name=name,
    )(h, *ws, *bs)


def _epi_glu(a, b):
    return a[0] * _sigmoid(a[1])


def _epi_dt(a, b):
    lane = lax.broadcasted_iota(jnp.int32, a[0].shape, 1)
    return jnp.where(lane < N_HEADS, _softplus(a[0] + b[0]), 0.0)


def _epi_silu(a, b):
    return _silu(a[0])


def _epi_id(a, b):
    return a[0]


def _epi_gate(a, b):
    return _sigmoid(a[0] + b[0])


def _merge_kernel(ua_ref, yb_ref, gt_ref, x_ref, wa_ref, ba_ref, wb_ref, wo_ref, gp_ref, o_ref):
    ya = _dot(ua_ref[...], wa_ref[...]) + ba_ref[...]
    yb = _dot(yb_ref[...], wb_ref[...])
    mix = gt_ref[:, :D_MODEL] * ya + gt_ref[:, D_MODEL:] * yb
    m = _dot(mix.astype(BF16), wo_ref[...])
    o_ref[...] = x_ref[...] + _rms(m, gp_ref[...])


def _merge(ua, yb, gates, x, wa, ba, wb, wo, gp):
    rows = x.shape[0]
    tm = min(rows, 256)
    row_spec = lambda w: pl.BlockSpec((tm, w), lambda i: (i, 0))
    res_spec = lambda a: pl.BlockSpec(a.shape, lambda i: (0, 0), pipeline_mode=pl.Buffered(1))
    return pl.pallas_call(
        _merge_kernel,
        grid=(rows // tm,),
        in_specs=[row_spec(D_A), row_spec(D_B), row_spec(2 * D_MODEL), row_spec(D_MODEL),
                  res_spec(wa), res_spec(ba), res_spec(wb), res_spec(wo), res_spec(gp)],
        out_specs=row_spec(D_MODEL),
        out_shape=jax.ShapeDtypeStruct((rows, D_MODEL), F32),
        compiler_params=_params("parallel"),
        name="merge",
    )(ua, yb, gates, x, wa, ba, wb, wo, gp)


def _ffn_kernel(x1_ref, gpre_ref, wg_ref, wv_ref, wd_ref, stg_ref, stv_ref, cwg_ref, cwv_ref, cbg_ref, cbv_ref,
                gpost_ref, o_ref, nsg_ref, nsv_ref, h_scr, acc_scr, act_scr, win_g, win_v, *carry,
                tm, sh, hist, rc):
    l = pl.program_id(1)
    j = pl.program_id(2)
    past = (CONV_F - 1) * sh

    @pl.when(j == 0)
    def _():
        h_scr[...] = _rms(x1_ref[0], gpre_ref[...]).astype(BF16)
        acc_scr[...] = jnp.zeros_like(acc_scr)

    for s, (st_ref, win) in enumerate(((stg_ref, win_g), (stv_ref, win_v))):
        if carry:
            @pl.when(l == 0)
            def _(win=win, st_ref=st_ref):
                win[hist - past:hist, :] = st_ref[0]

            @pl.when(l > 0)
            def _(win=win, s=s):
                win[0:hist, :] = carry[0][j, s]
        else:
            win[hist - past:hist, :] = st_ref[0]

    def conv(win, cw_ref, cb_ref, r0, cs):
        out = cb_ref[:, cs]
        for k in range(CONV_F):
            lo = hist - (CONV_F - 1 - k) * sh + r0
            out = out + cw_ref[k:k + 1, cs] * win[lo:lo + rc, cs]
        return out

    tf = win_g.shape[1]
    cw = min(tf, 256)
    mc = min(tm, PROJ_ROWS)
    for c0 in range(0, tf, cw):
        cs = slice(c0, c0 + cw)
        for m0 in range(0, tm, mc):
            win_g[hist + m0:hist + m0 + mc, cs] = _dot(h_scr[m0:m0 + mc, :], wg_ref[:, cs])
            win_v[hist + m0:hist + m0 + mc, cs] = _dot(h_scr[m0:m0 + mc, :], wv_ref[:, cs])
            for r0 in range(m0, m0 + mc, rc):
                gate = _gelu_tanh(conv(win_g, cwg_ref, cbg_ref, r0, cs))
                act_scr[r0:r0 + rc, cs] = (gate * conv(win_v, cwv_ref, cbv_ref, r0, cs)).astype(BF16)
    acc_scr[...] += _dot(act_scr[...], wd_ref[...])

    for s, (ns_ref, win) in enumerate(((nsg_ref, win_g), (nsv_ref, win_v))):
        ns_ref[0, 0] = win[hist + tm - past:hist + tm, :]
        if carry:
            carry[0][j, s] = win[tm:tm + hist, :]

    @pl.when(j == pl.num_programs(2) - 1)
    def _():
        o_ref[0] = x1_ref[0] + _rms(acc_scr[...], gpost_ref[...])


def _ffn(x1, state, wts, *, tm, sh):
    b, L, _ = x1.shape
    tf = 512
    nj = D_FF // tf
    past = (CONV_F - 1) * sh
    hist = -(-past // SUBLANES) * SUBLANES
    n_l = L // tm
    carry = [pltpu.VMEM((nj, 2, hist, tf), F32)] if n_l > 1 else []
    col = lambda rows, off: pl.BlockSpec((rows, tf), lambda i, l, j: (0, j + off))
    st = lambda off: pl.BlockSpec((1, past, tf), lambda i, l, j: (i, 0, j + off))
    tail = pl.BlockSpec((1, 1, past, tf), lambda i, l, j: (i, l, 0, j))
    vec = pl.BlockSpec((1, D_MODEL), lambda i, l, j: (0, 0))
    xblk = pl.BlockSpec((1, tm, D_MODEL), lambda i, l, j: (i, l, 0))
    out, ns_g, ns_v = pl.pallas_call(
        functools.partial(_ffn_kernel, tm=tm, sh=sh, hist=hist, rc=min(tm, 32)),
        grid=(b, n_l, nj),
        in_specs=[xblk, vec, col(D_MODEL, 0), col(D_MODEL, nj),
                  pl.BlockSpec((tf, D_MODEL), lambda i, l, j: (j, 0)),
                  st(0), st(nj), col(CONV_F, 0), col(CONV_F, nj), col(1, 0), col(1, nj), vec],
        out_specs=[xblk, tail, tail],
        out_shape=[jax.ShapeDtypeStruct((b, L, D_MODEL), F32),
                   jax.ShapeDtypeStruct((b, n_l, past, D_FF), F32),
                   jax.ShapeDtypeStruct((b, n_l, past, D_FF), F32)],
        scratch_shapes=[pltpu.VMEM((tm, D_MODEL), BF16), pltpu.VMEM((tm, D_MODEL), F32),
                        pltpu.VMEM((tm, tf), BF16),
                        pltpu.VMEM((hist + tm, tf), F32), pltpu.VMEM((hist + tm, tf), F32)] + carry,
        compiler_params=_params("parallel", "arbitrary", "arbitrary"),
        name="ffn",
    )(x1, wts["g_pre2"], wts["w_up"], wts["w_up"], wts["w_down"], state, state,
      wts["w_dw_f"], wts["w_dw_f"], wts["b_dw_f"], wts["b_dw_f"], wts["g_post2"])
    return out, jnp.concatenate([ns_g[:, -1], ns_v[:, -1]], axis=-1)


def _post_ln_silu(outs, extras):
    u = outs[0]
    mu = jnp.mean(u, axis=-1, keepdims=True)
    xc = u - mu
    r = lax.rsqrt(jnp.mean(xc * xc, axis=-1, keepdims=True) + EPS)
    return _silu(xc * r * extras[0] + extras[1])


def _post_silu(outs, extras):
    return _silu(outs[0])


def _conv_seq_kernel(*refs, width, tl, n_s, n_x, post, hist, rc):
    u_refs = refs[0:n_s]
    st_refs = refs[n_s:2 * n_s]
    w_refs = refs[2 * n_s:3 * n_s]
    b_refs = refs[3 * n_s:4 * n_s]
    x_refs = refs[4 * n_s:4 * n_s + n_x]
    o_ref = refs[4 * n_s + n_x]
    ns_refs = refs[4 * n_s + n_x + 1:4 * n_s + n_x + 1 + n_s]
    win_refs = refs[4 * n_s + n_x + 1 + n_s:4 * n_s + n_x + 1 + 2 * n_s]
    shf_refs = refs[4 * n_s + n_x + 1 + 2 * n_s:]
    past = width - 1
    l = pl.program_id(2)
    n_shf = hist + tl - SUBLANES

    for s in range(n_s):
        win = win_refs[s]

        @pl.when(l == 0)
        def _(win=win, s=s):
            if hist > past:
                win[0:hist - past, :] = jnp.zeros((hist - past, win.shape[1]), F32)
            win[hist - past:hist, :] = st_refs[s][0]

        @pl.when(l > 0)
        def _(win=win):
            win[0:hist, :] = win[tl:tl + hist, :]

        win[hist:hist + tl, :] = u_refs[s][0]
        for r in range(1, SUBLANES):
            shf_refs[s][r - 1, :, :] = win[r:r + n_shf, :]

    def tap(s, lo):
        a, r = divmod(lo, SUBLANES)
        if r == 0:
            return win_refs[s][lo:lo + rc, :]
        return shf_refs[s][r - 1, a * SUBLANES:a * SUBLANES + rc, :]

    extras = [x[...] for x in x_refs]
    for c in range(tl // rc):
        outs = []
        for s in range(n_s):
            acc = b_refs[s][...] + w_refs[s][0:1, :] * tap(s, c * rc + hist - past)
            for k in range(1, width):
                acc = acc + w_refs[s][k:k + 1, :] * tap(s, c * rc + hist - past + k)
            outs.append(acc)
        o_ref[0, c * rc:(c + 1) * rc, :] = post(outs, extras).astype(o_ref.dtype)

    @pl.when(l == pl.num_programs(2) - 1)
    def _():
        for s in range(n_s):
            ns_refs[s][0] = win_refs[s][hist + tl - past:hist + tl, :]


def _conv_seq(u, state, w, bias, extras, post, *, width, ct, tl, n_s, out_w, out_dtype, name):
    b, L, _ = u.shape
    past = width - 1
    hist = -(-past // SUBLANES) * SUBLANES
    nj = out_w // ct
    rc = min(tl, 64)
    kern = functools.partial(_conv_seq_kernel, width=width, tl=tl, n_s=n_s, n_x=len(extras),
                             post=post, hist=hist, rc=rc)
    in_specs = ([pl.BlockSpec((1, tl, ct), lambda i, j, l, s=s: (i, l, j + s * nj)) for s in range(n_s)]
                + [pl.BlockSpec((1, past, ct), lambda i, j, l, s=s: (i, 0, j + s * nj)) for s in range(n_s)]
                + [pl.BlockSpec((width, ct), lambda i, j, l, s=s: (0, j + s * nj)) for s in range(n_s)]
                + [pl.BlockSpec((1, ct), lambda i, j, l, s=s: (0, j + s * nj)) for s in range(n_s)]
                + [pl.BlockSpec((1, ct), lambda i, j, l: (0, j)) for _ in extras])
    out_specs = ([pl.BlockSpec((1, tl, ct), lambda i, j, l: (i, l, j))]
                 + [pl.BlockSpec((1, past, ct), lambda i, j, l: (i, 0, j)) for _ in range(n_s)])
    out_shape = ([jax.ShapeDtypeStruct((b, L, out_w), out_dtype)]
                 + [jax.ShapeDtypeStruct((b, past, out_w), F32) for _ in range(n_s)])
    res = pl.pallas_call(
        kern,
        grid=(b, nj, L // tl),
        in_specs=in_specs,
        out_specs=out_specs,
        out_shape=out_shape,
        scratch_shapes=[pltpu.VMEM((hist + tl, ct), F32) for _ in range(n_s)]
                       + [pltpu.VMEM((SUBLANES - 1, hist + tl - SUBLANES, ct), F32) for _ in range(n_s)],
        compiler_params=_params("parallel", "parallel", "arbitrary"),
        name=name,
    )(*([u] * n_s), *([state] * n_s), *([w] * n_s), *([bias] * n_s), *extras)
    return res[0], res[1:]


def _conv_slab_kernel(*refs, width, steps, n_s, n_x, post):
    u_refs = refs[0:n_s]
    st_refs = refs[n_s:2 * n_s]
    w_refs = refs[2 * n_s:3 * n_s]
    b_refs = refs[3 * n_s:4 * n_s]
    x_refs = refs[4 * n_s:4 * n_s + n_x]
    o_ref = refs[4 * n_s + n_x]
    ns_refs = refs[4 * n_s + n_x + 1:]
    past = width - 1

    def slab(s, i):
        return st_refs[s][i] if i < past else u_refs[s][i - past]

    extras = [x[...] for x in x_refs]
    for t in range(steps):
        outs = []
        for s in range(n_s):
            acc = b_refs[s][...] + w_refs[s][0:1, :] * slab(s, t)
            for k in range(1, width):
                acc = acc + w_refs[s][k:k + 1, :] * slab(s, t + k)
            outs.append(acc)
        o_ref[t] = post(outs, extras).astype(o_ref.dtype)
    for s in range(n_s):
        for i in range(past):
            ns_refs[s][i] = slab(s, i + steps)


def _conv_slab(u, state, w, bias, extras, post, *, width, ct, nbt, n_s, out_w, out_dtype, name):
    steps, nb, _ = u.shape
    past = width - 1
    nj = out_w // ct
    kern = functools.partial(_conv_slab_kernel, width=width, steps=steps, n_s=n_s, n_x=len(extras), post=post)
    in_specs = ([pl.BlockSpec((steps, nbt, ct), lambda i, j, s=s: (0, i, j + s * nj)) for s in range(n_s)]
                + [pl.BlockSpec((past, nbt, ct), lambda i, j, s=s: (0, i, j + s * nj)) for s in range(n_s)]
                + [pl.BlockSpec((width, ct), lambda i, j, s=s: (0, j + s * nj)) for s in range(n_s)]
                + [pl.BlockSpec((1, ct), lambda i, j, s=s: (0, j + s * nj)) for s in range(n_s)]
                + [pl.BlockSpec((1, ct), lambda i, j: (0, j)) for _ in extras])
    out_specs = ([pl.BlockSpec((steps, nbt, ct), lambda i, j: (0, i, j))]
                 + [pl.BlockSpec((past, nbt, ct), lambda i, j: (0, i, j)) for _ in range(n_s)])
    out_shape = ([jax.ShapeDtypeStruct((steps, nb, out_w), out_dtype)]
                 + [jax.ShapeDtypeStruct((past, nb, out_w), F32) for _ in range(n_s)])
    res = pl.pallas_call(
        kern,
        grid=(nb // nbt, nj),
        in_specs=in_specs,
        out_specs=out_specs,
        out_shape=out_shape,
        compiler_params=_params("parallel", "parallel"),
        name=name,
    )(*([u] * n_s), *([state] * n_s), *([w] * n_s), *([bias] * n_s), *extras)
    return res[0], res[1:]


def _ssd_kernel(*refs, q, bb, conv):
    xbc_ref, dt_ref, zs_ref, h0_ref, alog_ref, dsk_ref, gn_ref, e_ref = refs[:8]
    if conv:
        stb_ref, cw_ref, cb_ref, y_ref, ht_ref, nsb_ref, h_scr, y_scr, win = refs[8:]
    else:
        y_ref, ht_ref, h_scr, y_scr = refs[8:]
    qk = SSD_KEYS
    c = pl.program_id(1)

    @pl.when(c == 0)
    def _():
        h_scr[...] = h0_ref[...]

    a = -jnp.exp(alog_ref[...])
    e = e_ref[...]
    row = lax.broadcasted_iota(jnp.int32, (qk, qk), 0)
    col = lax.broadcasted_iota(jnp.int32, (qk, qk), 1)
    tril = jnp.where(row >= col, 1.0, 0.0).astype(BF16)
    causal = row[:q, :] >= col[:q, :]
    key_head = lax.broadcasted_iota(jnp.int32, (qk, GROUP_W), 1) // HEAD_DIM

    def expand(vs):
        v = jnp.concatenate(vs, axis=0) if len(vs) > 1 else vs[0]
        hi = v.astype(BF16)
        lo = (v - hi.astype(F32)).astype(BF16)
        out = _dot(hi, e) + _dot(lo, e)
        n = vs[0].shape[0]
        return [out[i * n:(i + 1) * n, :] for i in range(len(vs))]

    def pad_keys(v):
        if q == qk:
            return v
        return jnp.concatenate([v, jnp.zeros((qk - q, v.shape[1]), v.dtype)], axis=0)

    seqs = range(bb)
    xqs = []
    for s in seqs:
        if conv:
            past = CONV_B - 1
            hist = SUBLANES

            @pl.when(c == 0)
            def _(s=s):
                win[s, hist - past:hist, :] = stb_ref[s]

            @pl.when(c > 0)
            def _(s=s):
                win[s, 0:hist, :] = win[s, q:q + hist, :]

            win[s, hist:hist + q, :] = xbc_ref[s]
            pre = cb_ref[...] + cw_ref[0:1, :] * win[s, hist - past:hist - past + q, :]
            for k in range(1, CONV_B):
                pre = pre + cw_ref[k:k + 1, :] * win[s, hist - past + k:hist - past + k + q, :]
            xqs.append(_silu(pre))
            nsb_ref[s] = win[s, hist + q - past:hist + q, :]
        else:
            xqs.append(xbc_ref[s])

    xs = [xq[:, :D_B] for xq in xqs]
    bms = [pad_keys(xq[:, D_B:D_B + N_GROUPS * D_STATE]).astype(BF16) for xq in xqs]
    cms = [xq[:, D_B + N_GROUPS * D_STATE:].astype(BF16) for xq in xqs]
    dts = [dt_ref[s] for s in seqs]
    cums = [sum(_dot(tril, p) for p in _split_bf16(pad_keys(dt * a), 3)) for dt in dts]
    cum_ts = [cum.T for cum in cums]
    cum_qs = [cum[:q, :] for cum in cums]
    xdts = [x * d for x, d in zip(xs, expand(dts))]
    chunk_decay = [jnp.exp(cum[qk - 1:qk, :]) for cum in cums]
    to_end = [jnp.exp(cum[qk - 1:qk, :] - cum_q) for cum, cum_q in zip(cums, cum_qs)]
    wide = expand(to_end + [jnp.exp(cum_q) for cum_q in cum_qs])
    xws = [pad_keys(xdt * w).astype(BF16) for xdt, w in zip(xdts, wide[:bb])]
    xdts = [pad_keys(xdt).astype(BF16) for xdt in xdts]
    ecums = wide[bb:]

    for g in range(N_GROUPS):
        cols = slice(g * GROUP_W, (g + 1) * GROUP_W)
        for s in seqs:
            bg = bms[s][:, g * D_STATE:(g + 1) * D_STATE]
            cg = cms[s][:, g * D_STATE:(g + 1) * D_STATE]
            cb = lax.dot_general(cg, bg, (((1,), (1,)), ((), ())), preferred_element_type=F32)
            xdt_g = xdts[s][:, cols]
            scores, keys = [], []
            for r in range(HEADS_PER_GROUP):
                hd = g * HEADS_PER_GROUP + r
                seg = jnp.where(causal, cum_qs[s][:, hd:hd + 1] - cum_ts[s][hd:hd + 1, :], -1e30)
                scores.append((cb * jnp.exp(seg)).astype(BF16))
                keys.append(jnp.where(key_head == r, xdt_g, jnp.zeros_like(xdt_g)))
            y_g = _dot(jnp.concatenate(scores, axis=1), jnp.concatenate(keys, axis=0))
            h_g = h_scr[s, cols, :]
            y_inter = lax.dot_general(cg, h_g.astype(BF16), (((1,), (1,)), ((), ())),
                                      preferred_element_type=F32)
            y_scr[s, :, cols] = y_g + y_inter * ecums[s][:, cols]
            s_g = lax.dot_general(xws[s][:, cols], bg, (((0,), (0,)), ((), ())), preferred_element_type=F32)
            decay = jnp.concatenate(
                [jnp.broadcast_to(chunk_decay[s][:, g * HEADS_PER_GROUP + r:g * HEADS_PER_GROUP + r + 1],
                                  (HEAD_DIM, D_STATE)) for r in range(HEADS_PER_GROUP)], axis=0)
            h_scr[s, cols, :] = decay * h_g + s_g

    for s in seqs:
        y = y_scr[s] + dsk_ref[...] * xs[s]
        y_ref[s] = _rms(y * zs_ref[s], gn_ref[...]).astype(y_ref.dtype)

    @pl.when(c == pl.num_programs(1) - 1)
    def _():
        ht_ref[...] = h_scr[...]


def _ssd(xbc, dt, zs, h0, alog, dsk, gn, e, conv_b=None, *, q, bb):
    b, L, _ = xbc.shape
    seq = lambda w: pl.BlockSpec((bb, q, w), lambda i, c: (i, c, 0))
    full = lambda arr: pl.BlockSpec(arr.shape, lambda i, c: (0,) * arr.ndim)
    state = pl.BlockSpec((bb, D_B, D_STATE), lambda i, c: (i, 0, 0))
    cstate = pl.BlockSpec((bb, CONV_B - 1, D_XBC), lambda i, c: (i, 0, 0))
    conv = conv_b is not None
    return pl.pallas_call(
        functools.partial(_ssd_kernel, q=q, bb=bb, conv=conv),
        grid=(b // bb, L // q),
        in_specs=[seq(D_XBC), seq(LANES), seq(D_B), state,
                  full(alog), full(dsk), full(gn), full(e)]
                 + ([cstate, full(conv_b[1]), full(conv_b[2])] if conv else []),
        out_specs=[seq(D_B), state] + ([cstate] if conv else []),
        out_shape=[jax.ShapeDtypeStruct((b, L, D_B), BF16),
                   jax.ShapeDtypeStruct((b, D_B, D_STATE), F32)]
                  + ([jax.ShapeDtypeStruct((b, CONV_B - 1, D_XBC), F32)] if conv else []),
        scratch_shapes=[pltpu.VMEM((bb, D_B, D_STATE), F32), pltpu.VMEM((bb, q, D_B), F32)]
                       + ([pltpu.VMEM((bb, SUBLANES + q, D_XBC), F32)] if conv else []),
        compiler_params=_params("parallel", "arbitrary"),
        name="ssd",
    )(xbc, dt, zs, h0, alog, dsk, gn, e, *(conv_b or ()))


def _layer(x, wts, conv_a, ssd_fn, ffn_fn):
    h = _norm(x, wts["g_pre1"])
    w_main = wts["w_main"]
    uglu = _proj(h, [w_main, w_main], [], _epi_glu, 512, F32, "in_glu", n=D_A, cols=[0, D_A])
    dt = _proj(h, [wts["w_dt"]], [wts["dt_bias"]], _epi_dt, LANES, F32, "in_dt")
    zs = _proj(h, [w_main], [], _epi_silu, 512, F32, "in_z", n=D_B, cols=[COL_Z])
    xbc = _proj(h, [w_main], [], _epi_id, 512, F32, "in_xbc", n=D_XBC, cols=[COL_XBC])
    gates = _proj(h, [wts["w_g"]], [wts["b_gate"]], _epi_gate, 512, F32, "in_gates")

    ua, st_a = conv_a(uglu)
    yb, st_b, st_h = ssd_fn(xbc, dt, zs)

    x1 = _merge(ua, yb, gates, x, wts["w_a_out"], wts["b_a_out"], wts["w_b_out"], wts["w_o"], wts["g_post1"])
    x2, st_f = ffn_fn(x1)
    return x2, st_a, st_b, st_h, st_f


def _seq_group(x, wts, st_a, st_b, st_h, st_f, tl, q):
    b, L, _ = x.shape

    def conv_a(uglu):
        ua, (ns,) = _conv_seq(uglu.reshape(b, L, D_A), st_a, wts["w_dw_a"], wts["b_dw_a"],
                              [wts["g_ln_a"], wts["b_ln_a"]], _post_ln_silu, width=CONV_A, ct=D_A,
                              tl=min(tl, 256), n_s=1, out_w=D_A, out_dtype=BF16, name="conv_a")
        return ua.reshape(b * L, D_A), ns

    def ssd_fn(xbc, dt, zs):
        yb, ht, ns = _ssd(xbc.reshape(b, L, D_XBC), dt.reshape(b, L, LANES), zs.reshape(b, L, D_B), st_h,
                          wts["a_log"], wts["d_skip"], wts["g_norm_b"], wts["expand"],
                          (st_b, wts["w_dw_b"], wts["b_dw_b"]), q=q, bb=1)
        return yb.reshape(b * L, D_B), ns, ht

    def ffn_fn(x1):
        x2, ns = _ffn(x1.reshape(b, L, D_MODEL), st_f, wts, tm=min(L, 512), sh=1)
        return x2.reshape(b * L, D_MODEL), ns

    x2, ns_a, ns_b, ns_h, ns_f = _layer(x.reshape(b * L, D_MODEL), wts, conv_a, ssd_fn, ffn_fn)
    return x2.reshape(b, L, D_MODEL), ns_a, ns_b, ns_h, ns_f


def _step_group(x, wts, st_a, st_b, st_h, st_f):
    nb, L, _ = x.shape
    tmaj = lambda s: jnp.transpose(s, (1, 0, 2))
    lpad = -(-L // SUBLANES) * SUBLANES

    def conv_a(uglu):
        ua, (ns,) = _conv_slab(uglu.reshape(L, nb, D_A), tmaj(st_a), wts["w_dw_a"], wts["b_dw_a"],
                               [wts["g_ln_a"], wts["b_ln_a"]], _post_ln_silu, width=CONV_A, ct=D_A,
                               nbt=32, n_s=1, out_w=D_A, out_dtype=BF16, name="conv_a_step")
        return ua.reshape(L * nb, D_A), tmaj(ns)

    def bmaj_pad(v):
        return jnp.pad(tmaj(v), ((0, 0), (0, lpad - L), (0, 0)))

    def ssd_fn(xbc, dt, zs):
        act, (ns,) = _conv_slab(xbc.reshape(L, nb, D_XBC), tmaj(st_b), wts["w_dw_b"], wts["b_dw_b"], [],
                                _post_silu, width=CONV_B, ct=512, nbt=nb, n_s=1, out_w=D_XBC,
                                out_dtype=F32, name="conv_b_step")
        yb, ht = _ssd(bmaj_pad(act), bmaj_pad(dt.reshape(L, nb, LANES)), bmaj_pad(zs.reshape(L, nb, D_B)),
                      st_h, wts["a_log"], wts["d_skip"], wts["g_norm_b"], wts["expand"], q=lpad, bb=4)
        return tmaj(yb[:, :L]).reshape(L * nb, D_B), tmaj(ns), ht

    def ffn_fn(x1):
        past = CONV_F - 1
        x2, ns = _ffn(x1.reshape(1, L * nb, D_MODEL), tmaj(st_f).reshape(1, past * nb, 2 * D_FF), wts,
                      tm=L * nb, sh=nb)
        return x2.reshape(L * nb, D_MODEL), tmaj(ns.reshape(past, nb, 2 * D_FF))

    x2, ns_a, ns_b, ns_h, ns_f = _layer(tmaj(x).reshape(L * nb, D_MODEL), wts, conv_a, ssd_fn, ffn_fn)
    return tmaj(x2.reshape(L, nb, D_MODEL)), ns_a, ns_b, ns_h, ns_f


def _layer_weights(l, g_pre1, g_post1, w_in, b_gate, w_dw_a, b_dw_a, g_ln_a, b_ln_a, w_a_out, b_a_out,
                   w_dw_b, b_dw_b, dt_bias, a_log, d_skip, g_norm_b, w_b_out, w_o,
                   g_pre2, g_post2, w_up, w_dw_f, b_dw_f, w_down):
    row = lambda v: v[l].reshape(1, -1)
    lane_pad = lambda v: jnp.pad(v, ((0, 0), (0, LANES - v.shape[1])))
    w = w_in[l]
    head = jnp.arange(LANES, dtype=jnp.int32)[:, None]
    chan_head = (jnp.arange(D_B, dtype=jnp.int32) // HEAD_DIM)[None, :]
    expand = (head == chan_head).astype(BF16)
    return {
        "g_pre1": row(g_pre1), "g_post1": row(g_post1), "g_pre2": row(g_pre2), "g_post2": row(g_post2),
        "w_main": w[:, :COL_DT].astype(BF16),
        "w_dt": lane_pad(w[:, COL_DT:COL_GATE]).astype(BF16), "w_g": w[:, COL_GATE:].astype(BF16),
        "dt_bias": lane_pad(row(dt_bias)), "b_gate": row(b_gate),
        "w_dw_a": w_dw_a[l], "b_dw_a": row(b_dw_a), "g_ln_a": row(g_ln_a), "b_ln_a": row(b_ln_a),
        "w_a_out": w_a_out[l].astype(BF16), "b_a_out": row(b_a_out),
        "w_dw_b": w_dw_b[l], "b_dw_b": row(b_dw_b),
        "a_log": lane_pad(row(a_log)), "d_skip": jnp.repeat(d_skip[l], HEAD_DIM).reshape(1, D_B),
        "g_norm_b": row(g_norm_b), "expand": expand,
        "w_b_out": w_b_out[l].astype(BF16), "w_o": w_o[l].astype(BF16),
        "w_up": w_up[l].astype(BF16), "w_dw_f": w_dw_f[l], "b_dw_f": row(b_dw_f),
        "w_down": w_down[l].astype(BF16),
    }


def kernel(x_prompt, x_sample, state_conv_a, state_conv_b, state_ssm, state_conv_ffn, meta_tokens, g_pre1, g_post1, w_in, b_gate, w_dw_a, b_dw_a, g_ln_a, b_ln_a, w_a_out, b_a_out, w_dw_b, b_dw_b, dt_bias, a_log, d_skip, g_norm_b, w_b_out, w_o, g_pre2, g_post2, w_up, w_dw_f, b_dw_f, w_down):
    depth = w_in.shape[0]
    bp = x_prompt.shape[0]
    nb = x_sample.shape[0]
    xm = meta_tokens.astype(x_prompt.dtype)[None]
    xp, xs = x_prompt, x_sample
    pa, pb, ph, pf = [], [], [], []
    sa, sb, sh, sf = [], [], [], []
    for l in range(depth):
        wts = _layer_weights(l, g_pre1, g_post1, w_in, b_gate, w_dw_a, b_dw_a, g_ln_a, b_ln_a, w_a_out, b_a_out,
                             w_dw_b, b_dw_b, dt_bias, a_log, d_skip, g_norm_b, w_b_out, w_o,
                             g_pre2, g_post2, w_up, w_dw_f, b_dw_f, w_down)
        xm, m_a, m_b, m_h, m_f = _seq_group(
            xm, wts, jnp.zeros((1, CONV_A - 1, D_A), F32), jnp.zeros((1, CONV_B - 1, D_XBC), F32),
            jnp.zeros((1, D_B, D_STATE), F32), jnp.zeros((1, CONV_F - 1, 2 * D_FF), F32), tl=N_META, q=SUBLANES)
        rep = lambda s: jnp.broadcast_to(s, (bp,) + s.shape[1:])
        xp, c_a, c_b, c_h, c_f = _seq_group(xp, wts, rep(m_a), rep(m_b), rep(m_h), rep(m_f), tl=512, q=SSD_KEYS)
        pa.append(c_a); pb.append(c_b); ph.append(c_h.reshape(bp, N_HEADS, HEAD_DIM, D_STATE)); pf.append(c_f)
        xs, d_a, d_b, d_h, d_f = _step_group(
            xs, wts, state_conv_a[l], state_conv_b[l], state_ssm[l].reshape(nb, D_B, D_STATE), state_conv_ffn[l])
        sa.append(d_a); sb.append(d_b); sh.append(d_h.reshape(nb, N_HEADS, HEAD_DIM, D_STATE)); sf.append(d_f)
    return (xp, xs, jnp.stack(pa), jnp.stack(pb), jnp.stack(ph), jnp.stack(pf),
            jnp.stack(sa), jnp.stack(sb), jnp.stack(sh), jnp.stack(sf))
```

```python
import functools

import jax
import jax.numpy as jnp
from jax import lax
from jax.experimental import pallas as pl
from jax.experimental.pallas import tpu as pltpu

D_MODEL = 2048
N_META = 16
D_A = 1024
CONV_A = 31
D_B = 2048
HEAD_DIM = 64
N_HEADS = D_B // HEAD_DIM
N_GROUPS = 8
HEADS_PER_GROUP = N_HEADS // N_GROUPS
GROUP_W = HEADS_PER_GROUP * HEAD_DIM
D_STATE = 128
CONV_B = 4
D_FF = 5632
CONV_F = 3
EPS = 1e-6
D_XBC = D_B + 2 * N_GROUPS * D_STATE
COL_Z = 2 * D_A
COL_XBC = COL_Z + D_B
COL_DT = COL_XBC + D_XBC
COL_GATE = COL_DT + N_HEADS

LANES = 128
SUBLANES = 8
SSD_KEYS = 128
PROJ_ROWS = 256
PROJ_COLS = 1024
VMEM_LIMIT = 56 * 1024 * 1024

F32 = jnp.float32
BF16 = jnp.bfloat16


def _params(*sem):
    return pltpu.CompilerParams(dimension_semantics=sem, vmem_limit_bytes=VMEM_LIMIT)


def _sigmoid(x):
    return 1.0 / (1.0 + jnp.exp(-x))


def _silu(x):
    return x * _sigmoid(x)


def _softplus(x):
    return jnp.maximum(x, 0.0) + jnp.log1p(jnp.exp(-jnp.abs(x)))


def _gelu_tanh(x):
    return 0.5 * x * (1.0 + jnp.tanh(0.7978845608028654 * (x + 0.044715 * (x * x * x))))


def _rms(x, g):
    r = lax.rsqrt(jnp.mean(x * x, axis=-1, keepdims=True) + EPS)
    return x * r * g


def _dot(a, b):
    return jnp.dot(a, b, preferred_element_type=F32)


def _split_bf16(v, parts):
    out = []
    for _ in range(parts):
        p = v.astype(BF16)
        out.append(p)
        v = v - p.astype(F32)
    return out


def _norm_kernel(x_ref, g_ref, o_ref):
    o_ref[...] = _rms(x_ref[...], g_ref[...]).astype(o_ref.dtype)


def _norm(x, g):
    rows, d = x.shape
    tm = min(rows, 512)
    return pl.pallas_call(
        _norm_kernel,
        grid=(rows // tm,),
        in_specs=[pl.BlockSpec((tm, d), lambda i: (i, 0)), pl.BlockSpec((1, d), lambda i: (0, 0))],
        out_specs=pl.BlockSpec((tm, d), lambda i: (i, 0)),
        out_shape=jax.ShapeDtypeStruct((rows, d), BF16),
        compiler_params=_params("parallel"),
        name="norm",
    )(x, g)


def _proj_kernel(*refs, n_w, n_b, epilogue):
    h_ref = refs[0]
    w_refs = refs[1:1 + n_w]
    b_refs = refs[1 + n_w:1 + n_w + n_b]
    o_ref = refs[-1]
    bs = [b[...] for b in b_refs]
    rows = h_ref.shape[0]
    mc = min(rows, PROJ_ROWS)
    for r0 in range(0, rows, mc):
        h = h_ref[r0:r0 + mc, :]
        accs = [_dot(h, w[...]) for w in w_refs]
        o_ref[r0:r0 + mc, :] = epilogue(accs, bs).astype(o_ref.dtype)


def _proj(h, ws, bs, epilogue, tn, out_dtype, name, n=None, cols=None):
    rows, d = h.shape
    n = n or ws[0].shape[1]
    cols = cols or [0] * len(ws)
    tm = min(rows, 1024)
    kern = functools.partial(_proj_kernel, n_w=len(ws), n_b=len(bs), epilogue=epilogue)
    return pl.pallas_call(
        kern,
        grid=(rows // tm, n // tn),
        in_specs=[pl.BlockSpec((tm, d), lambda i, j: (i, 0))]
                 + [pl.BlockSpec((d, tn), lambda i, j, c=c // tn: (0, j + c)) for c in cols]
                 + [pl.BlockSpec((1, tn), lambda i, j: (0, j)) for _ in bs],
        out_specs=pl.BlockSpec((tm, tn), lambda i, j: (i, j)),
        out_shape=jax.ShapeDtypeStruct((rows, n), out_dtype),
        compiler_params=_params("parallel", "parallel"),
        name=name,
    )(h, *ws, *bs)


def _proj_conv_kernel(h_ref, w_ref, st_ref, cw_ref, cb_ref, o_ref, ns_ref, win, *carry, tm, rc):
    l = pl.program_id(1)
    j = pl.program_id(2)
    past = CONV_B - 1
    hist = SUBLANES
    if carry:
        @pl.when(l == 0)
        def _():
            win[hist - past:hist, :] = st_ref[0]

        @pl.when(l > 0)
        def _():
            win[0:hist, :] = carry[0][j]
    else:
        win[hist - past:hist, :] = st_ref[0]

    mc = min(tm, PROJ_ROWS)
    for m0 in range(0, tm, mc):
        win[hist + m0:hist + m0 + mc, :] = _dot(h_ref[0, m0:m0 + mc, :], w_ref[...])
        for r0 in range(m0, m0 + mc, rc):
            pre = cb_ref[...]
            for k in range(CONV_B):
                lo = hist - past + k + r0
                pre = pre + cw_ref[k:k + 1, :] * win[lo:lo + rc, :]
            o_ref[0, r0:r0 + rc, :] = _silu(pre)

    ns_ref[0, 0] = win[hist + tm - past:hist + tm, :]
    if carry:
        carry[0][j] = win[tm:tm + hist, :]


def _proj_conv(h, w, col0, state, cw, cb, name):
    b, L, d = h.shape
    n = state.shape[-1]
    past = CONV_B - 1
    tn = PROJ_COLS
    tm = min(L, 1024)
    n_l, nj = L // tm, n // tn
    carry = [pltpu.VMEM((nj, SUBLANES, tn), F32)] if n_l > 1 else []
    out, ns = pl.pallas_call(
        functools.partial(_proj_conv_kernel, tm=tm, rc=min(tm, 32)),
        grid=(b, n_l, nj),
        in_specs=[pl.BlockSpec((1, tm, d), lambda i, l, j: (i, l, 0)),
                  pl.BlockSpec((d, tn), lambda i, l, j: (0, j + col0 // tn)),
                  pl.BlockSpec((1, past, tn), lambda i, l, j: (i, 0, j)),
                  pl.BlockSpec((CONV_B, tn), lambda i, l, j: (0, j)),
                  pl.BlockSpec((1, tn), lambda i, l, j: (0, j))],
        out_specs=[pl.BlockSpec((1, tm, tn), lambda i, l, j: (i, l, j)),
                   pl.BlockSpec((1, 1, past, tn), lambda i, l, j: (i, l, 0, j))],
        out_shape=[jax.ShapeDtypeStruct((b, L, n), F32), jax.ShapeDtypeStruct((b, n_l, past, n), F32)],
        scratch_shapes=[pltpu.VMEM((SUBLANES + tm, tn), F32)] + carry,
        compiler_params=_params("parallel", "arbitrary", "arbitrary"),
        name=name,
    )(h, w, state, cw, cb)
    return out, ns[:, -1]


def _epi_glu(a, b):
    return a[0] * _sigmoid(a[1])


def _epi_dt(a, b):
    lane = lax.broadcasted_iota(jnp.int32, a[0].shape, 1)
    return jnp.where(lane < N_HEADS, _softplus(a[0] + b[0]), 0.0)


def _epi_silu(a, b):
    return _silu(a[0])


def _epi_id(a, b):
    return a[0]


def _epi_gate(a, b):
    return _sigmoid(a[0] + b[0])


def _merge_kernel(ua_ref, yb_ref, gt_ref, x_ref, wa_ref, ba_ref, wb_ref, wo_ref, gp_ref, o_ref):
    ya = _dot(ua_ref[...], wa_ref[...]) + ba_ref[...]
    yb = _dot(yb_ref[...], wb_ref[...])
    mix = gt_ref[:, :D_MODEL] * ya + gt_ref[:, D_MODEL:] * yb
    m = _dot(mix.astype(BF16), wo_ref[...])
    o_ref[...] = x_ref[...] + _rms(m, gp_ref[...])


def _merge(ua, yb, gates, x, wa, ba, wb, wo, gp):
    rows = x.shape[0]
    tm = min(rows, 256)
    row_spec = lambda w: pl.BlockSpec((tm, w), lambda i: (i, 0))
    res_spec = lambda a: pl.BlockSpec(a.shape, lambda i: (0, 0), pipeline_mode=pl.Buffered(1))
    return pl.pallas_call(
        _merge_kernel,
        grid=(rows // tm,),
        in_specs=[row_spec(D_A), row_spec(D_B), row_spec(2 * D_MODEL), row_spec(D_MODEL),
                  res_spec(wa), res_spec(ba), res_spec(wb), res_spec(wo), res_spec(gp)],
        out_specs=row_spec(D_MODEL),
        out_shape=jax.ShapeDtypeStruct((rows, D_MODEL), F32),
        compiler_params=_params("parallel"),
        name="merge",
    )(ua, yb, gates, x, wa, ba, wb, wo, gp)


def _ffn_kernel(x1_ref, gpre_ref, wg_ref, wv_ref, wd_ref, stg_ref, stv_ref, cwg_ref, cwv_ref, cbg_ref, cbv_ref,
                gpost_ref, o_ref, nsg_ref, nsv_ref, h_scr, acc_scr, act_scr, win_g, win_v, *carry,
                tm, sh, hist, rc):
    l = pl.program_id(1)
    j = pl.program_id(2)
    past = (CONV_F - 1) * sh

    @pl.when(j == 0)
    def _():
        h_scr[...] = _rms(x1_ref[0], gpre_ref[...]).astype(BF16)
        acc_scr[...] = jnp.zeros_like(acc_scr)

    for s, (st_ref, win) in enumerate(((stg_ref, win_g), (stv_ref, win_v))):
        if carry:
            @pl.when(l == 0)
            def _(win=win, st_ref=st_ref):
                win[hist - past:hist, :] = st_ref[0]

            @pl.when(l > 0)
            def _(win=win, s=s):
                win[0:hist, :] = carry[0][j, s]
        else:
            win[hist - past:hist, :] = st_ref[0]

    def conv(win, cw_ref, cb_ref, r0, cs):
        out = cb_ref[:, cs]
        for k in range(CONV_F):
            lo = hist - (CONV_F - 1 - k) * sh + r0
            out = out + cw_ref[k:k + 1, cs] * win[lo:lo + rc, cs]
        return out

    tf = win_g.shape[1]
    cw = min(tf, 256)
    mc = min(tm, PROJ_ROWS)
    for c0 in range(0, tf, cw):
        cs = slice(c0, c0 + cw)
        for m0 in range(0, tm, mc):
            win_g[hist + m0:hist + m0 + mc, cs] = _dot(h_scr[m0:m0 + mc, :], wg_ref[:, cs])
            win_v[hist + m0:hist + m0 + mc, cs] = _dot(h_scr[m0:m0 + mc, :], wv_ref[:, cs])
            for r0 in range(m0, m0 + mc, rc):
                gate = _gelu_tanh(conv(win_g, cwg_ref, cbg_ref, r0, cs))
                act_scr[r0:r0 + rc, cs] = (gate * conv(win_v, cwv_ref, cbv_ref, r0, cs)).astype(BF16)
    acc_scr[...] += _dot(act_scr[...], wd_ref[...])

    for s, (ns_ref, win) in enumerate(((nsg_ref, win_g), (nsv_ref, win_v))):
        ns_ref[0, 0] = win[hist + tm - past:hist + tm, :]
        if carry:
            carry[0][j, s] = win[tm:tm + hist, :]

    @pl.when(j == pl.num_programs(2) - 1)
    def _():
        o_ref[0] = x1_ref[0] + _rms(acc_scr[...], gpost_ref[...])


def _ffn(x1, state, wts, *, tm, sh):
    b, L, _ = x1.shape
    tf = 512
    nj = D_FF // tf
    past = (CONV_F - 1) * sh
    hist = -(-past // SUBLANES) * SUBLANES
    n_l = L // tm
    carry = [pltpu.VMEM((nj, 2, hist, tf), F32)] if n_l > 1 else []
    col = lambda rows, off: pl.BlockSpec((rows, tf), lambda i, l, j: (0, j + off))
    st = lambda off: pl.BlockSpec((1, past, tf), lambda i, l, j: (i, 0, j + off))
    tail = pl.BlockSpec((1, 1, past, tf), lambda i, l, j: (i, l, 0, j))
    vec = pl.BlockSpec((1, D_MODEL), lambda i, l, j: (0, 0))
    xblk = pl.BlockSpec((1, tm, D_MODEL), lambda i, l, j: (i, l, 0))
    out, ns_g, ns_v = pl.pallas_call(
        functools.partial(_ffn_kernel, tm=tm, sh=sh, hist=hist, rc=min(tm, 32)),
        grid=(b, n_l, nj),
        in_specs=[xblk, vec, col(D_MODEL, 0), col(D_MODEL, nj),
                  pl.BlockSpec((tf, D_MODEL), lambda i, l, j: (j, 0)),
                  st(0), st(nj), col(CONV_F, 0), col(CONV_F, nj), col(1, 0), col(1, nj), vec],
        out_specs=[xblk, tail, tail],
        out_shape=[jax.ShapeDtypeStruct((b, L, D_MODEL), F32),
                   jax.ShapeDtypeStruct((b, n_l, past, D_FF), F32),
                   jax.ShapeDtypeStruct((b, n_l, past, D_FF), F32)],
        scratch_shapes=[pltpu.VMEM((tm, D_MODEL), BF16), pltpu.VMEM((tm, D_MODEL), F32),
                        pltpu.VMEM((tm, tf), BF16),
                        pltpu.VMEM((hist + tm, tf), F32), pltpu.VMEM((hist + tm, tf), F32)] + carry,
        compiler_params=_params("parallel", "arbitrary", "arbitrary"),
        name="ffn",
    )(x1, wts["g_pre2"], wts["w_up"], wts["w_up"], wts["w_down"], state, state,
      wts["w_dw_f"], wts["w_dw_f"], wts["b_dw_f"], wts["b_dw_f"], wts["g_post2"])
    return out, jnp.concatenate([ns_g[:, -1], ns_v[:, -1]], axis=-1)


def _post_ln_silu(outs, extras):
    u = outs[0]
    mu = jnp.mean(u, axis=-1, keepdims=True)
    xc = u - mu
    r = lax.rsqrt(jnp.mean(xc * xc, axis=-1, keepdims=True) + EPS)
    return _silu(xc * r * extras[0] + extras[1])


def _post_silu(outs, extras):
    return _silu(outs[0])


def _conv_seq_kernel(*refs, width, tl, n_s, n_x, post, hist, rc):
    u_refs = refs[0:n_s]
    st_refs = refs[n_s:2 * n_s]
    w_refs = refs[2 * n_s:3 * n_s]
    b_refs = refs[3 * n_s:4 * n_s]
    x_refs = refs[4 * n_s:4 * n_s + n_x]
    o_ref = refs[4 * n_s + n_x]
    ns_refs = refs[4 * n_s + n_x + 1:4 * n_s + n_x + 1 + n_s]
    win_refs = refs[4 * n_s + n_x + 1 + n_s:4 * n_s + n_x + 1 + 2 * n_s]
    shf_refs = refs[4 * n_s + n_x + 1 + 2 * n_s:]
    past = width - 1
    l = pl.program_id(2)
    n_shf = hist + tl - SUBLANES

    for s in range(n_s):
        win = win_refs[s]

        @pl.when(l == 0)
        def _(win=win, s=s):
            if hist > past:
                win[0:hist - past, :] = jnp.zeros((hist - past, win.shape[1]), F32)
            win[hist - past:hist, :] = st_refs[s][0]

        @pl.when(l > 0)
        def _(win=win):
            win[0:hist, :] = win[tl:tl + hist, :]

        win[hist:hist + tl, :] = u_refs[s][0]
        for r in range(1, SUBLANES):
            shf_refs[s][r - 1, :, :] = win[r:r + n_shf, :]

    def tap(s, lo):
        a, r = divmod(lo, SUBLANES)
        if r == 0:
            return win_refs[s][lo:lo + rc, :]
        return shf_refs[s][r - 1, a * SUBLANES:a * SUBLANES + rc, :]

    extras = [x[...] for x in x_refs]
    for c in range(tl // rc):
        outs = []
        for s in range(n_s):
            acc = b_refs[s][...] + w_refs[s][0:1, :] * tap(s, c * rc + hist - past)
            for k in range(1, width):
                acc = acc + w_refs[s][k:k + 1, :] * tap(s, c * rc + hist - past + k)
            outs.append(acc)
        o_ref[0, c * rc:(c + 1) * rc, :] = post(outs, extras).astype(o_ref.dtype)

    @pl.when(l == pl.num_programs(2) - 1)
    def _():
        for s in range(n_s):
            ns_refs[s][0] = win_refs[s][hist + tl - past:hist + tl, :]


def _conv_seq(u, state, w, bias, extras, post, *, width, ct, tl, n_s, out_w, out_dtype, name):
    b, L, _ = u.shape
    past = width - 1
    hist = -(-past // SUBLANES) * SUBLANES
    nj = out_w // ct
    rc = min(tl, 64)
    kern = functools.partial(_conv_seq_kernel, width=width, tl=tl, n_s=n_s, n_x=len(extras),
                             post=post, hist=hist, rc=rc)
    in_specs = ([pl.BlockSpec((1, tl, ct), lambda i, j, l, s=s: (i, l, j + s * nj)) for s in range(n_s)]
                + [pl.BlockSpec((1, past, ct), lambda i, j, l, s=s: (i, 0, j + s * nj)) for s in range(n_s)]
                + [pl.BlockSpec((width, ct), lambda i, j, l, s=s: (0, j + s * nj)) for s in range(n_s)]
                + [pl.BlockSpec((1, ct), lambda i, j, l, s=s: (0, j + s * nj)) for s in range(n_s)]
                + [pl.BlockSpec((1, ct), lambda i, j, l: (0, j)) for _ in extras])
    out_specs = ([pl.BlockSpec((1, tl, ct), lambda i, j, l: (i, l, j))]
                 + [pl.BlockSpec((1, past, ct), lambda i, j, l: (i, 0, j)) for _ in range(n_s)])
    out_shape = ([jax.ShapeDtypeStruct((b, L, out_w), out_dtype)]
                 + [jax.ShapeDtypeStruct((b, past, out_w), F32) for _ in range(n_s)])
    res = pl.pallas_call(
        kern,
        grid=(b, nj, L // tl),
        in_specs=in_specs,
        out_specs=out_specs,
        out_shape=out_shape,
        scratch_shapes=[pltpu.VMEM((hist + tl, ct), F32) for _ in range(n_s)]
                       + [pltpu.VMEM((SUBLANES - 1, hist + tl - SUBLANES, ct), F32) for _ in range(n_s)],
        compiler_params=_params("parallel", "parallel", "arbitrary"),
        name=name,
    )(*([u] * n_s), *([state] * n_s), *([w] * n_s), *([bias] * n_s), *extras)
    return res[0], res[1:]


def _conv_slab_kernel(*refs, width, steps, n_s, n_x, post):
    u_refs = refs[0:n_s]
    st_refs = refs[n_s:2 * n_s]
    w_refs = refs[2 * n_s:3 * n_s]
    b_refs = refs[3 * n_s:4 * n_s]
    x_refs = refs[4 * n_s:4 * n_s + n_x]
    o_ref = refs[4 * n_s + n_x]
    ns_refs = refs[4 * n_s + n_x + 1:]
    past = width - 1

    def slab(s, i):
        return st_refs[s][i] if i < past else u_refs[s][i - past]

    extras = [x[...] for x in x_refs]
    for t in range(steps):
        outs = []
        for s in range(n_s):
            acc = b_refs[s][...] + w_refs[s][0:1, :] * slab(s, t)
            for k in range(1, width):
                acc = acc + w_refs[s][k:k + 1, :] * slab(s, t + k)
            outs.append(acc)
        o_ref[t] = post(outs, extras).astype(o_ref.dtype)
    for s in range(n_s):
        for i in range(past):
            ns_refs[s][i] = slab(s, i + steps)


def _conv_slab(u, state, w, bias, extras, post, *, width, ct, nbt, n_s, out_w, out_dtype, name):
    steps, nb, _ = u.shape
    past = width - 1
    nj = out_w // ct
    kern = functools.partial(_conv_slab_kernel, width=width, steps=steps, n_s=n_s, n_x=len(extras), post=post)
    in_specs = ([pl.BlockSpec((steps, nbt, ct), lambda i, j, s=s: (0, i, j + s * nj)) for s in range(n_s)]
                + [pl.BlockSpec((past, nbt, ct), lambda i, j, s=s: (0, i, j + s * nj)) for s in range(n_s)]
                + [pl.BlockSpec((width, ct), lambda i, j, s=s: (0, j + s * nj)) for s in range(n_s)]
                + [pl.BlockSpec((1, ct), lambda i, j, s=s: (0, j + s * nj)) for s in range(n_s)]
                + [pl.BlockSpec((1, ct), lambda i, j: (0, j)) for _ in extras])
    out_specs = ([pl.BlockSpec((steps, nbt, ct), lambda i, j: (0, i, j))]
                 + [pl.BlockSpec((past, nbt, ct), lambda i, j: (0, i, j)) for _ in range(n_s)])
    out_shape = ([jax.ShapeDtypeStruct((steps, nb, out_w), out_dtype)]
                 + [jax.ShapeDtypeStruct((past, nb, out_w), F32) for _ in range(n_s)])
    res = pl.pallas_call(
        kern,
        grid=(nb // nbt, nj),
        in_specs=in_specs,
        out_specs=out_specs,
        out_shape=out_shape,
        compiler_params=_params("parallel", "parallel"),
        name=name,
    )(*([u] * n_s), *([state] * n_s), *([w] * n_s), *([bias] * n_s), *extras)
    return res[0], res[1:]


def _ssd_kernel(xbc_ref, dt_ref, zs_ref, h0_ref, alog_ref, dsk_ref, gn_ref, e_ref, y_ref, ht_ref, h_scr, y_scr,
                *, q, bb):
    qk = SSD_KEYS
    c = pl.program_id(1)

    @pl.when(c == 0)
    def _():
        h_scr[...] = h0_ref[...]

    a = -jnp.exp(alog_ref[...])
    e = e_ref[...]
    row = lax.broadcasted_iota(jnp.int32, (qk, qk), 0)
    col = lax.broadcasted_iota(jnp.int32, (qk, qk), 1)
    tril = jnp.where(row >= col, 1.0, 0.0).astype(BF16)
    causal = row[:q, :] >= col[:q, :]
    key_head = lax.broadcasted_iota(jnp.int32, (qk, GROUP_W), 1) // HEAD_DIM

    def expand(vs):
        v = jnp.concatenate(vs, axis=0) if len(vs) > 1 else vs[0]
        hi = v.astype(BF16)
        lo = (v - hi.astype(F32)).astype(BF16)
        out = _dot(hi, e) + _dot(lo, e)
        n = vs[0].shape[0]
        return [out[i * n:(i + 1) * n, :] for i in range(len(vs))]

    def pad_keys(v):
        if q == qk:
            return v
        return jnp.concatenate([v, jnp.zeros((qk - q, v.shape[1]), v.dtype)], axis=0)

    seqs = range(bb)
    xqs = [xbc_ref[s] for s in seqs]
    xs = [xq[:, :D_B] for xq in xqs]
    bms = [pad_keys(xq[:, D_B:D_B + N_GROUPS * D_STATE]).astype(BF16) for xq in xqs]
    cms = [xq[:, D_B + N_GROUPS * D_STATE:].astype(BF16) for xq in xqs]
    dts = [dt_ref[s] for s in seqs]
    cums = [sum(_dot(tril, p) for p in _split_bf16(pad_keys(dt * a), 3)) for dt in dts]
    cum_ts = [cum.T for cum in cums]
    cum_qs = [cum[:q, :] for cum in cums]
    xdts = [x * d for x, d in zip(xs, expand(dts))]
    chunk_decay = [jnp.exp(cum[qk - 1:qk, :]) for cum in cums]
    to_end = [jnp.exp(cum[qk - 1:qk, :] - cum_q) for cum, cum_q in zip(cums, cum_qs)]
    wide = expand(to_end + [jnp.exp(cum_q) for cum_q in cum_qs])
    xws = [pad_keys(xdt * w).astype(BF16) for xdt, w in zip(xdts, wide[:bb])]
    xdts = [pad_keys(xdt).astype(BF16) for xdt in xdts]
    ecums = wide[bb:]

    for g in range(N_GROUPS):
        cols = slice(g * GROUP_W, (g + 1) * GROUP_W)
        for s in seqs:
            bg = bms[s][:, g * D_STATE:(g + 1) * D_STATE]
            cg = cms[s][:, g * D_STATE:(g + 1) * D_STATE]
            cb = lax.dot_general(cg, bg, (((1,), (1,)), ((), ())), preferred_element_type=F32)
            xdt_g = xdts[s][:, cols]
            scores, keys = [], []
            for r in range(HEADS_PER_GROUP):
                hd = g * HEADS_PER_GROUP + r
                seg = jnp.where(causal, cum_qs[s][:, hd:hd + 1] - cum_ts[s][hd:hd + 1, :], -1e30)
                scores.append((cb * jnp.exp(seg)).astype(BF16))
                keys.append(jnp.where(key_head == r, xdt_g, jnp.zeros_like(xdt_g)))
            y_g = _dot(jnp.concatenate(scores, axis=1), jnp.concatenate(keys, axis=0))
            h_g = h_scr[s, cols, :]
            y_inter = lax.dot_general(cg, h_g.astype(BF16), (((1,), (1,)), ((), ())),
                                      preferred_element_type=F32)
            y_scr[s, :, cols] = y_g + y_inter * ecums[s][:, cols]
            s_g = lax.dot_general(xws[s][:, cols], bg, (((0,), (0,)), ((), ())), preferred_element_type=F32)
            decay = jnp.concatenate(
                [jnp.broadcast_to(chunk_decay[s][:, g * HEADS_PER_GROUP + r:g * HEADS_PER_GROUP + r + 1],
                                  (HEAD_DIM, D_STATE)) for r in range(HEADS_PER_GROUP)], axis=0)
            h_scr[s, cols, :] = decay * h_g + s_g

    for s in seqs:
        y = y_scr[s] + dsk_ref[...] * xs[s]
        y_ref[s] = _rms(y * zs_ref[s], gn_ref[...]).astype(y_ref.dtype)

    @pl.when(c == pl.num_programs(1) - 1)
    def _():
        ht_ref[...] = h_scr[...]


def _ssd(xbc, dt, zs, h0, alog, dsk, gn, e, *, q, bb):
    b, L, _ = xbc.shape
    seq = lambda w: pl.BlockSpec((bb, q, w), lambda i, c: (i, c, 0))
    full = lambda arr: pl.BlockSpec(arr.shape, lambda i, c: (0,) * arr.ndim)
    state = pl.BlockSpec((bb, D_B, D_STATE), lambda i, c: (i, 0, 0))
    return pl.pallas_call(
        functools.partial(_ssd_kernel, q=q, bb=bb),
        grid=(b // bb, L // q),
        in_specs=[seq(D_XBC), seq(LANES), seq(D_B), state, full(alog), full(dsk), full(gn), full(e)],
        out_specs=[seq(D_B), state],
        out_shape=[jax.ShapeDtypeStruct((b, L, D_B), BF16), jax.ShapeDtypeStruct((b, D_B, D_STATE), F32)],
        scratch_shapes=[pltpu.VMEM((bb, D_B, D_STATE), F32), pltpu.VMEM((bb, q, D_B), F32)],
        compiler_params=_params("parallel", "arbitrary"),
        name="ssd",
    )(xbc, dt, zs, h0, alog, dsk, gn, e)


def _layer(x, wts, conv_a, xbc_fn, ssd_fn, ffn_fn):
    h = _norm(x, wts["g_pre1"])
    w_main = wts["w_main"]
    uglu = _proj(h, [w_main, w_main], [], _epi_glu, PROJ_COLS, F32, "in_glu", n=D_A, cols=[0, D_A])
    dt = _proj(h, [wts["w_dt"]], [wts["dt_bias"]], _epi_dt, LANES, F32, "in_dt")
    zs = _proj(h, [w_main], [], _epi_silu, PROJ_COLS, F32, "in_z", n=D_B, cols=[COL_Z])
    gates = _proj(h, [wts["w_g"]], [wts["b_gate"]], _epi_gate, PROJ_COLS, F32, "in_gates")

    ua, st_a = conv_a(uglu)
    xbc_act, st_b = xbc_fn(h)
    yb, st_h = ssd_fn(xbc_act, dt, zs)

    x1 = _merge(ua, yb, gates, x, wts["w_a_out"], wts["b_a_out"], wts["w_b_out"], wts["w_o"], wts["g_post1"])
    x2, st_f = ffn_fn(x1)
    return x2, st_a, st_b, st_h, st_f


def _seq_group(x, wts, st_a, st_b, st_h, st_f, tl, q):
    b, L, _ = x.shape

    def conv_a(uglu):
        ua, (ns,) = _conv_seq(uglu.reshape(b, L, D_A), st_a, wts["w_dw_a"], wts["b_dw_a"],
                              [wts["g_ln_a"], wts["b_ln_a"]], _post_ln_silu, width=CONV_A, ct=D_A,
                              tl=min(tl, 256), n_s=1, out_w=D_A, out_dtype=BF16, name="conv_a")
        return ua.reshape(b * L, D_A), ns

    def xbc_fn(h):
        return _proj_conv(h.reshape(b, L, D_MODEL), wts["w_main"], COL_XBC, st_b, wts["w_dw_b"], wts["b_dw_b"],
                          "in_xbc_conv")

    def ssd_fn(xbc_act, dt, zs):
        yb, ht = _ssd(xbc_act, dt.reshape(b, L, LANES), zs.reshape(b, L, D_B), st_h,
                      wts["a_log"], wts["d_skip"], wts["g_norm_b"], wts["expand"], q=q, bb=1)
        return yb.reshape(b * L, D_B), ht

    def ffn_fn(x1):
        x2, ns = _ffn(x1.reshape(b, L, D_MODEL), st_f, wts, tm=min(L, 512), sh=1)
        return x2.reshape(b * L, D_MODEL), ns

    x2, ns_a, ns_b, ns_h, ns_f = _layer(x.reshape(b * L, D_MODEL), wts, conv_a, xbc_fn, ssd_fn, ffn_fn)
    return x2.reshape(b, L, D_MODEL), ns_a, ns_b, ns_h, ns_f


def _step_group(x, wts, st_a, st_b, st_h, st_f):
    nb, L, _ = x.shape
    tmaj = lambda s: jnp.transpose(s, (1, 0, 2))
    lpad = -(-L // SUBLANES) * SUBLANES

    def conv_a(uglu):
        ua, (ns,) = _conv_slab(uglu.reshape(L, nb, D_A), tmaj(st_a), wts["w_dw_a"], wts["b_dw_a"],
                               [wts["g_ln_a"], wts["b_ln_a"]], _post_ln_silu, width=CONV_A, ct=D_A,
                               nbt=32, n_s=1, out_w=D_A, out_dtype=BF16, name="conv_a_step")
        return ua.reshape(L * nb, D_A), tmaj(ns)

    def bmaj_pad(v):
        return jnp.pad(tmaj(v), ((0, 0), (0, lpad - L), (0, 0)))

    def xbc_fn(h):
        xbc = _proj(h, [wts["w_main"]], [], _epi_id, PROJ_COLS, F32, "in_xbc", n=D_XBC, cols=[COL_XBC])
        act, (ns,) = _conv_slab(xbc.reshape(L, nb, D_XBC), tmaj(st_b), wts["w_dw_b"], wts["b_dw_b"], [],
                                _post_silu, width=CONV_B, ct=512, nbt=nb, n_s=1, out_w=D_XBC,
                                out_dtype=F32, name="conv_b_step")
        return act, tmaj(ns)

    def ssd_fn(xbc_act, dt, zs):
        yb, ht = _ssd(bmaj_pad(xbc_act), bmaj_pad(dt.reshape(L, nb, LANES)), bmaj_pad(zs.reshape(L, nb, D_B)),
                      st_h, wts["a_log"], wts["d_skip"], wts["g_norm_b"], wts["expand"], q=lpad, bb=4)
        return tmaj(yb[:, :L]).reshape(L * nb, D_B), ht

    def ffn_fn(x1):
        past = CONV_F - 1
        x2, ns = _ffn(x1.reshape(1, L * nb, D_MODEL), tmaj(st_f).reshape(1, past * nb, 2 * D_FF), wts,
                      tm=L * nb, sh=nb)
        return x2.reshape(L * nb, D_MODEL), tmaj(ns.reshape(past, nb, 2 * D_FF))

    x2, ns_a, ns_b, ns_h, ns_f = _layer(tmaj(x).reshape(L * nb, D_MODEL), wts, conv_a, xbc_fn, ssd_fn, ffn_fn)
    return tmaj(x2.reshape(L, nb, D_MODEL)), ns_a, ns_b, ns_h, ns_f


def _layer_weights(l, g_pre1, g_post1, w_in, b_gate, w_dw_a, b_dw_a, g_ln_a, b_ln_a, w_a_out, b_a_out,
                   w_dw_b, b_dw_b, dt_bias, a_log, d_skip, g_norm_b, w_b_out, w_o,
                   g_pre2, g_post2, w_up, w_dw_f, b_dw_f, w_down):
    row = lambda v: v[l].reshape(1, -1)
    lane_pad = lambda v: jnp.pad(v, ((0, 0), (0, LANES - v.shape[1])))
    w = w_in[l]
    head = jnp.arange(LANES, dtype=jnp.int32)[:, None]
    chan_head = (jnp.arange(D_B, dtype=jnp.int32) // HEAD_DIM)[None, :]
    expand = (head == chan_head).astype(BF16)
    return {
        "g_pre1": row(g_pre1), "g_post1": row(g_post1), "g_pre2": row(g_pre2), "g_post2": row(g_post2),
        "w_main": w[:, :COL_DT].astype(BF16),
        "w_dt": lane_pad(w[:, COL_DT:COL_GATE]).astype(BF16), "w_g": w[:, COL_GATE:].astype(BF16),
        "dt_bias": lane_pad(row(dt_bias)), "b_gate": row(b_gate),
        "w_dw_a": w_dw_a[l], "b_dw_a": row(b_dw_a), "g_ln_a": row(g_ln_a), "b_ln_a": row(b_ln_a),
        "w_a_out": w_a_out[l].astype(BF16), "b_a_out": row(b_a_out),
        "w_dw_b": w_dw_b[l], "b_dw_b": row(b_dw_b),
        "a_log": lane_pad(row(a_log)), "d_skip": jnp.repeat(d_skip[l], HEAD_DIM).reshape(1, D_B),
        "g_norm_b": row(g_norm_b), "expand": expand,
        "w_b_out": w_b_out[l].astype(BF16), "w_o": w_o[l].astype(BF16),
        "w_up": w_up[l].astype(BF16), "w_dw_f": w_dw_f[l], "b_dw_f": row(b_dw_f),
        "w_down": w_down[l].astype(BF16),
    }


def kernel(x_prompt, x_sample, state_conv_a, state_conv_b, state_ssm, state_conv_ffn, meta_tokens, g_pre1, g_post1, w_in, b_gate, w_dw_a, b_dw_a, g_ln_a, b_ln_a, w_a_out, b_a_out, w_dw_b, b_dw_b, dt_bias, a_log, d_skip, g_norm_b, w_b_out, w_o, g_pre2, g_post2, w_up, w_dw_f, b_dw_f, w_down):
    depth = w_in.shape[0]
    bp = x_prompt.shape[0]
    nb = x_sample.shape[0]
    xm = meta_tokens.astype(x_prompt.dtype)[None]
    xp, xs = x_prompt, x_sample
    pa, pb, ph, pf = [], [], [], []
    sa, sb, sh, sf = [], [], [], []
    for l in range(depth):
        wts = _layer_weights(l, g_pre1, g_post1, w_in, b_gate, w_dw_a, b_dw_a, g_ln_a, b_ln_a, w_a_out, b_a_out,
                             w_dw_b, b_dw_b, dt_bias, a_log, d_skip, g_norm_b, w_b_out, w_o,
                             g_pre2, g_post2, w_up, w_dw_f, b_dw_f, w_down)
        xm, m_a, m_b, m_h, m_f = _seq_group(
            xm, wts, jnp.zeros((1, CONV_A - 1, D_A), F32), jnp.zeros((1, CONV_B - 1, D_XBC), F32),
            jnp.zeros((1, D_B, D_STATE), F32), jnp.zeros((1, CONV_F - 1, 2 * D_FF), F32), tl=N_META, q=SUBLANES)
        rep = lambda s: jnp.broadcast_to(s, (bp,) + s.shape[1:])
        xp, c_a, c_b, c_h, c_f = _seq_group(xp, wts, rep(m_a), rep(m_b), rep(m_h), rep(m_f), tl=512, q=SSD_KEYS)
        pa.append(c_a); pb.append(c_b); ph.append(c_h.reshape(bp, N_HEADS, HEAD_DIM, D_STATE)); pf.append(c_f)
        xs, d_a, d_b, d_h, d_f = _step_group(
            xs, wts, state_conv_a[l], state_conv_b[l], state_ssm[l].reshape(nb, D_B, D_STATE), state_conv_ffn[l])
        sa.append(d_a); sb.append(d_b); sh.append(d_h.reshape(nb, N_HEADS, HEAD_DIM, D_STATE)); sf.append(d_f)
    return (xp, xs, jnp.stack(pa), jnp.stack(pb), jnp.stack(ph), jnp.stack(pf),
            jnp.stack(sa), jnp.stack(sb), jnp.stack(sh), jnp.stack(sf))
```

```python
import functools

import jax
import jax.numpy as jnp
from jax import lax
from jax.experimental import pallas as pl
from jax.experimental.pallas import tpu as pltpu

D_MODEL = 2048
N_META = 16
D_A = 1024
CONV_A = 31
D_B = 2048
HEAD_DIM = 64
N_HEADS = D_B // HEAD_DIM
N_GROUPS = 8
HEADS_PER_GROUP = N_HEADS // N_GROUPS
GROUP_W = HEADS_PER_GROUP * HEAD_DIM
D_STATE = 128
CONV_B = 4
D_FF = 5632
CONV_F = 3
EPS = 1e-6
D_XBC = D_B + 2 * N_GROUPS * D_STATE
COL_Z = 2 * D_A
COL_XBC = COL_Z + D_B
COL_DT = COL_XBC + D_XBC
COL_GATE = COL_DT + N_HEADS

LANES = 128
SUBLANES = 8
SSD_KEYS = 128
PROJ_ROWS = 256
PROJ_COLS = 1024
VMEM_LIMIT = 56 * 1024 * 1024

F32 = jnp.float32
BF16 = jnp.bfloat16


def _params(*sem):
    return pltpu.CompilerParams(dimension_semantics=sem, vmem_limit_bytes=VMEM_LIMIT)


def _sigmoid(x):
    return 1.0 / (1.0 + jnp.exp(-x))


def _silu(x):
    return x * _sigmoid(x)


def _softplus(x):
    return jnp.maximum(x, 0.0) + jnp.log1p(jnp.exp(-jnp.abs(x)))


def _gelu_tanh(x):
    return 0.5 * x * (1.0 + jnp.tanh(0.7978845608028654 * (x + 0.044715 * (x * x * x))))


def _rms(x, g):
    r = lax.rsqrt(jnp.mean(x * x, axis=-1, keepdims=True) + EPS)
    return x * r * g


def _dot(a, b):
    return jnp.dot(a, b, preferred_element_type=F32)


def _rows(ref, lo, n, cols=slice(None)):
    base, off = divmod(lo, SUBLANES)
    if off == 0:
        return ref[lo:lo + n, cols]
    blk = ref[base * SUBLANES:base * SUBLANES + n + SUBLANES, cols]
    return pltpu.roll(blk, n + SUBLANES - off, 0)[0:n, :]


def _split_bf16(v, parts):
    out = []
    for _ in range(parts):
        p = v.astype(BF16)
        out.append(p)
        v = v - p.astype(F32)
    return out


def _norm_kernel(x_ref, g_ref, o_ref):
    o_ref[...] = _rms(x_ref[...], g_ref[...]).astype(o_ref.dtype)


def _norm(x, g):
    rows, d = x.shape
    tm = min(rows, 512)
    return pl.pallas_call(
        _norm_kernel,
        grid=(rows // tm,),
        in_specs=[pl.BlockSpec((tm, d), lambda i: (i, 0)), pl.BlockSpec((1, d), lambda i: (0, 0))],
        out_specs=pl.BlockSpec((tm, d), lambda i: (i, 0)),
        out_shape=jax.ShapeDtypeStruct((rows, d), BF16),
        compiler_params=_params("parallel"),
        name="norm",
    )(x, g)


def _proj_kernel(*refs, n_w, n_b, epilogue):
    h_ref = refs[0]
    w_refs = refs[1:1 + n_w]
    b_refs = refs[1 + n_w:1 + n_w + n_b]
    o_ref = refs[-1]
    bs = [b[...] for b in b_refs]
    rows = h_ref.shape[0]
    mc = min(rows, PROJ_ROWS)
    for r0 in range(0, rows, mc):
        h = h_ref[r0:r0 + mc, :]
        accs = [_dot(h, w[...]) for w in w_refs]
        o_ref[r0:r0 + mc, :] = epilogue(accs, bs).astype(o_ref.dtype)


def _proj(h, ws, bs, epilogue, tn, out_dtype, name, n=None, cols=None):
    rows, d = h.shape
    n = n or ws[0].shape[1]
    cols = cols or [0] * len(ws)
    tm = min(rows, 1024)
    kern = functools.partial(_proj_kernel, n_w=len(ws), n_b=len(bs), epilogue=epilogue)
    return pl.pallas_call(
        kern,
        grid=(rows // tm, n // tn),
        in_specs=[pl.BlockSpec((tm, d), lambda i, j: (i, 0))]
                 + [pl.BlockSpec((d, tn), lambda i, j, c=c // tn: (0, j + c)) for c in cols]
                 + [pl.BlockSpec((1, tn), lambda i, j: (0, j)) for _ in bs],
        out_specs=pl.BlockSpec((tm, tn), lambda i, j: (i, j)),
        out_shape=jax.ShapeDtypeStruct((rows, n), out_dtype),
        compiler_params=_params("parallel", "parallel"),
        name=name,
    )(h, *ws, *bs)


def _proj_conv_kernel(h_ref, w_ref, st_ref, cw_ref, cb_ref, o_ref, ns_ref, win, *carry, tm, rc):
    l = pl.program_id(1)
    j = pl.program_id(2)
    past = CONV_B - 1
    hist = SUBLANES
    if carry:
        @pl.when(l == 0)
        def _():
            win[hist - past:hist, :] = st_ref[0]

        @pl.when(l > 0)
        def _():
            win[0:hist, :] = carry[0][j]
    else:
        win[hist - past:hist, :] = st_ref[0]

    mc = min(tm, PROJ_ROWS)
    for m0 in range(0, tm, mc):
        win[hist + m0:hist + m0 + mc, :] = _dot(h_ref[0, m0:m0 + mc, :], w_ref[...])
        for r0 in range(m0, m0 + mc, rc):
            pre = cb_ref[...]
            for k in range(CONV_B):
                pre = pre + cw_ref[k:k + 1, :] * _rows(win, r0 + hist - past + k, rc)
            o_ref[0, r0:r0 + rc, :] = _silu(pre)

    ns_ref[0, 0] = win[hist + tm - past:hist + tm, :]
    if carry:
        carry[0][j] = win[tm:tm + hist, :]


def _proj_conv(h, w, col0, state, cw, cb, name):
    b, L, d = h.shape
    n = state.shape[-1]
    past = CONV_B - 1
    tn = PROJ_COLS
    tm = min(L, 1024)
    n_l, nj = L // tm, n // tn
    carry = [pltpu.VMEM((nj, SUBLANES, tn), F32)] if n_l > 1 else []
    out, ns = pl.pallas_call(
        functools.partial(_proj_conv_kernel, tm=tm, rc=min(tm, 32)),
        grid=(b, n_l, nj),
        in_specs=[pl.BlockSpec((1, tm, d), lambda i, l, j: (i, l, 0)),
                  pl.BlockSpec((d, tn), lambda i, l, j: (0, j + col0 // tn)),
                  pl.BlockSpec((1, past, tn), lambda i, l, j: (i, 0, j)),
                  pl.BlockSpec((CONV_B, tn), lambda i, l, j: (0, j)),
                  pl.BlockSpec((1, tn), lambda i, l, j: (0, j))],
        out_specs=[pl.BlockSpec((1, tm, tn), lambda i, l, j: (i, l, j)),
                   pl.BlockSpec((1, 1, past, tn), lambda i, l, j: (i, l, 0, j))],
        out_shape=[jax.ShapeDtypeStruct((b, L, n), F32), jax.ShapeDtypeStruct((b, n_l, past, n), F32)],
        scratch_shapes=[pltpu.VMEM((SUBLANES + tm, tn), F32)] + carry,
        compiler_params=_params("parallel", "arbitrary", "arbitrary"),
        name=name,
    )(h, w, state, cw, cb)
    return out, ns[:, -1]


def _epi_glu(a, b):
    return a[0] * _sigmoid(a[1])


def _epi_dt(a, b):
    lane = lax.broadcasted_iota(jnp.int32, a[0].shape, 1)
    return jnp.where(lane < N_HEADS, _softplus(a[0] + b[0]), 0.0)


def _epi_silu(a, b):
    return _silu(a[0])


def _epi_id(a, b):
    return a[0]


def _epi_gate(a, b):
    return _sigmoid(a[0] + b[0])


def _merge_kernel(ua_ref, yb_ref, gt_ref, x_ref, wa_ref, ba_ref, wb_ref, wo_ref, gp_ref, o_ref):
    ya = _dot(ua_ref[...], wa_ref[...]) + ba_ref[...]
    yb = _dot(yb_ref[...], wb_ref[...])
    mix = gt_ref[:, :D_MODEL] * ya + gt_ref[:, D_MODEL:] * yb
    m = _dot(mix.astype(BF16), wo_ref[...])
    o_ref[...] = x_ref[...] + _rms(m, gp_ref[...])


def _merge(ua, yb, gates, x, wa, ba, wb, wo, gp):
    rows = x.shape[0]
    tm = min(rows, 256)
    row_spec = lambda w: pl.BlockSpec((tm, w), lambda i: (i, 0))
    res_spec = lambda a: pl.BlockSpec(a.shape, lambda i: (0, 0), pipeline_mode=pl.Buffered(1))
    return pl.pallas_call(
        _merge_kernel,
        grid=(rows // tm,),
        in_specs=[row_spec(D_A), row_spec(D_B), row_spec(2 * D_MODEL), row_spec(D_MODEL),
                  res_spec(wa), res_spec(ba), res_spec(wb), res_spec(wo), res_spec(gp)],
        out_specs=row_spec(D_MODEL),
        out_shape=jax.ShapeDtypeStruct((rows, D_MODEL), F32),
        compiler_params=_params("parallel"),
        name="merge",
    )(ua, yb, gates, x, wa, ba, wb, wo, gp)


def _ffn_kernel(x1_ref, gpre_ref, wg_ref, wv_ref, wd_ref, stg_ref, stv_ref, cwg_ref, cwv_ref, cbg_ref, cbv_ref,
                gpost_ref, o_ref, nsg_ref, nsv_ref, h_scr, acc_scr, act_scr, win_g, win_v, *carry,
                tm, sh, hist, rc):
    l = pl.program_id(1)
    j = pl.program_id(2)
    past = (CONV_F - 1) * sh

    @pl.when(j == 0)
    def _():
        h_scr[...] = _rms(x1_ref[0], gpre_ref[...]).astype(BF16)
        acc_scr[...] = jnp.zeros_like(acc_scr)

    for s, (st_ref, win) in enumerate(((stg_ref, win_g), (stv_ref, win_v))):
        if carry:
            @pl.when(l == 0)
            def _(win=win, st_ref=st_ref):
                win[hist - past:hist, :] = st_ref[0]

            @pl.when(l > 0)
            def _(win=win, s=s):
                win[0:hist, :] = carry[0][j, s]
        else:
            win[hist - past:hist, :] = st_ref[0]

    def conv(win, cw_ref, cb_ref, r0, cs):
        out = cb_ref[:, cs]
        for k in range(CONV_F):
            out = out + cw_ref[k:k + 1, cs] * _rows(win, hist - (CONV_F - 1 - k) * sh + r0, rc, cs)
        return out

    tf = win_g.shape[1]
    cw = min(tf, 256)
    mc = min(tm, PROJ_ROWS)
    for c0 in range(0, tf, cw):
        cs = slice(c0, c0 + cw)
        for m0 in range(0, tm, mc):
            win_g[hist + m0:hist + m0 + mc, cs] = _dot(h_scr[m0:m0 + mc, :], wg_ref[:, cs])
            win_v[hist + m0:hist + m0 + mc, cs] = _dot(h_scr[m0:m0 + mc, :], wv_ref[:, cs])
            for r0 in range(m0, m0 + mc, rc):
                gate = _gelu_tanh(conv(win_g, cwg_ref, cbg_ref, r0, cs))
                act_scr[r0:r0 + rc, cs] = (gate * conv(win_v, cwv_ref, cbv_ref, r0, cs)).astype(BF16)
    acc_scr[...] += _dot(act_scr[...], wd_ref[...])

    for s, (ns_ref, win) in enumerate(((nsg_ref, win_g), (nsv_ref, win_v))):
        ns_ref[0, 0] = win[hist + tm - past:hist + tm, :]
        if carry:
            carry[0][j, s] = win[tm:tm + hist, :]

    @pl.when(j == pl.num_programs(2) - 1)
    def _():
        o_ref[0] = x1_ref[0] + _rms(acc_scr[...], gpost_ref[...])


def _ffn(x1, state, wts, *, tm, sh):
    b, L, _ = x1.shape
    tf = 512
    nj = D_FF // tf
    past = (CONV_F - 1) * sh
    hist = -(-past // SUBLANES) * SUBLANES
    n_l = L // tm
    carry = [pltpu.VMEM((nj, 2, hist, tf), F32)] if n_l > 1 else []
    col = lambda rows, off: pl.BlockSpec((rows, tf), lambda i, l, j: (0, j + off))
    st = lambda off: pl.BlockSpec((1, past, tf), lambda i, l, j: (i, 0, j + off))
    tail = pl.BlockSpec((1, 1, past, tf), lambda i, l, j: (i, l, 0, j))
    vec = pl.BlockSpec((1, D_MODEL), lambda i, l, j: (0, 0))
    xblk = pl.BlockSpec((1, tm, D_MODEL), lambda i, l, j: (i, l, 0))
    out, ns_g, ns_v = pl.pallas_call(
        functools.partial(_ffn_kernel, tm=tm, sh=sh, hist=hist, rc=min(tm, 32)),
        grid=(b, n_l, nj),
        in_specs=[xblk, vec, col(D_MODEL, 0), col(D_MODEL, nj),
                  pl.BlockSpec((tf, D_MODEL), lambda i, l, j: (j, 0)),
                  st(0), st(nj), col(CONV_F, 0), col(CONV_F, nj), col(1, 0), col(1, nj), vec],
        out_specs=[xblk, tail, tail],
        out_shape=[jax.ShapeDtypeStruct((b, L, D_MODEL), F32),
                   jax.ShapeDtypeStruct((b, n_l, past, D_FF), F32),
                   jax.ShapeDtypeStruct((b, n_l, past, D_FF), F32)],
        scratch_shapes=[pltpu.VMEM((tm, D_MODEL), BF16), pltpu.VMEM((tm, D_MODEL), F32),
                        pltpu.VMEM((tm, tf), BF16),
                        pltpu.VMEM((hist + tm, tf), F32), pltpu.VMEM((hist + tm, tf), F32)] + carry,
        compiler_params=_params("parallel", "arbitrary", "arbitrary"),
        name="ffn",
    )(x1, wts["g_pre2"], wts["w_up"], wts["w_up"], wts["w_down"], state, state,
      wts["w_dw_f"], wts["w_dw_f"], wts["b_dw_f"], wts["b_dw_f"], wts["g_post2"])
    return out, jnp.concatenate([ns_g[:, -1], ns_v[:, -1]], axis=-1)


def _post_ln_silu(outs, extras):
    u = outs[0]
    mu = jnp.mean(u, axis=-1, keepdims=True)
    xc = u - mu
    r = lax.rsqrt(jnp.mean(xc * xc, axis=-1, keepdims=True) + EPS)
    return _silu(xc * r * extras[0] + extras[1])


def _post_silu(outs, extras):
    return _silu(outs[0])


def _conv_seq_kernel(*refs, width, tl, n_s, n_x, post, hist, rc):
    u_refs = refs[0:n_s]
    st_refs = refs[n_s:2 * n_s]
    w_refs = refs[2 * n_s:3 * n_s]
    b_refs = refs[3 * n_s:4 * n_s]
    x_refs = refs[4 * n_s:4 * n_s + n_x]
    o_ref = refs[4 * n_s + n_x]
    ns_refs = refs[4 * n_s + n_x + 1:4 * n_s + n_x + 1 + n_s]
    win_refs = refs[4 * n_s + n_x + 1 + n_s:4 * n_s + n_x + 1 + 2 * n_s]
    shf_refs = refs[4 * n_s + n_x + 1 + 2 * n_s:]
    past = width - 1
    l = pl.program_id(2)
    n_shf = hist + tl - SUBLANES

    for s in range(n_s):
        win = win_refs[s]

        @pl.when(l == 0)
        def _(win=win, s=s):
            if hist > past:
                win[0:hist - past, :] = jnp.zeros((hist - past, win.shape[1]), F32)
            win[hist - past:hist, :] = st_refs[s][0]

        @pl.when(l > 0)
        def _(win=win):
            win[0:hist, :] = win[tl:tl + hist, :]

        win[hist:hist + tl, :] = u_refs[s][0]
        for r in range(1, SUBLANES):
            shf_refs[s][r - 1, :, :] = _rows(win, r, n_shf)

    def tap(s, lo):
        a, r = divmod(lo, SUBLANES)
        if r == 0:
            return win_refs[s][lo:lo + rc, :]
        return shf_refs[s][r - 1, a * SUBLANES:a * SUBLANES + rc, :]

    extras = [x[...] for x in x_refs]
    for c in range(tl // rc):
        outs = []
        for s in range(n_s):
            acc = b_refs[s][...] + w_refs[s][0:1, :] * tap(s, c * rc + hist - past)
            for k in range(1, width):
                acc = acc + w_refs[s][k:k + 1, :] * tap(s, c * rc + hist - past + k)
            outs.append(acc)
        o_ref[0, c * rc:(c + 1) * rc, :] = post(outs, extras).astype(o_ref.dtype)

    @pl.when(l == pl.num_programs(2) - 1)
    def _():
        for s in range(n_s):
            ns_refs[s][0] = win_refs[s][hist + tl - past:hist + tl, :]


def _conv_seq(u, state, w, bias, extras, post, *, width, ct, tl, n_s, out_w, out_dtype, name):
    b, L, _ = u.shape
    past = width - 1
    hist = -(-past // SUBLANES) * SUBLANES
    nj = out_w // ct
    rc = min(tl, 64)
    kern = functools.partial(_conv_seq_kernel, width=width, tl=tl, n_s=n_s, n_x=len(extras),
                             post=post, hist=hist, rc=rc)
    in_specs = ([pl.BlockSpec((1, tl, ct), lambda i, j, l, s=s: (i, l, j + s * nj)) for s in range(n_s)]
                + [pl.BlockSpec((1, past, ct), lambda i, j, l, s=s: (i, 0, j + s * nj)) for s in range(n_s)]
                + [pl.BlockSpec((width, ct), lambda i, j, l, s=s: (0, j + s * nj)) for s in range(n_s)]
                + [pl.BlockSpec((1, ct), lambda i, j, l, s=s: (0, j + s * nj)) for s in range(n_s)]
                + [pl.BlockSpec((1, ct), lambda i, j, l: (0, j)) for _ in extras])
    out_specs = ([pl.BlockSpec((1, tl, ct), lambda i, j, l: (i, l, j))]
                 + [pl.BlockSpec((1, past, ct), lambda i, j, l: (i, 0, j)) for _ in range(n_s)])
    out_shape = ([jax.ShapeDtypeStruct((b, L, out_w), out_dtype)]
                 + [jax.ShapeDtypeStruct((b, past, out_w), F32) for _ in range(n_s)])
    res = pl.pallas_call(
        kern,
        grid=(b, nj, L // tl),
        in_specs=in_specs,
        out_specs=out_specs,
        out_shape=out_shape,
        scratch_shapes=[pltpu.VMEM((hist + tl, ct), F32) for _ in range(n_s)]
                       + [pltpu.VMEM((SUBLANES - 1, hist + tl - SUBLANES, ct), F32) for _ in range(n_s)],
        compiler_params=_params("parallel", "parallel", "arbitrary"),
        name=name,
    )(*([u] * n_s), *([state] * n_s), *([w] * n_s), *([bias] * n_s), *extras)
    return res[0], res[1:]


def _conv_slab_kernel(*refs, width, steps, n_s, n_x, post):
    u_refs = refs[0:n_s]
    st_refs = refs[n_s:2 * n_s]
    w_refs = refs[2 * n_s:3 * n_s]
    b_refs = refs[3 * n_s:4 * n_s]
    x_refs = refs[4 * n_s:4 * n_s + n_x]
    o_ref = refs[4 * n_s + n_x]
    ns_refs = refs[4 * n_s + n_x + 1:]
    past = width - 1

    def slab(s, i):
        return st_refs[s][i] if i < past else u_refs[s][i - past]

    extras = [x[...] for x in x_refs]
    for t in range(steps):
        outs = []
        for s in range(n_s):
            acc = b_refs[s][...] + w_refs[s][0:1, :] * slab(s, t)
            for k in range(1, width):
                acc = acc + w_refs[s][k:k + 1, :] * slab(s, t + k)
            outs.append(acc)
        o_ref[t] = post(outs, extras).astype(o_ref.dtype)
    for s in range(n_s):
        for i in range(past):
            ns_refs[s][i] = slab(s, i + steps)


def _conv_slab(u, state, w, bias, extras, post, *, width, ct, nbt, n_s, out_w, out_dtype, name):
    steps, nb, _ = u.shape
    past = width - 1
    nj = out_w // ct
    kern = functools.partial(_conv_slab_kernel, width=width, steps=steps, n_s=n_s, n_x=len(extras), post=post)
    in_specs = ([pl.BlockSpec((steps, nbt, ct), lambda i, j, s=s: (0, i, j + s * nj)) for s in range(n_s)]
                + [pl.BlockSpec((past, nbt, ct), lambda i, j, s=s: (0, i, j + s * nj)) for s in range(n_s)]
                + [pl.BlockSpec((width, ct), lambda i, j, s=s: (0, j + s * nj)) for s in range(n_s)]
                + [pl.BlockSpec((1, ct), lambda i, j, s=s: (0, j + s * nj)) for s in range(n_s)]
                + [pl.BlockSpec((1, ct), lambda i, j: (0, j)) for _ in extras])
    out_specs = ([pl.BlockSpec((steps, nbt, ct), lambda i, j: (0, i, j))]
                 + [pl.BlockSpec((past, nbt, ct), lambda i, j: (0, i, j)) for _ in range(n_s)])
    out_shape = ([jax.ShapeDtypeStruct((steps, nb, out_w), out_dtype)]
                 + [jax.ShapeDtypeStruct((past, nb, out_w), F32) for _ in range(n_s)])
    res = pl.pallas_call(
        kern,
        grid=(nb // nbt, nj),
        in_specs=in_specs,
        out_specs=out_specs,
        out_shape=out_shape,
        compiler_params=_params("parallel", "parallel"),
        name=name,
    )(*([u] * n_s), *([state] * n_s), *([w] * n_s), *([bias] * n_s), *extras)
    return res[0], res[1:]


def _ssd_kernel(xbc_ref, dt_ref, zs_ref, h0_ref, alog_ref, dsk_ref, gn_ref, e_ref, y_ref, ht_ref, h_scr, y_scr,
                *, q, bb):
    qk = SSD_KEYS
    c = pl.program_id(1)

    @pl.when(c == 0)
    def _():
        h_scr[...] = h0_ref[...]

    a = -jnp.exp(alog_ref[...])
    e = e_ref[...]
    row = lax.broadcasted_iota(jnp.int32, (qk, qk), 0)
    col = lax.broadcasted_iota(jnp.int32, (qk, qk), 1)
    tril = jnp.where(row >= col, 1.0, 0.0).astype(BF16)
    causal = row[:q, :] >= col[:q, :]
    key_head = lax.broadcasted_iota(jnp.int32, (qk, GROUP_W), 1) // HEAD_DIM

    def expand(vs):
        v = jnp.concatenate(vs, axis=0) if len(vs) > 1 else vs[0]
        hi = v.astype(BF16)
        lo = (v - hi.astype(F32)).astype(BF16)
        out = _dot(hi, e) + _dot(lo, e)
        n = vs[0].shape[0]
        return [out[i * n:(i + 1) * n, :] for i in range(len(vs))]

    def pad_keys(v):
        if q == qk:
            return v
        return jnp.concatenate([v, jnp.zeros((qk - q, v.shape[1]), v.dtype)], axis=0)

    seqs = range(bb)
    xqs = [xbc_ref[s] for s in seqs]
    xs = [xq[:, :D_B] for xq in xqs]
    bms = [pad_keys(xq[:, D_B:D_B + N_GROUPS * D_STATE]).astype(BF16) for xq in xqs]
    cms = [xq[:, D_B + N_GROUPS * D_STATE:].astype(BF16) for xq in xqs]
    dts = [dt_ref[s] for s in seqs]
    cums = [sum(_dot(tril, p) for p in _split_bf16(pad_keys(dt * a), 3)) for dt in dts]
    cum_ts = [cum.T for cum in cums]
    cum_qs = [cum[:q, :] for cum in cums]
    xdts = [x * d for x, d in zip(xs, expand(dts))]
    chunk_decay = [jnp.exp(cum[qk - 1:qk, :]) for cum in cums]
    to_end = [jnp.exp(cum[qk - 1:qk, :] - cum_q) for cum, cum_q in zip(cums, cum_qs)]
    wide = expand(to_end + [jnp.exp(cum_q) for cum_q in cum_qs])
    xws = [pad_keys(xdt * w).astype(BF16) for xdt, w in zip(xdts, wide[:bb])]
    xdts = [pad_keys(xdt).astype(BF16) for xdt in xdts]
    ecums = wide[bb:]

    for g in range(N_GROUPS):
        cols = slice(g * GROUP_W, (g + 1) * GROUP_W)
        for s in seqs:
            bg = bms[s][:, g * D_STATE:(g + 1) * D_STATE]
            cg = cms[s][:, g * D_STATE:(g + 1) * D_STATE]
            cb = lax.dot_general(cg, bg, (((1,), (1,)), ((), ())), preferred_element_type=F32)
            xdt_g = xdts[s][:, cols]
            scores, keys = [], []
            for r in range(HEADS_PER_GROUP):
                hd = g * HEADS_PER_GROUP + r
                seg = jnp.where(causal, cum_qs[s][:, hd:hd + 1] - cum_ts[s][hd:hd + 1, :], -1e30)
                scores.append((cb * jnp.exp(seg)).astype(BF16))
                keys.append(jnp.where(key_head == r, xdt_g, jnp.zeros_like(xdt_g)))
            y_g = _dot(jnp.concatenate(scores, axis=1), jnp.concatenate(keys, axis=0))
            h_g = h_scr[s, cols, :]
            y_inter = lax.dot_general(cg, h_g.astype(BF16), (((1,), (1,)), ((), ())),
                                      preferred_element_type=F32)
            y_scr[s, :, cols] = y_g + y_inter * ecums[s][:, cols]
            s_g = lax.dot_general(xws[s][:, cols], bg, (((0,), (0,)), ((), ())), preferred_element_type=F32)
            decay = jnp.concatenate(
                [jnp.broadcast_to(chunk_decay[s][:, g * HEADS_PER_GROUP + r:g * HEADS_PER_GROUP + r + 1],
                                  (HEAD_DIM, D_STATE)) for r in range(HEADS_PER_GROUP)], axis=0)
            h_scr[s, cols, :] = decay * h_g + s_g

    for s in seqs:
        y = y_scr[s] + dsk_ref[...] * xs[s]
        y_ref[s] = _rms(y * zs_ref[s], gn_ref[...]).astype(y_ref.dtype)

    @pl.when(c == pl.num_programs(1) - 1)
    def _():
        ht_ref[...] = h_scr[...]


def _ssd(xbc, dt, zs, h0, alog, dsk, gn, e, *, q, bb):
    b, L, _ = xbc.shape
    seq = lambda w: pl.BlockSpec((bb, q, w), lambda i, c: (i, c, 0))
    full = lambda arr: pl.BlockSpec(arr.shape, lambda i, c: (0,) * arr.ndim)
    state = pl.BlockSpec((bb, D_B, D_STATE), lambda i, c: (i, 0, 0))
    return pl.pallas_call(
        functools.partial(_ssd_kernel, q=q, bb=bb),
        grid=(b // bb, L // q),
        in_specs=[seq(D_XBC), seq(LANES), seq(D_B), state, full(alog), full(dsk), full(gn), full(e)],
        out_specs=[seq(D_B), state],
        out_shape=[jax.ShapeDtypeStruct((b, L, D_B), BF16), jax.ShapeDtypeStruct((b, D_B, D_STATE), F32)],
        scratch_shapes=[pltpu.VMEM((bb, D_B, D_STATE), F32), pltpu.VMEM((bb, q, D_B), F32)],
        compiler_params=_params("parallel", "arbitrary"),
        name="ssd",
    )(xbc, dt, zs, h0, alog, dsk, gn, e)


def _layer(x, wts, conv_a, xbc_fn, ssd_fn, ffn_fn):
    h = _norm(x, wts["g_pre1"])
    w_main = wts["w_main"]
    uglu = _proj(h, [w_main, w_main], [], _epi_glu, PROJ_COLS, F32, "in_glu", n=D_A, cols=[0, D_A])
    dt = _proj(h, [wts["w_dt"]], [wts["dt_bias"]], _epi_dt, LANES, F32, "in_dt")
    zs = _proj(h, [w_main], [], _epi_silu, PROJ_COLS, F32, "in_z", n=D_B, cols=[COL_Z])
    gates = _proj(h, [wts["w_g"]], [wts["b_gate"]], _epi_gate, PROJ_COLS, F32, "in_gates")

    ua, st_a = conv_a(uglu)
    xbc_act, st_b = xbc_fn(h)
    yb, st_h = ssd_fn(xbc_act, dt, zs)

    x1 = _merge(ua, yb, gates, x, wts["w_a_out"], wts["b_a_out"], wts["w_b_out"], wts["w_o"], wts["g_post1"])
    x2, st_f = ffn_fn(x1)
    return x2, st_a, st_b, st_h, st_f


def _seq_group(x, wts, st_a, st_b, st_h, st_f, tl, q):
    b, L, _ = x.shape

    def conv_a(uglu):
        ua, (ns,) = _conv_seq(uglu.reshape(b, L, D_A), st_a, wts["w_dw_a"], wts["b_dw_a"],
                              [wts["g_ln_a"], wts["b_ln_a"]], _post_ln_silu, width=CONV_A, ct=D_A,
                              tl=min(tl, 256), n_s=1, out_w=D_A, out_dtype=BF16, name="conv_a")
        return ua.reshape(b * L, D_A), ns

    def xbc_fn(h):
        return _proj_conv(h.reshape(b, L, D_MODEL), wts["w_main"], COL_XBC, st_b, wts["w_dw_b"], wts["b_dw_b"],
                          "in_xbc_conv")

    def ssd_fn(xbc_act, dt, zs):
        yb, ht = _ssd(xbc_act, dt.reshape(b, L, LANES), zs.reshape(b, L, D_B), st_h,
                      wts["a_log"], wts["d_skip"], wts["g_norm_b"], wts["expand"], q=q, bb=1)
        return yb.reshape(b * L, D_B), ht

    def ffn_fn(x1):
        x2, ns = _ffn(x1.reshape(b, L, D_MODEL), st_f, wts, tm=min(L, 512), sh=1)
        return x2.reshape(b * L, D_MODEL), ns

    x2, ns_a, ns_b, ns_h, ns_f = _layer(x.reshape(b * L, D_MODEL), wts, conv_a, xbc_fn, ssd_fn, ffn_fn)
    return x2.reshape(b, L, D_MODEL), ns_a, ns_b, ns_h, ns_f


def _step_group(x, wts, st_a, st_b, st_h, st_f):
    nb, L, _ = x.shape
    tmaj = lambda s: jnp.transpose(s, (1, 0, 2))
    lpad = -(-L // SUBLANES) * SUBLANES

    def conv_a(uglu):
        ua, (ns,) = _conv_slab(uglu.reshape(L, nb, D_A), tmaj(st_a), wts["w_dw_a"], wts["b_dw_a"],
                               [wts["g_ln_a"], wts["b_ln_a"]], _post_ln_silu, width=CONV_A, ct=D_A,
                               nbt=32, n_s=1, out_w=D_A, out_dtype=BF16, name="conv_a_step")
        return ua.reshape(L * nb, D_A), tmaj(ns)

    def bmaj_pad(v):
        return jnp.pad(tmaj(v), ((0, 0), (0, lpad - L), (0, 0)))

    def xbc_fn(h):
        xbc = _proj(h, [wts["w_main"]], [], _epi_id, PROJ_COLS, F32, "in_xbc", n=D_XBC, cols=[COL_XBC])
        act, (ns,) = _conv_slab(xbc.reshape(L, nb, D_XBC), tmaj(st_b), wts["w_dw_b"], wts["b_dw_b"], [],
                                _post_silu, width=CONV_B, ct=512, nbt=nb, n_s=1, out_w=D_XBC,
                                out_dtype=F32, name="conv_b_step")
        return act, tmaj(ns)

    def ssd_fn(xbc_act, dt, zs):
        yb, ht = _ssd(bmaj_pad(xbc_act), bmaj_pad(dt.reshape(L, nb, LANES)), bmaj_pad(zs.reshape(L, nb, D_B)),
                      st_h, wts["a_log"], wts["d_skip"], wts["g_norm_b"], wts["expand"], q=lpad, bb=4)
        return tmaj(yb[:, :L]).reshape(L * nb, D_B), ht

    def ffn_fn(x1):
        past = CONV_F - 1
        x2, ns = _ffn(x1.reshape(1, L * nb, D_MODEL), tmaj(st_f).reshape(1, past * nb, 2 * D_FF), wts,
                      tm=L * nb, sh=nb)
        return x2.reshape(L * nb, D_MODEL), tmaj(ns.reshape(past, nb, 2 * D_FF))

    x2, ns_a, ns_b, ns_h, ns_f = _layer(tmaj(x).reshape(L * nb, D_MODEL), wts, conv_a, xbc_fn, ssd_fn, ffn_fn)
    return tmaj(x2.reshape(L, nb, D_MODEL)), ns_a, ns_b, ns_h, ns_f


def _layer_weights(l, g_pre1, g_post1, w_in, b_gate, w_dw_a, b_dw_a, g_ln_a, b_ln_a, w_a_out, b_a_out,
                   w_dw_b, b_dw_b, dt_bias, a_log, d_skip, g_norm_b, w_b_out, w_o,
                   g_pre2, g_post2, w_up, w_dw_f, b_dw_f, w_down):
    row = lambda v: v[l].reshape(1, -1)
    lane_pad = lambda v: jnp.pad(v, ((0, 0), (0, LANES - v.shape[1])))
    w = w_in[l]
    head = jnp.arange(LANES, dtype=jnp.int32)[:, None]
    chan_head = (jnp.arange(D_B, dtype=jnp.int32) // HEAD_DIM)[None, :]
    expand = (head == chan_head).astype(BF16)
    return {
        "g_pre1": row(g_pre1), "g_post1": row(g_post1), "g_pre2": row(g_pre2), "g_post2": row(g_post2),
        "w_main": w[:, :COL_DT].astype(BF16),
        "w_dt": lane_pad(w[:, COL_DT:COL_GATE]).astype(BF16), "w_g": w[:, COL_GATE:].astype(BF16),
        "dt_bias": lane_pad(row(dt_bias)), "b_gate": row(b_gate),
        "w_dw_a": w_dw_a[l], "b_dw_a": row(b_dw_a), "g_ln_a": row(g_ln_a), "b_ln_a": row(b_ln_a),
        "w_a_out": w_a_out[l].astype(BF16), "b_a_out": row(b_a_out),
        "w_dw_b": w_dw_b[l], "b_dw_b": row(b_dw_b),
        "a_log": lane_pad(row(a_log)), "d_skip": jnp.repeat(d_skip[l], HEAD_DIM).reshape(1, D_B),
        "g_norm_b": row(g_norm_b), "expand": expand,
        "w_b_out": w_b_out[l].astype(BF16), "w_o": w_o[l].astype(BF16),
        "w_up": w_up[l].astype(BF16), "w_dw_f": w_dw_f[l], "b_dw_f": row(b_dw_f),
        "w_down": w_down[l].astype(BF16),
    }


def kernel(x_prompt, x_sample, state_conv_a, state_conv_b, state_ssm, state_conv_ffn, meta_tokens, g_pre1, g_post1, w_in, b_gate, w_dw_a, b_dw_a, g_ln_a, b_ln_a, w_a_out, b_a_out, w_dw_b, b_dw_b, dt_bias, a_log, d_skip, g_norm_b, w_b_out, w_o, g_pre2, g_post2, w_up, w_dw_f, b_dw_f, w_down):
    depth = w_in.shape[0]
    bp = x_prompt.shape[0]
    nb = x_sample.shape[0]
    xm = meta_tokens.astype(x_prompt.dtype)[None]
    xp, xs = x_prompt, x_sample
    pa, pb, ph, pf = [], [], [], []
    sa, sb, sh, sf = [], [], [], []
    for l in range(depth):
        wts = _layer_weights(l, g_pre1, g_post1, w_in, b_gate, w_dw_a, b_dw_a, g_ln_a, b_ln_a, w_a_out, b_a_out,
                             w_dw_b, b_dw_b, dt_bias, a_log, d_skip, g_norm_b, w_b_out, w_o,
                             g_pre2, g_post2, w_up, w_dw_f, b_dw_f, w_down)
        xm, m_a, m_b, m_h, m_f = _seq_group(
            xm, wts, jnp.zeros((1, CONV_A - 1, D_A), F32), jnp.zeros((1, CONV_B - 1, D_XBC), F32),
            jnp.zeros((1, D_B, D_STATE), F32), jnp.zeros((1, CONV_F - 1, 2 * D_FF), F32), tl=N_META, q=SUBLANES)
        rep = lambda s: jnp.broadcast_to(s, (bp,) + s.shape[1:])
        xp, c_a, c_b, c_h, c_f = _seq_group(xp, wts, rep(m_a), rep(m_b), rep(m_h), rep(m_f), tl=512, q=SSD_KEYS)
        pa.append(c_a); pb.append(c_b); ph.append(c_h.reshape(bp, N_HEADS, HEAD_DIM, D_STATE)); pf.append(c_f)
        xs, d_a, d_b, d_h, d_f = _step_group(
            xs, wts, state_conv_a[l], state_conv_b[l], state_ssm[l].reshape(nb, D_B, D_STATE), state_conv_ffn[l])
        sa.append(d_a); sb.append(d_b); sh.append(d_h.reshape(nb, N_HEADS, HEAD_DIM, D_STATE)); sf.append(d_f)
    return (xp, xs, jnp.stack(pa), jnp.stack(pb), jnp.stack(ph), jnp.stack(pf),
            jnp.stack(sa), jnp.stack(sb), jnp.stack(sh), jnp.stack(sf))
```

```python
import functools

import jax
import jax.numpy as jnp
from jax import lax
from jax.experimental import pallas as pl
from jax.experimental.pallas import tpu as pltpu

D_MODEL = 2048
N_META = 16
D_A = 1024
CONV_A = 31
D_B = 2048
HEAD_DIM = 64
N_HEADS = D_B // HEAD_DIM
N_GROUPS = 8
HEADS_PER_GROUP = N_HEADS // N_GROUPS
GROUP_W = HEADS_PER_GROUP * HEAD_DIM
D_STATE = 128
CONV_B = 4
D_FF = 5632
CONV_F = 3
EPS = 1e-6
D_XBC = D_B + 2 * N_GROUPS * D_STATE
COL_Z = 2 * D_A
COL_XBC = COL_Z + D_B
COL_DT = COL_XBC + D_XBC
COL_GATE = COL_DT + N_HEADS

LANES = 128
SUBLANES = 8
SSD_KEYS = 128
PROJ_ROWS = 256
PROJ_COLS = 1024
VMEM_LIMIT = 56 * 1024 * 1024

F32 = jnp.float32
BF16 = jnp.bfloat16


def _params(*sem):
    return pltpu.CompilerParams(dimension_semantics=sem, vmem_limit_bytes=VMEM_LIMIT)


def _sigmoid(x):
    return 1.0 / (1.0 + jnp.exp(-x))


def _silu(x):
    return x * _sigmoid(x)


def _softplus(x):
    return jnp.maximum(x, 0.0) + jnp.log1p(jnp.exp(-jnp.abs(x)))


def _gelu_tanh(x):
    return 0.5 * x * (1.0 + jnp.tanh(0.7978845608028654 * (x + 0.044715 * (x * x * x))))


def _rms(x, g):
    r = lax.rsqrt(jnp.mean(x * x, axis=-1, keepdims=True) + EPS)
    return x * r * g


def _dot(a, b):
    return jnp.dot(a, b, preferred_element_type=F32)


def _rows(ref, lo, n, cols=slice(None)):
    base, off = divmod(lo, SUBLANES)
    if off == 0:
        return ref[lo:lo + n, cols]
    blk = ref[base * SUBLANES:base * SUBLANES + n + SUBLANES, cols]
    return pltpu.roll(blk, n + SUBLANES - off, 0)[0:n, :]


def _split_bf16(v, parts):
    out = []
    for _ in range(parts):
        p = v.astype(BF16)
        out.append(p)
        v = v - p.astype(F32)
    return out


def _norm_kernel(x_ref, g_ref, wdt_ref, bdt_ref, h_ref, dt_ref):
    h = _rms(x_ref[...], g_ref[...]).astype(BF16)
    h_ref[...] = h
    dt_ref[...] = _epi_dt([_dot(h, wdt_ref[...])], [bdt_ref[...]])


def _norm(x, g, w_dt, dt_bias):
    rows, d = x.shape
    tm = min(rows, 512)
    const = lambda a: pl.BlockSpec(a.shape, lambda i: (0, 0))
    return pl.pallas_call(
        _norm_kernel,
        grid=(rows // tm,),
        in_specs=[pl.BlockSpec((tm, d), lambda i: (i, 0)), const(g), const(w_dt), const(dt_bias)],
        out_specs=[pl.BlockSpec((tm, d), lambda i: (i, 0)), pl.BlockSpec((tm, LANES), lambda i: (i, 0))],
        out_shape=[jax.ShapeDtypeStruct((rows, d), BF16), jax.ShapeDtypeStruct((rows, LANES), F32)],
        compiler_params=_params("parallel"),
        name="norm_dt",
    )(x, g, w_dt, dt_bias)


def _proj_kernel(*refs, n_w, n_b, epilogue):
    h_ref = refs[0]
    w_refs = refs[1:1 + n_w]
    b_refs = refs[1 + n_w:1 + n_w + n_b]
    o_ref = refs[-1]
    bs = [b[...] for b in b_refs]
    rows = h_ref.shape[0]
    mc = min(rows, PROJ_ROWS)
    for r0 in range(0, rows, mc):
        h = h_ref[r0:r0 + mc, :]
        accs = [_dot(h, w[...]) for w in w_refs]
        o_ref[r0:r0 + mc, :] = epilogue(accs, bs).astype(o_ref.dtype)


def _proj(h, ws, bs, epilogue, tn, out_dtype, name, n=None, cols=None):
    rows, d = h.shape
    n = n or ws[0].shape[1]
    cols = cols or [0] * len(ws)
    tm = min(rows, 1024)
    kern = functools.partial(_proj_kernel, n_w=len(ws), n_b=len(bs), epilogue=epilogue)
    return pl.pallas_call(
        kern,
        grid=(rows // tm, n // tn),
        in_specs=[pl.BlockSpec((tm, d), lambda i, j: (i, 0))]
                 + [pl.BlockSpec((d, tn), lambda i, j, c=c // tn: (0, j + c)) for c in cols]
                 + [pl.BlockSpec((1, tn), lambda i, j: (0, j)) for _ in bs],
        out_specs=pl.BlockSpec((tm, tn), lambda i, j: (i, j)),
        out_shape=jax.ShapeDtypeStruct((rows, n), out_dtype),
        compiler_params=_params("parallel", "parallel"),
        name=name,
    )(h, *ws, *bs)


def _proj_conv_kernel(h_ref, w_ref, st_ref, cw_ref, cb_ref, o_ref, ns_ref, win, *carry, tm, rc):
    l = pl.program_id(1)
    j = pl.program_id(2)
    past = CONV_B - 1
    hist = SUBLANES
    if carry:
        @pl.when(l == 0)
        def _():
            win[hist - past:hist, :] = st_ref[0]

        @pl.when(l > 0)
        def _():
            win[0:hist, :] = carry[0][j]
    else:
        win[hist - past:hist, :] = st_ref[0]

    mc = min(tm, PROJ_ROWS)
    for m0 in range(0, tm, mc):
        win[hist + m0:hist + m0 + mc, :] = _dot(h_ref[0, m0:m0 + mc, :], w_ref[...])
        for r0 in range(m0, m0 + mc, rc):
            pre = cb_ref[...]
            for k in range(CONV_B):
                pre = pre + cw_ref[k:k + 1, :] * _rows(win, r0 + hist - past + k, rc)
            o_ref[0, r0:r0 + rc, :] = _silu(pre)

    ns_ref[0, 0] = win[hist + tm - past:hist + tm, :]
    if carry:
        carry[0][j] = win[tm:tm + hist, :]


def _proj_conv(h, w, col0, state, cw, cb, name):
    b, L, d = h.shape
    n = state.shape[-1]
    past = CONV_B - 1
    tn = PROJ_COLS
    tm = min(L, 1024)
    n_l, nj = L // tm, n // tn
    carry = [pltpu.VMEM((nj, SUBLANES, tn), F32)] if n_l > 1 else []
    out, ns = pl.pallas_call(
        functools.partial(_proj_conv_kernel, tm=tm, rc=min(tm, 32)),
        grid=(b, n_l, nj),
        in_specs=[pl.BlockSpec((1, tm, d), lambda i, l, j: (i, l, 0)),
                  pl.BlockSpec((d, tn), lambda i, l, j: (0, j + col0 // tn)),
                  pl.BlockSpec((1, past, tn), lambda i, l, j: (i, 0, j)),
                  pl.BlockSpec((CONV_B, tn), lambda i, l, j: (0, j)),
                  pl.BlockSpec((1, tn), lambda i, l, j: (0, j))],
        out_specs=[pl.BlockSpec((1, tm, tn), lambda i, l, j: (i, l, j)),
                   pl.BlockSpec((1, 1, past, tn), lambda i, l, j: (i, l, 0, j))],
        out_shape=[jax.ShapeDtypeStruct((b, L, n), F32), jax.ShapeDtypeStruct((b, n_l, past, n), F32)],
        scratch_shapes=[pltpu.VMEM((SUBLANES + tm, tn), F32)] + carry,
        compiler_params=_params("parallel", "arbitrary", "arbitrary"),
        name=name,
    )(h, w, state, cw, cb)
    return out, ns[:, -1]


def _epi_glu(a, b):
    return a[0] * _sigmoid(a[1])


def _epi_dt(a, b):
    lane = lax.broadcasted_iota(jnp.int32, a[0].shape, 1)
    return jnp.where(lane < N_HEADS, _softplus(a[0] + b[0]), 0.0)


def _epi_silu(a, b):
    return _silu(a[0])


def _epi_id(a, b):
    return a[0]


def _epi_gate(a, b):
    return _sigmoid(a[0] + b[0])


def _merge_kernel(ua_ref, yb_ref, gt_ref, x_ref, wa_ref, ba_ref, wb_ref, wo_ref, gp_ref, o_ref):
    ya = _dot(ua_ref[...], wa_ref[...]) + ba_ref[...]
    yb = _dot(yb_ref[...], wb_ref[...])
    mix = gt_ref[:, :D_MODEL] * ya + gt_ref[:, D_MODEL:] * yb
    m = _dot(mix.astype(BF16), wo_ref[...])
    o_ref[...] = x_ref[...] + _rms(m, gp_ref[...])


def _merge(ua, yb, gates, x, wa, ba, wb, wo, gp):
    rows = x.shape[0]
    tm = min(rows, 256)
    row_spec = lambda w: pl.BlockSpec((tm, w), lambda i: (i, 0))
    res_spec = lambda a: pl.BlockSpec(a.shape, lambda i: (0, 0), pipeline_mode=pl.Buffered(1))
    return pl.pallas_call(
        _merge_kernel,
        grid=(rows // tm,),
        in_specs=[row_spec(D_A), row_spec(D_B), row_spec(2 * D_MODEL), row_spec(D_MODEL),
                  res_spec(wa), res_spec(ba), res_spec(wb), res_spec(wo), res_spec(gp)],
        out_specs=row_spec(D_MODEL),
        out_shape=jax.ShapeDtypeStruct((rows, D_MODEL), F32),
        compiler_params=_params("parallel"),
        name="merge",
    )(ua, yb, gates, x, wa, ba, wb, wo, gp)


def _ffn_kernel(x1_ref, gpre_ref, wg_ref, wv_ref, wd_ref, stg_ref, stv_ref, cwg_ref, cwv_ref, cbg_ref, cbv_ref,
                gpost_ref, o_ref, nsg_ref, nsv_ref, h_scr, acc_scr, act_scr, win_g, win_v, *carry,
                tm, sh, hist, rc):
    l = pl.program_id(1)
    j = pl.program_id(2)
    past = (CONV_F - 1) * sh

    @pl.when(j == 0)
    def _():
        h_scr[...] = _rms(x1_ref[0], gpre_ref[...]).astype(BF16)
        acc_scr[...] = jnp.zeros_like(acc_scr)

    for s, (st_ref, win) in enumerate(((stg_ref, win_g), (stv_ref, win_v))):
        if carry:
            @pl.when(l == 0)
            def _(win=win, st_ref=st_ref):
                win[hist - past:hist, :] = st_ref[0]

            @pl.when(l > 0)
            def _(win=win, s=s):
                win[0:hist, :] = carry[0][j, s]
        else:
            win[hist - past:hist, :] = st_ref[0]

    def conv(win, cw_ref, cb_ref, r0, cs):
        out = cb_ref[:, cs]
        for k in range(CONV_F):
            out = out + cw_ref[k:k + 1, cs] * _rows(win, hist - (CONV_F - 1 - k) * sh + r0, rc, cs)
        return out

    tf = win_g.shape[1]
    cw = min(tf, 256)
    mc = min(tm, PROJ_ROWS)
    for c0 in range(0, tf, cw):
        cs = slice(c0, c0 + cw)
        for m0 in range(0, tm, mc):
            win_g[hist + m0:hist + m0 + mc, cs] = _dot(h_scr[m0:m0 + mc, :], wg_ref[:, cs])
            win_v[hist + m0:hist + m0 + mc, cs] = _dot(h_scr[m0:m0 + mc, :], wv_ref[:, cs])
            for r0 in range(m0, m0 + mc, rc):
                gate = _gelu_tanh(conv(win_g, cwg_ref, cbg_ref, r0, cs))
                act_scr[r0:r0 + rc, cs] = (gate * conv(win_v, cwv_ref, cbv_ref, r0, cs)).astype(BF16)
    acc_scr[...] += _dot(act_scr[...], wd_ref[...])

    for s, (ns_ref, win) in enumerate(((nsg_ref, win_g), (nsv_ref, win_v))):
        ns_ref[0, 0] = win[hist + tm - past:hist + tm, :]
        if carry:
            carry[0][j, s] = win[tm:tm + hist, :]

    @pl.when(j == pl.num_programs(2) - 1)
    def _():
        o_ref[0] = x1_ref[0] + _rms(acc_scr[...], gpost_ref[...])


def _ffn(x1, state, wts, *, tm, sh):
    b, L, _ = x1.shape
    tf = 512
    nj = D_FF // tf
    past = (CONV_F - 1) * sh
    hist = -(-past // SUBLANES) * SUBLANES
    n_l = L // tm
    carry = [pltpu.VMEM((nj, 2, hist, tf), F32)] if n_l > 1 else []
    col = lambda rows, off: pl.BlockSpec((rows, tf), lambda i, l, j: (0, j + off))
    st = lambda off: pl.BlockSpec((1, past, tf), lambda i, l, j: (i, 0, j + off))
    tail = pl.BlockSpec((1, 1, past, tf), lambda i, l, j: (i, l, 0, j))
    vec = pl.BlockSpec((1, D_MODEL), lambda i, l, j: (0, 0))
    xblk = pl.BlockSpec((1, tm, D_MODEL), lambda i, l, j: (i, l, 0))
    out, ns_g, ns_v = pl.pallas_call(
        functools.partial(_ffn_kernel, tm=tm, sh=sh, hist=hist, rc=min(tm, 32)),
        grid=(b, n_l, nj),
        in_specs=[xblk, vec, col(D_MODEL, 0), col(D_MODEL, nj),
                  pl.BlockSpec((tf, D_MODEL), lambda i, l, j: (j, 0)),
                  st(0), st(nj), col(CONV_F, 0), col(CONV_F, nj), col(1, 0), col(1, nj), vec],
        out_specs=[xblk, tail, tail],
        out_shape=[jax.ShapeDtypeStruct((b, L, D_MODEL), F32),
                   jax.ShapeDtypeStruct((b, n_l, past, D_FF), F32),
                   jax.ShapeDtypeStruct((b, n_l, past, D_FF), F32)],
        scratch_shapes=[pltpu.VMEM((tm, D_MODEL), BF16), pltpu.VMEM((tm, D_MODEL), F32),
                        pltpu.VMEM((tm, tf), BF16),
                        pltpu.VMEM((hist + tm, tf), F32), pltpu.VMEM((hist + tm, tf), F32)] + carry,
        compiler_params=_params("parallel", "arbitrary", "arbitrary"),
        name="ffn",
    )(x1, wts["g_pre2"], wts["w_up"], wts["w_up"], wts["w_down"], state, state,
      wts["w_dw_f"], wts["w_dw_f"], wts["b_dw_f"], wts["b_dw_f"], wts["g_post2"])
    return out, jnp.concatenate([ns_g[:, -1], ns_v[:, -1]], axis=-1)


def _post_ln_silu(outs, extras):
    u = outs[0]
    mu = jnp.mean(u, axis=-1, keepdims=True)
    xc = u - mu
    r = lax.rsqrt(jnp.mean(xc * xc, axis=-1, keepdims=True) + EPS)
    return _silu(xc * r * extras[0] + extras[1])


def _post_silu(outs, extras):
    return _silu(outs[0])


def _conv_seq_kernel(*refs, width, tl, n_s, n_x, post, hist, rc):
    u_refs = refs[0:n_s]
    st_refs = refs[n_s:2 * n_s]
    w_refs = refs[2 * n_s:3 * n_s]
    b_refs = refs[3 * n_s:4 * n_s]
    x_refs = refs[4 * n_s:4 * n_s + n_x]
    o_ref = refs[4 * n_s + n_x]
    ns_refs = refs[4 * n_s + n_x + 1:4 * n_s + n_x + 1 + n_s]
    win_refs = refs[4 * n_s + n_x + 1 + n_s:4 * n_s + n_x + 1 + 2 * n_s]
    shf_refs = refs[4 * n_s + n_x + 1 + 2 * n_s:]
    past = width - 1
    l = pl.program_id(2)
    n_shf = hist + tl - SUBLANES

    for s in range(n_s):
        win = win_refs[s]

        @pl.when(l == 0)
        def _(win=win, s=s):
            if hist > past:
                win[0:hist - past, :] = jnp.zeros((hist - past, win.shape[1]), F32)
            win[hist - past:hist, :] = st_refs[s][0]

        @pl.when(l > 0)
        def _(win=win):
            win[0:hist, :] = win[tl:tl + hist, :]

        win[hist:hist + tl, :] = u_refs[s][0]
        for r in range(1, SUBLANES):
            shf_refs[s][r - 1, :, :] = _rows(win, r, n_shf)

    def tap(s, lo):
        a, r = divmod(lo, SUBLANES)
        if r == 0:
            return win_refs[s][lo:lo + rc, :]
        return shf_refs[s][r - 1, a * SUBLANES:a * SUBLANES + rc, :]

    extras = [x[...] for x in x_refs]
    for c in range(tl // rc):
        outs = []
        for s in range(n_s):
            acc = b_refs[s][...] + w_refs[s][0:1, :] * tap(s, c * rc + hist - past)
            for k in range(1, width):
                acc = acc + w_refs[s][k:k + 1, :] * tap(s, c * rc + hist - past + k)
            outs.append(acc)
        o_ref[0, c * rc:(c + 1) * rc, :] = post(outs, extras).astype(o_ref.dtype)

    @pl.when(l == pl.num_programs(2) - 1)
    def _():
        for s in range(n_s):
            ns_refs[s][0] = win_refs[s][hist + tl - past:hist + tl, :]


def _conv_seq(u, state, w, bias, extras, post, *, width, ct, tl, n_s, out_w, out_dtype, name):
    b, L, _ = u.shape
    past = width - 1
    hist = -(-past // SUBLANES) * SUBLANES
    nj = out_w // ct
    rc = min(tl, 64)
    kern = functools.partial(_conv_seq_kernel, width=width, tl=tl, n_s=n_s, n_x=len(extras),
                             post=post, hist=hist, rc=rc)
    in_specs = ([pl.BlockSpec((1, tl, ct), lambda i, j, l, s=s: (i, l, j + s * nj)) for s in range(n_s)]
                + [pl.BlockSpec((1, past, ct), lambda i, j, l, s=s: (i, 0, j + s * nj)) for s in range(n_s)]
                + [pl.BlockSpec((width, ct), lambda i, j, l, s=s: (0, j + s * nj)) for s in range(n_s)]
                + [pl.BlockSpec((1, ct), lambda i, j, l, s=s: (0, j + s * nj)) for s in range(n_s)]
                + [pl.BlockSpec((1, ct), lambda i, j, l: (0, j)) for _ in extras])
    out_specs = ([pl.BlockSpec((1, tl, ct), lambda i, j, l: (i, l, j))]
                 + [pl.BlockSpec((1, past, ct), lambda i, j, l: (i, 0, j)) for _ in range(n_s)])
    out_shape = ([jax.ShapeDtypeStruct((b, L, out_w), out_dtype)]
                 + [jax.ShapeDtypeStruct((b, past, out_w), F32) for _ in range(n_s)])
    res = pl.pallas_call(
        kern,
        grid=(b, nj, L // tl),
        in_specs=in_specs,
        out_specs=out_specs,
        out_shape=out_shape,
        scratch_shapes=[pltpu.VMEM((hist + tl, ct), F32) for _ in range(n_s)]
                       + [pltpu.VMEM((SUBLANES - 1, hist + tl - SUBLANES, ct), F32) for _ in range(n_s)],
        compiler_params=_params("parallel", "parallel", "arbitrary"),
        name=name,
    )(*([u] * n_s), *([state] * n_s), *([w] * n_s), *([bias] * n_s), *extras)
    return res[0], res[1:]


def _conv_slab_kernel(*refs, width, steps, n_s, n_x, post):
    u_refs = refs[0:n_s]
    st_refs = refs[n_s:2 * n_s]
    w_refs = refs[2 * n_s:3 * n_s]
    b_refs = refs[3 * n_s:4 * n_s]
    x_refs = refs[4 * n_s:4 * n_s + n_x]
    o_ref = refs[4 * n_s + n_x]
    ns_refs = refs[4 * n_s + n_x + 1:]
    past = width - 1

    def slab(s, i):
        return st_refs[s][i] if i < past else u_refs[s][i - past]

    extras = [x[...] for x in x_refs]
    for t in range(steps):
        outs = []
        for s in range(n_s):
            acc = b_refs[s][...] + w_refs[s][0:1, :] * slab(s, t)
            for k in range(1, width):
                acc = acc + w_refs[s][k:k + 1, :] * slab(s, t + k)
            outs.append(acc)
        o_ref[t] = post(outs, extras).astype(o_ref.dtype)
    for s in range(n_s):
        for i in range(past):
            ns_refs[s][i] = slab(s, i + steps)


def _conv_slab(u, state, w, bias, extras, post, *, width, ct, nbt, n_s, out_w, out_dtype, name):
    steps, nb, _ = u.shape
    past = width - 1
    nj = out_w // ct
    kern = functools.partial(_conv_slab_kernel, width=width, steps=steps, n_s=n_s, n_x=len(extras), post=post)
    in_specs = ([pl.BlockSpec((steps, nbt, ct), lambda i, j, s=s: (0, i, j + s * nj)) for s in range(n_s)]
                + [pl.BlockSpec((past, nbt, ct), lambda i, j, s=s: (0, i, j + s * nj)) for s in range(n_s)]
                + [pl.BlockSpec((width, ct), lambda i, j, s=s: (0, j + s * nj)) for s in range(n_s)]
                + [pl.BlockSpec((1, ct), lambda i, j, s=s: (0, j + s * nj)) for s in range(n_s)]
                + [pl.BlockSpec((1, ct), lambda i, j: (0, j)) for _ in extras])
    out_specs = ([pl.BlockSpec((steps, nbt, ct), lambda i, j: (0, i, j))]
                 + [pl.BlockSpec((past, nbt, ct), lambda i, j: (0, i, j)) for _ in range(n_s)])
    out_shape = ([jax.ShapeDtypeStruct((steps, nb, out_w), out_dtype)]
                 + [jax.ShapeDtypeStruct((past, nb, out_w), F32) for _ in range(n_s)])
    res = pl.pallas_call(
        kern,
        grid=(nb // nbt, nj),
        in_specs=in_specs,
        out_specs=out_specs,
        out_shape=out_shape,
        compiler_params=_params("parallel", "parallel"),
        name=name,
    )(*([u] * n_s), *([state] * n_s), *([w] * n_s), *([bias] * n_s), *extras)
    return res[0], res[1:]


def _ssd_kernel(xbc_ref, dt_ref, zs_ref, h0_ref, alog_ref, dsk_ref, gn_ref, e_ref, y_ref, ht_ref, h_scr, y_scr,
                *, q, bb):
    qk = SSD_KEYS
    c = pl.program_id(1)

    @pl.when(c == 0)
    def _():
        h_scr[...] = h0_ref[...]

    a = -jnp.exp(alog_ref[...])
    e = e_ref[...]
    row = lax.broadcasted_iota(jnp.int32, (qk, qk), 0)
    col = lax.broadcasted_iota(jnp.int32, (qk, qk), 1)
    tril = jnp.where(row >= col, 1.0, 0.0).astype(BF16)
    causal = row[:q, :] >= col[:q, :]
    key_head = lax.broadcasted_iota(jnp.int32, (qk, GROUP_W), 1) // HEAD_DIM

    def expand(vs):
        v = jnp.concatenate(vs, axis=0) if len(vs) > 1 else vs[0]
        hi = v.astype(BF16)
        lo = (v - hi.astype(F32)).astype(BF16)
        out = _dot(hi, e) + _dot(lo, e)
        n = vs[0].shape[0]
        return [out[i * n:(i + 1) * n, :] for i in range(len(vs))]

    def pad_keys(v):
        if q == qk:
            return v
        return jnp.concatenate([v, jnp.zeros((qk - q, v.shape[1]), v.dtype)], axis=0)

    seqs = range(bb)
    xqs = [xbc_ref[s] for s in seqs]
    xs = [xq[:, :D_B] for xq in xqs]
    bms = [pad_keys(xq[:, D_B:D_B + N_GROUPS * D_STATE]).astype(BF16) for xq in xqs]
    cms = [xq[:, D_B + N_GROUPS * D_STATE:].astype(BF16) for xq in xqs]
    dts = [dt_ref[s] for s in seqs]
    cums = [sum(_dot(tril, p) for p in _split_bf16(pad_keys(dt * a), 3)) for dt in dts]
    cum_ts = [cum.T for cum in cums]
    cum_qs = [cum[:q, :] for cum in cums]
    xdts = [x * d for x, d in zip(xs, expand(dts))]
    chunk_decay = [jnp.exp(cum[qk - 1:qk, :]) for cum in cums]
    to_end = [jnp.exp(cum[qk - 1:qk, :] - cum_q) for cum, cum_q in zip(cums, cum_qs)]
    wide = expand(to_end + [jnp.exp(cum_q) for cum_q in cum_qs])
    xws = [pad_keys(xdt * w).astype(BF16) for xdt, w in zip(xdts, wide[:bb])]
    xdts = [pad_keys(xdt).astype(BF16) for xdt in xdts]
    ecums = wide[bb:]

    for g in range(N_GROUPS):
        cols = slice(g * GROUP_W, (g + 1) * GROUP_W)
        for s in seqs:
            bg = bms[s][:, g * D_STATE:(g + 1) * D_STATE]
            cg = cms[s][:, g * D_STATE:(g + 1) * D_STATE]
            cb = lax.dot_general(cg, bg, (((1,), (1,)), ((), ())), preferred_element_type=F32)
            xdt_g = xdts[s][:, cols]
            scores, keys = [], []
            for r in range(HEADS_PER_GROUP):
                hd = g * HEADS_PER_GROUP + r
                seg = jnp.where(causal, cum_qs[s][:, hd:hd + 1] - cum_ts[s][hd:hd + 1, :], -1e30)
                scores.append((cb * jnp.exp(seg)).astype(BF16))
                keys.append(jnp.where(key_head == r, xdt_g, jnp.zeros_like(xdt_g)))
            y_g = _dot(jnp.concatenate(scores, axis=1), jnp.concatenate(keys, axis=0))
            h_g = h_scr[s, cols, :]
            y_inter = lax.dot_general(cg, h_g.astype(BF16), (((1,), (1,)), ((), ())),
                                      preferred_element_type=F32)
            y_scr[s, :, cols] = y_g + y_inter * ecums[s][:, cols]
            s_g = lax.dot_general(xws[s][:, cols], bg, (((0,), (0,)), ((), ())), preferred_element_type=F32)
            decay = jnp.concatenate(
                [jnp.broadcast_to(chunk_decay[s][:, g * HEADS_PER_GROUP + r:g * HEADS_PER_GROUP + r + 1],
                                  (HEAD_DIM, D_STATE)) for r in range(HEADS_PER_GROUP)], axis=0)
            h_scr[s, cols, :] = decay * h_g + s_g

    for s in seqs:
        y = y_scr[s] + dsk_ref[...] * xs[s]
        y_ref[s] = _rms(y * zs_ref[s], gn_ref[...]).astype(y_ref.dtype)

    @pl.when(c == pl.num_programs(1) - 1)
    def _():
        ht_ref[...] = h_scr[...]


def _ssd(xbc, dt, zs, h0, alog, dsk, gn, e, *, q, bb):
    b, L, _ = xbc.shape
    seq = lambda w: pl.BlockSpec((bb, q, w), lambda i, c: (i, c, 0))
    full = lambda arr: pl.BlockSpec(arr.shape, lambda i, c: (0,) * arr.ndim)
    state = pl.BlockSpec((bb, D_B, D_STATE), lambda i, c: (i, 0, 0))
    return pl.pallas_call(
        functools.partial(_ssd_kernel, q=q, bb=bb),
        grid=(b // bb, L // q),
        in_specs=[seq(D_XBC), seq(LANES), seq(D_B), state, full(alog), full(dsk), full(gn), full(e)],
        out_specs=[seq(D_B), state],
        out_shape=[jax.ShapeDtypeStruct((b, L, D_B), BF16), jax.ShapeDtypeStruct((b, D_B, D_STATE), F32)],
        scratch_shapes=[pltpu.VMEM((bb, D_B, D_STATE), F32), pltpu.VMEM((bb, q, D_B), F32)],
        compiler_params=_params("parallel", "arbitrary"),
        name="ssd",
    )(xbc, dt, zs, h0, alog, dsk, gn, e)


def _layer(x, wts, conv_a, xbc_fn, ssd_fn, ffn_fn):
    h, dt = _norm(x, wts["g_pre1"], wts["w_dt"], wts["dt_bias"])
    w_main = wts["w_main"]
    uglu = _proj(h, [w_main, w_main], [], _epi_glu, PROJ_COLS, F32, "in_glu", n=D_A, cols=[0, D_A])
    zs = _proj(h, [w_main], [], _epi_silu, PROJ_COLS, F32, "in_z", n=D_B, cols=[COL_Z])
    gates = _proj(h, [wts["w_g"]], [wts["b_gate"]], _epi_gate, PROJ_COLS, F32, "in_gates")

    ua, st_a = conv_a(uglu)
    xbc_act, st_b = xbc_fn(h)
    yb, st_h = ssd_fn(xbc_act, dt, zs)

    x1 = _merge(ua, yb, gates, x, wts["w_a_out"], wts["b_a_out"], wts["w_b_out"], wts["w_o"], wts["g_post1"])
    x2, st_f = ffn_fn(x1)
    return x2, st_a, st_b, st_h, st_f


def _seq_group(x, wts, st_a, st_b, st_h, st_f, tl, q):
    b, L, _ = x.shape

    def conv_a(uglu):
        ua, (ns,) = _conv_seq(uglu.reshape(b, L, D_A), st_a, wts["w_dw_a"], wts["b_dw_a"],
                              [wts["g_ln_a"], wts["b_ln_a"]], _post_ln_silu, width=CONV_A, ct=D_A,
                              tl=min(tl, 256), n_s=1, out_w=D_A, out_dtype=BF16, name="conv_a")
        return ua.reshape(b * L, D_A), ns

    def xbc_fn(h):
        return _proj_conv(h.reshape(b, L, D_MODEL), wts["w_main"], COL_XBC, st_b, wts["w_dw_b"], wts["b_dw_b"],
                          "in_xbc_conv")

    def ssd_fn(xbc_act, dt, zs):
        yb, ht = _ssd(xbc_act, dt.reshape(b, L, LANES), zs.reshape(b, L, D_B), st_h,
                      wts["a_log"], wts["d_skip"], wts["g_norm_b"], wts["expand"], q=q, bb=1)
        return yb.reshape(b * L, D_B), ht

    def ffn_fn(x1):
        x2, ns = _ffn(x1.reshape(b, L, D_MODEL), st_f, wts, tm=min(L, 512), sh=1)
        return x2.reshape(b * L, D_MODEL), ns

    x2, ns_a, ns_b, ns_h, ns_f = _layer(x.reshape(b * L, D_MODEL), wts, conv_a, xbc_fn, ssd_fn, ffn_fn)
    return x2.reshape(b, L, D_MODEL), ns_a, ns_b, ns_h, ns_f


def _step_group(x, wts, st_a, st_b, st_h, st_f):
    nb, L, _ = x.shape
    tmaj = lambda s: jnp.transpose(s, (1, 0, 2))
    lpad = -(-L // SUBLANES) * SUBLANES

    def conv_a(uglu):
        ua, (ns,) = _conv_slab(uglu.reshape(L, nb, D_A), tmaj(st_a), wts["w_dw_a"], wts["b_dw_a"],
                               [wts["g_ln_a"], wts["b_ln_a"]], _post_ln_silu, width=CONV_A, ct=D_A,
                               nbt=32, n_s=1, out_w=D_A, out_dtype=BF16, name="conv_a_step")
        return ua.reshape(L * nb, D_A), tmaj(ns)

    def bmaj_pad(v):
        return jnp.pad(tmaj(v), ((0, 0), (0, lpad - L), (0, 0)))

    def xbc_fn(h):
        xbc = _proj(h, [wts["w_main"]], [], _epi_id, PROJ_COLS, F32, "in_xbc", n=D_XBC, cols=[COL_XBC])
        act, (ns,) = _conv_slab(xbc.reshape(L, nb, D_XBC), tmaj(st_b), wts["w_dw_b"], wts["b_dw_b"], [],
                                _post_silu, width=CONV_B, ct=512, nbt=nb, n_s=1, out_w=D_XBC,
                                out_dtype=F32, name="conv_b_step")
        return act, tmaj(ns)

    def ssd_fn(xbc_act, dt, zs):
        yb, ht = _ssd(bmaj_pad(xbc_act), bmaj_pad(dt.reshape(L, nb, LANES)), bmaj_pad(zs.reshape(L, nb, D_B)),
                      st_h, wts["a_log"], wts["d_skip"], wts["g_norm_b"], wts["expand"], q=lpad, bb=4)
        return tmaj(yb[:, :L]).reshape(L * nb, D_B), ht

    def ffn_fn(x1):
        past = CONV_F - 1
        x2, ns = _ffn(x1.reshape(1, L * nb, D_MODEL), tmaj(st_f).reshape(1, past * nb, 2 * D_FF), wts,
                      tm=L * nb, sh=nb)
        return x2.reshape(L * nb, D_MODEL), tmaj(ns.reshape(past, nb, 2 * D_FF))

    x2, ns_a, ns_b, ns_h, ns_f = _layer(tmaj(x).reshape(L * nb, D_MODEL), wts, conv_a, xbc_fn, ssd_fn, ffn_fn)
    return tmaj(x2.reshape(L, nb, D_MODEL)), ns_a, ns_b, ns_h, ns_f


def _layer_weights(l, g_pre1, g_post1, w_in, b_gate, w_dw_a, b_dw_a, g_ln_a, b_ln_a, w_a_out, b_a_out,
                   w_dw_b, b_dw_b, dt_bias, a_log, d_skip, g_norm_b, w_b_out, w_o,
                   g_pre2, g_post2, w_up, w_dw_f, b_dw_f, w_down):
    row = lambda v: v[l].reshape(1, -1)
    lane_pad = lambda v: jnp.pad(v, ((0, 0), (0, LANES - v.shape[1])))
    w = w_in[l]
    head = jnp.arange(LANES, dtype=jnp.int32)[:, None]
    chan_head = (jnp.arange(D_B, dtype=jnp.int32) // HEAD_DIM)[None, :]
    expand = (head == chan_head).astype(BF16)
    return {
        "g_pre1": row(g_pre1), "g_post1": row(g_post1), "g_pre2": row(g_pre2), "g_post2": row(g_post2),
        "w_main": w[:, :COL_DT].astype(BF16),
        "w_dt": lane_pad(w[:, COL_DT:COL_GATE]).astype(BF16), "w_g": w[:, COL_GATE:].astype(BF16),
        "dt_bias": lane_pad(row(dt_bias)), "b_gate": row(b_gate),
        "w_dw_a": w_dw_a[l], "b_dw_a": row(b_dw_a), "g_ln_a": row(g_ln_a), "b_ln_a": row(b_ln_a),
        "w_a_out": w_a_out[l].astype(BF16), "b_a_out": row(b_a_out),
        "w_dw_b": w_dw_b[l], "b_dw_b": row(b_dw_b),
        "a_log": lane_pad(row(a_log)), "d_skip": jnp.repeat(d_skip[l], HEAD_DIM).reshape(1, D_B),
        "g_norm_b": row(g_norm_b), "expand": expand,
        "w_b_out": w_b_out[l].astype(BF16), "w_o": w_o[l].astype(BF16),
        "w_up": w_up[l].astype(BF16), "w_dw_f": w_dw_f[l], "b_dw_f": row(b_dw_f),
        "w_down": w_down[l].astype(BF16),
    }


def kernel(x_prompt, x_sample, state_conv_a, state_conv_b, state_ssm, state_conv_ffn, meta_tokens, g_pre1, g_post1, w_in, b_gate, w_dw_a, b_dw_a, g_ln_a, b_ln_a, w_a_out, b_a_out, w_dw_b, b_dw_b, dt_bias, a_log, d_skip, g_norm_b, w_b_out, w_o, g_pre2, g_post2, w_up, w_dw_f, b_dw_f, w_down):
    depth = w_in.shape[0]
    bp = x_prompt.shape[0]
    nb = x_sample.shape[0]
    xm = meta_tokens.astype(x_prompt.dtype)[None]
    xp, xs = x_prompt, x_sample
    pa, pb, ph, pf = [], [], [], []
    sa, sb, sh, sf = [], [], [], []
    for l in range(depth):
        wts = _layer_weights(l, g_pre1, g_post1, w_in, b_gate, w_dw_a, b_dw_a, g_ln_a, b_ln_a, w_a_out, b_a_out,
                             w_dw_b, b_dw_b, dt_bias, a_log, d_skip, g_norm_b, w_b_out, w_o,
                             g_pre2, g_post2, w_up, w_dw_f, b_dw_f, w_down)
        xm, m_a, m_b, m_h, m_f = _seq_group(
            xm, wts, jnp.zeros((1, CONV_A - 1, D_A), F32), jnp.zeros((1, CONV_B - 1, D_XBC), F32),
            jnp.zeros((1, D_B, D_STATE), F32), jnp.zeros((1, CONV_F - 1, 2 * D_FF), F32), tl=N_META, q=SUBLANES)
        rep = lambda s: jnp.broadcast_to(s, (bp,) + s.shape[1:])
        xp, c_a, c_b, c_h, c_f = _seq_group(xp, wts, rep(m_a), rep(m_b), rep(m_h), rep(m_f), tl=512, q=SSD_KEYS)
        pa.append(c_a); pb.append(c_b); ph.append(c_h.reshape(bp, N_HEADS, HEAD_DIM, D_STATE)); pf.append(c_f)
        xs, d_a, d_b, d_h, d_f = _step_group(
            xs, wts, state_conv_a[l], state_conv_b[l], state_ssm[l].reshape(nb, D_B, D_STATE), state_conv_ffn[l])
        sa.append(d_a); sb.append(d_b); sh.append(d_h.reshape(nb, N_HEADS, HEAD_DIM, D_STATE)); sf.append(d_f)
    return (xp, xs, jnp.stack(pa), jnp.stack(pb), jnp.stack(ph), jnp.stack(pf),
            jnp.stack(sa), jnp.stack(sb), jnp.stack(sh), jnp.stack(sf))
```

```python
import functools

import jax
import jax.numpy as jnp
from jax import lax
from jax.experimental import pallas as pl
from jax.experimental.pallas import tpu as pltpu

D_MODEL = 2048
N_META = 16
D_A = 1024
CONV_A = 31
D_B = 2048
HEAD_DIM = 64
N_HEADS = D_B // HEAD_DIM
N_GROUPS = 8
HEADS_PER_GROUP = N_HEADS // N_GROUPS
GROUP_W = HEADS_PER_GROUP * HEAD_DIM
D_STATE = 128
CONV_B = 4
D_FF = 5632
CONV_F = 3
EPS = 1e-6
D_XBC = D_B + 2 * N_GROUPS * D_STATE
COL_Z = 2 * D_A
COL_XBC = COL_Z + D_B
COL_DT = COL_XBC + D_XBC
COL_GATE = COL_DT + N_HEADS

LANES = 128
SUBLANES = 8
SSD_KEYS = 128
PROJ_ROWS = 256
PROJ_COLS = 1024
VMEM_LIMIT = 56 * 1024 * 1024

F32 = jnp.float32
BF16 = jnp.bfloat16


def _params(*sem):
    return pltpu.CompilerParams(dimension_semantics=sem, vmem_limit_bytes=VMEM_LIMIT)


def _sigmoid(x):
    return 1.0 / (1.0 + jnp.exp(-x))


def _silu(x):
    return x * _sigmoid(x)


def _softplus(x):
    return jnp.maximum(x, 0.0) + jnp.log1p(jnp.exp(-jnp.abs(x)))


def _gelu_tanh(x):
    return 0.5 * x * (1.0 + jnp.tanh(0.7978845608028654 * (x + 0.044715 * (x * x * x))))


def _rms(x, g):
    r = lax.rsqrt(jnp.mean(x * x, axis=-1, keepdims=True) + EPS)
    return x * r * g


def _dot(a, b):
    return jnp.dot(a, b, preferred_element_type=F32)


def _rows(ref, lo, n, cols=slice(None)):
    base, off = divmod(lo, SUBLANES)
    if off == 0:
        return ref[lo:lo + n, cols]
    blk = ref[base * SUBLANES:base * SUBLANES + n + SUBLANES, cols]
    return pltpu.roll(blk, n + SUBLANES - off, 0)[0:n, :]


def _split_bf16(v, parts):
    out = []
    for _ in range(parts):
        p = v.astype(BF16)
        out.append(p)
        v = v - p.astype(F32)
    return out


def _norm_kernel(x_ref, g_ref, wdt_ref, bdt_ref, h_ref, dt_ref):
    h = _rms(x_ref[...], g_ref[...]).astype(BF16)
    h_ref[...] = h
    dt_ref[...] = _epi_dt([_dot(h, wdt_ref[...])], [bdt_ref[...]])


def _norm(x, g, w_dt, dt_bias):
    rows, d = x.shape
    tm = min(rows, 512)
    const = lambda a: pl.BlockSpec(a.shape, lambda i: (0, 0))
    return pl.pallas_call(
        _norm_kernel,
        grid=(rows // tm,),
        in_specs=[pl.BlockSpec((tm, d), lambda i: (i, 0)), const(g), const(w_dt), const(dt_bias)],
        out_specs=[pl.BlockSpec((tm, d), lambda i: (i, 0)), pl.BlockSpec((tm, LANES), lambda i: (i, 0))],
        out_shape=[jax.ShapeDtypeStruct((rows, d), BF16), jax.ShapeDtypeStruct((rows, LANES), F32)],
        compiler_params=_params("parallel"),
        name="norm_dt",
    )(x, g, w_dt, dt_bias)


def _proj_kernel(*refs, n_w, n_b, epilogue):
    h_ref = refs[0]
    w_refs = refs[1:1 + n_w]
    b_refs = refs[1 + n_w:1 + n_w + n_b]
    o_ref = refs[-1]
    bs = [b[...] for b in b_refs]
    rows = h_ref.shape[0]
    mc = min(rows, PROJ_ROWS)
    for r0 in range(0, rows, mc):
        h = h_ref[r0:r0 + mc, :]
        accs = [_dot(h, w[...]) for w in w_refs]
        o_ref[r0:r0 + mc, :] = epilogue(accs, bs).astype(o_ref.dtype)


def _proj(h, ws, bs, epilogue, tn, out_dtype, name, n=None, cols=None):
    rows, d = h.shape
    n = n or ws[0].shape[1]
    cols = cols or [0] * len(ws)
    tm = min(rows, 1024)
    kern = functools.partial(_proj_kernel, n_w=len(ws), n_b=len(bs), epilogue=epilogue)
    return pl.pallas_call(
        kern,
        grid=(rows // tm, n // tn),
        in_specs=[pl.BlockSpec((tm, d), lambda i, j: (i, 0))]
                 + [pl.BlockSpec((d, tn), lambda i, j, c=c // tn: (0, j + c)) for c in cols]
                 + [pl.BlockSpec((1, tn), lambda i, j: (0, j)) for _ in bs],
        out_specs=pl.BlockSpec((tm, tn), lambda i, j: (i, j)),
        out_shape=jax.ShapeDtypeStruct((rows, n), out_dtype),
        compiler_params=_params("parallel", "parallel"),
        name=name,
    )(h, *ws, *bs)


def _proj_conv_kernel(h_ref, w_ref, st_ref, cw_ref, cb_ref, o_ref, ns_ref, win, *carry, tm, rc):
    l = pl.program_id(1)
    j = pl.program_id(2)
    past = CONV_B - 1
    hist = SUBLANES
    if carry:
        @pl.when(l == 0)
        def _():
            win[hist - past:hist, :] = st_ref[0]

        @pl.when(l > 0)
        def _():
            win[0:hist, :] = carry[0][j]
    else:
        win[hist - past:hist, :] = st_ref[0]

    mc = min(tm, PROJ_ROWS)
    for m0 in range(0, tm, mc):
        win[hist + m0:hist + m0 + mc, :] = _dot(h_ref[0, m0:m0 + mc, :], w_ref[...])
        for r0 in range(m0, m0 + mc, rc):
            pre = cb_ref[...]
            for k in range(CONV_B):
                pre = pre + cw_ref[k:k + 1, :] * _rows(win, r0 + hist - past + k, rc)
            o_ref[0, r0:r0 + rc, :] = _silu(pre)

    ns_ref[0, 0] = win[hist + tm - past:hist + tm, :]
    if carry:
        carry[0][j] = win[tm:tm + hist, :]


def _proj_conv(h, w, col0, state, cw, cb, name):
    b, L, d = h.shape
    n = state.shape[-1]
    past = CONV_B - 1
    tn = PROJ_COLS
    tm = min(L, 1024)
    n_l, nj = L // tm, n // tn
    carry = [pltpu.VMEM((nj, SUBLANES, tn), F32)] if n_l > 1 else []
    out, ns = pl.pallas_call(
        functools.partial(_proj_conv_kernel, tm=tm, rc=min(tm, 32)),
        grid=(b, n_l, nj),
        in_specs=[pl.BlockSpec((1, tm, d), lambda i, l, j: (i, l, 0)),
                  pl.BlockSpec((d, tn), lambda i, l, j: (0, j + col0 // tn)),
                  pl.BlockSpec((1, past, tn), lambda i, l, j: (i, 0, j)),
                  pl.BlockSpec((CONV_B, tn), lambda i, l, j: (0, j)),
                  pl.BlockSpec((1, tn), lambda i, l, j: (0, j))],
        out_specs=[pl.BlockSpec((1, tm, tn), lambda i, l, j: (i, l, j)),
                   pl.BlockSpec((1, 1, past, tn), lambda i, l, j: (i, l, 0, j))],
        out_shape=[jax.ShapeDtypeStruct((b, L, n), F32), jax.ShapeDtypeStruct((b, n_l, past, n), F32)],
        scratch_shapes=[pltpu.VMEM((SUBLANES + tm, tn), F32)] + carry,
        compiler_params=_params("parallel", "arbitrary", "arbitrary"),
        name=name,
    )(h, w, state, cw, cb)
    return out, ns[:, -1]


def _epi_glu(a, b):
    return a[0] * _sigmoid(a[1])


def _epi_dt(a, b):
    lane = lax.broadcasted_iota(jnp.int32, a[0].shape, 1)
    return jnp.where(lane < N_HEADS, _softplus(a[0] + b[0]), 0.0)


def _epi_silu(a, b):
    return _silu(a[0])


def _epi_id(a, b):
    return a[0]


def _epi_gate(a, b):
    return _sigmoid(a[0] + b[0])


def _merge_kernel(ua_ref, yb_ref, gt_ref, x_ref, wa_ref, ba_ref, wb_ref, wo_ref, gp_ref, o_ref):
    ya = _dot(ua_ref[...], wa_ref[...]) + ba_ref[...]
    yb = _dot(yb_ref[...], wb_ref[...])
    mix = gt_ref[:, :D_MODEL] * ya + gt_ref[:, D_MODEL:] * yb
    m = _dot(mix.astype(BF16), wo_ref[...])
    o_ref[...] = x_ref[...] + _rms(m, gp_ref[...])


def _merge(ua, yb, gates, x, wa, ba, wb, wo, gp):
    rows = x.shape[0]
    tm = min(rows, 256)
    row_spec = lambda w: pl.BlockSpec((tm, w), lambda i: (i, 0))
    res_spec = lambda a: pl.BlockSpec(a.shape, lambda i: (0, 0), pipeline_mode=pl.Buffered(1))
    return pl.pallas_call(
        _merge_kernel,
        grid=(rows // tm,),
        in_specs=[row_spec(D_A), row_spec(D_B), row_spec(2 * D_MODEL), row_spec(D_MODEL),
                  res_spec(wa), res_spec(ba), res_spec(wb), res_spec(wo), res_spec(gp)],
        out_specs=row_spec(D_MODEL),
        out_shape=jax.ShapeDtypeStruct((rows, D_MODEL), F32),
        compiler_params=_params("parallel"),
        name="merge",
    )(ua, yb, gates, x, wa, ba, wb, wo, gp)


def _ffn_kernel(x1_ref, gpre_ref, wg_ref, wv_ref, wd_ref, stg_ref, stv_ref, cw_ref, cb_ref,
                gpost_ref, o_ref, nsg_ref, nsv_ref, h_scr, acc_scr, act_scr, win_g, win_v, *carry,
                tm, sh, hist, rc):
    l = pl.program_id(1)
    j = pl.program_id(2)
    past = (CONV_F - 1) * sh

    @pl.when(j == 0)
    def _():
        h_scr[...] = _rms(x1_ref[0], gpre_ref[...]).astype(BF16)
        acc_scr[...] = jnp.zeros_like(acc_scr)

    for s, (st_ref, win) in enumerate(((stg_ref, win_g), (stv_ref, win_v))):
        if carry:
            @pl.when(l == 0)
            def _(win=win, st_ref=st_ref):
                win[hist - past:hist, :] = st_ref[0]

            @pl.when(l > 0)
            def _(win=win, s=s):
                win[0:hist, :] = carry[0][j, s]
        else:
            win[hist - past:hist, :] = st_ref[0]

    tf = win_g.shape[1]
    col_g = pl.multiple_of(j * tf, tf)
    col_v = pl.multiple_of(D_FF + j * tf, tf)
    cwg, cbg = cw_ref[:, pl.ds(col_g, tf)], cb_ref[:, pl.ds(col_g, tf)]
    cwv, cbv = cw_ref[:, pl.ds(col_v, tf)], cb_ref[:, pl.ds(col_v, tf)]

    def conv(win, cw_t, cb_t, r0, cs):
        out = cb_t[:, cs]
        for k in range(CONV_F):
            out = out + cw_t[k:k + 1, cs] * _rows(win, hist - (CONV_F - 1 - k) * sh + r0, rc, cs)
        return out

    cw = min(tf, 256)
    mc = min(tm, PROJ_ROWS)
    for c0 in range(0, tf, cw):
        cs = slice(c0, c0 + cw)
        for m0 in range(0, tm, mc):
            win_g[hist + m0:hist + m0 + mc, cs] = _dot(h_scr[m0:m0 + mc, :], wg_ref[:, cs])
            win_v[hist + m0:hist + m0 + mc, cs] = _dot(h_scr[m0:m0 + mc, :], wv_ref[:, cs])
            for r0 in range(m0, m0 + mc, rc):
                gate = _gelu_tanh(conv(win_g, cwg, cbg, r0, cs))
                act_scr[r0:r0 + rc, cs] = (gate * conv(win_v, cwv, cbv, r0, cs)).astype(BF16)
    acc_scr[...] += _dot(act_scr[...], wd_ref[...])

    for s, (ns_ref, win) in enumerate(((nsg_ref, win_g), (nsv_ref, win_v))):
        ns_ref[0, 0] = win[hist + tm - past:hist + tm, :]
        if carry:
            carry[0][j, s] = win[tm:tm + hist, :]

    @pl.when(j == pl.num_programs(2) - 1)
    def _():
        o_ref[0] = x1_ref[0] + _rms(acc_scr[...], gpost_ref[...])


def _ffn(x1, state, wts, *, tm, sh):
    b, L, _ = x1.shape
    tf = 512
    nj = D_FF // tf
    past = (CONV_F - 1) * sh
    hist = -(-past // SUBLANES) * SUBLANES
    n_l = L // tm
    carry = [pltpu.VMEM((nj, 2, hist, tf), F32)] if n_l > 1 else []
    col = lambda rows, off: pl.BlockSpec((rows, tf), lambda i, l, j: (0, j + off))
    st = lambda off: pl.BlockSpec((1, past, tf), lambda i, l, j: (i, 0, j + off))
    tail = pl.BlockSpec((1, 1, past, tf), lambda i, l, j: (i, l, 0, j))
    vec = pl.BlockSpec((1, D_MODEL), lambda i, l, j: (0, 0))
    const = lambda a: pl.BlockSpec(a.shape, lambda i, l, j: (0, 0))
    xblk = pl.BlockSpec((1, tm, D_MODEL), lambda i, l, j: (i, l, 0))
    out, ns_g, ns_v = pl.pallas_call(
        functools.partial(_ffn_kernel, tm=tm, sh=sh, hist=hist, rc=min(tm, 32)),
        grid=(b, n_l, nj),
        in_specs=[xblk, vec, col(D_MODEL, 0), col(D_MODEL, nj),
                  pl.BlockSpec((tf, D_MODEL), lambda i, l, j: (j, 0)),
                  st(0), st(nj), const(wts["w_dw_f"]), const(wts["b_dw_f"]), vec],
        out_specs=[xblk, tail, tail],
        out_shape=[jax.ShapeDtypeStruct((b, L, D_MODEL), F32),
                   jax.ShapeDtypeStruct((b, n_l, past, D_FF), F32),
                   jax.ShapeDtypeStruct((b, n_l, past, D_FF), F32)],
        scratch_shapes=[pltpu.VMEM((tm, D_MODEL), BF16), pltpu.VMEM((tm, D_MODEL), F32),
                        pltpu.VMEM((tm, tf), BF16),
                        pltpu.VMEM((hist + tm, tf), F32), pltpu.VMEM((hist + tm, tf), F32)] + carry,
        compiler_params=_params("parallel", "arbitrary", "arbitrary"),
        name="ffn",
    )(x1, wts["g_pre2"], wts["w_up"], wts["w_up"], wts["w_down"], state, state,
      wts["w_dw_f"], wts["b_dw_f"], wts["g_post2"])
    return out, jnp.concatenate([ns_g[:, -1], ns_v[:, -1]], axis=-1)


def _post_ln_silu(outs, extras):
    u = outs[0]
    mu = jnp.mean(u, axis=-1, keepdims=True)
    xc = u - mu
    r = lax.rsqrt(jnp.mean(xc * xc, axis=-1, keepdims=True) + EPS)
    return _silu(xc * r * extras[0] + extras[1])


def _post_silu(outs, extras):
    return _silu(outs[0])


def _conv_seq_kernel(*refs, width, tl, n_s, n_x, post, hist, rc):
    u_refs = refs[0:n_s]
    st_refs = refs[n_s:2 * n_s]
    w_refs = refs[2 * n_s:3 * n_s]
    b_refs = refs[3 * n_s:4 * n_s]
    x_refs = refs[4 * n_s:4 * n_s + n_x]
    o_ref = refs[4 * n_s + n_x]
    ns_refs = refs[4 * n_s + n_x + 1:4 * n_s + n_x + 1 + n_s]
    win_refs = refs[4 * n_s + n_x + 1 + n_s:4 * n_s + n_x + 1 + 2 * n_s]
    shf_refs = refs[4 * n_s + n_x + 1 + 2 * n_s:]
    past = width - 1
    l = pl.program_id(2)
    n_shf = hist + tl - SUBLANES

    for s in range(n_s):
        win = win_refs[s]

        @pl.when(l == 0)
        def _(win=win, s=s):
            if hist > past:
                win[0:hist - past, :] = jnp.zeros((hist - past, win.shape[1]), F32)
            win[hist - past:hist, :] = st_refs[s][0]

        @pl.when(l > 0)
        def _(win=win):
            win[0:hist, :] = win[tl:tl + hist, :]

        win[hist:hist + tl, :] = u_refs[s][0]
        for r in range(1, SUBLANES):
            shf_refs[s][r - 1, :, :] = _rows(win, r, n_shf)

    def tap(s, lo):
        a, r = divmod(lo, SUBLANES)
        if r == 0:
            return win_refs[s][lo:lo + rc, :]
        return shf_refs[s][r - 1, a * SUBLANES:a * SUBLANES + rc, :]

    extras = [x[...] for x in x_refs]
    for c in range(tl // rc):
        outs = []
        for s in range(n_s):
            acc = b_refs[s][...] + w_refs[s][0:1, :] * tap(s, c * rc + hist - past)
            for k in range(1, width):
                acc = acc + w_refs[s][k:k + 1, :] * tap(s, c * rc + hist - past + k)
            outs.append(acc)
        o_ref[0, c * rc:(c + 1) * rc, :] = post(outs, extras).astype(o_ref.dtype)

    @pl.when(l == pl.num_programs(2) - 1)
    def _():
        for s in range(n_s):
            ns_refs[s][0] = win_refs[s][hist + tl - past:hist + tl, :]


def _conv_seq(u, state, w, bias, extras, post, *, width, ct, tl, n_s, out_w, out_dtype, name):
    b, L, _ = u.shape
    past = width - 1
    hist = -(-past // SUBLANES) * SUBLANES
    nj = out_w // ct
    rc = min(tl, 64)
    kern = functools.partial(_conv_seq_kernel, width=width, tl=tl, n_s=n_s, n_x=len(extras),
                             post=post, hist=hist, rc=rc)
    in_specs = ([pl.BlockSpec((1, tl, ct), lambda i, j, l, s=s: (i, l, j + s * nj)) for s in range(n_s)]
                + [pl.BlockSpec((1, past, ct), lambda i, j, l, s=s: (i, 0, j + s * nj)) for s in range(n_s)]
                + [pl.BlockSpec((width, ct), lambda i, j, l, s=s: (0, j + s * nj)) for s in range(n_s)]
                + [pl.BlockSpec((1, ct), lambda i, j, l, s=s: (0, j + s * nj)) for s in range(n_s)]
                + [pl.BlockSpec((1, ct), lambda i, j, l: (0, j)) for _ in extras])
    out_specs = ([pl.BlockSpec((1, tl, ct), lambda i, j, l: (i, l, j))]
                 + [pl.BlockSpec((1, past, ct), lambda i, j, l: (i, 0, j)) for _ in range(n_s)])
    out_shape = ([jax.ShapeDtypeStruct((b, L, out_w), out_dtype)]
                 + [jax.ShapeDtypeStruct((b, past, out_w), F32) for _ in range(n_s)])
    res = pl.pallas_call(
        kern,
        grid=(b, nj, L // tl),
        in_specs=in_specs,
        out_specs=out_specs,
        out_shape=out_shape,
        scratch_shapes=[pltpu.VMEM((hist + tl, ct), F32) for _ in range(n_s)]
                       + [pltpu.VMEM((SUBLANES - 1, hist + tl - SUBLANES, ct), F32) for _ in range(n_s)],
        compiler_params=_params("parallel", "parallel", "arbitrary"),
        name=name,
    )(*([u] * n_s), *([state] * n_s), *([w] * n_s), *([bias] * n_s), *extras)
    return res[0], res[1:]


def _conv_slab_kernel(*refs, width, steps, n_s, n_x, post):
    u_refs = refs[0:n_s]
    st_refs = refs[n_s:2 * n_s]
    w_refs = refs[2 * n_s:3 * n_s]
    b_refs = refs[3 * n_s:4 * n_s]
    x_refs = refs[4 * n_s:4 * n_s + n_x]
    o_ref = refs[4 * n_s + n_x]
    ns_refs = refs[4 * n_s + n_x + 1:]
    past = width - 1

    def slab(s, i):
        return st_refs[s][i] if i < past else u_refs[s][i - past]

    extras = [x[...] for x in x_refs]
    for t in range(steps):
        outs = []
        for s in range(n_s):
            acc = b_refs[s][...] + w_refs[s][0:1, :] * slab(s, t)
            for k in range(1, width):
                acc = acc + w_refs[s][k:k + 1, :] * slab(s, t + k)
            outs.append(acc)
        o_ref[t] = post(outs, extras).astype(o_ref.dtype)
    for s in range(n_s):
        for i in range(past):
            ns_refs[s][i] = slab(s, i + steps)


def _conv_slab(u, state, w, bias, extras, post, *, width, ct, nbt, n_s, out_w, out_dtype, name):
    steps, nb, _ = u.shape
    past = width - 1
    nj = out_w // ct
    kern = functools.partial(_conv_slab_kernel, width=width, steps=steps, n_s=n_s, n_x=len(extras), post=post)
    in_specs = ([pl.BlockSpec((steps, nbt, ct), lambda i, j, s=s: (0, i, j + s * nj)) for s in range(n_s)]
                + [pl.BlockSpec((past, nbt, ct), lambda i, j, s=s: (0, i, j + s * nj)) for s in range(n_s)]
                + [pl.BlockSpec((width, ct), lambda i, j, s=s: (0, j + s * nj)) for s in range(n_s)]
                + [pl.BlockSpec((1, ct), lambda i, j, s=s: (0, j + s * nj)) for s in range(n_s)]
                + [pl.BlockSpec((1, ct), lambda i, j: (0, j)) for _ in extras])
    out_specs = ([pl.BlockSpec((steps, nbt, ct), lambda i, j: (0, i, j))]
                 + [pl.BlockSpec((past, nbt, ct), lambda i, j: (0, i, j)) for _ in range(n_s)])
    out_shape = ([jax.ShapeDtypeStruct((steps, nb, out_w), out_dtype)]
                 + [jax.ShapeDtypeStruct((past, nb, out_w), F32) for _ in range(n_s)])
    res = pl.pallas_call(
        kern,
        grid=(nb // nbt, nj),
        in_specs=in_specs,
        out_specs=out_specs,
        out_shape=out_shape,
        compiler_params=_params("parallel", "parallel"),
        name=name,
    )(*([u] * n_s), *([state] * n_s), *([w] * n_s), *([bias] * n_s), *extras)
    return res[0], res[1:]


def _ssd_kernel(xbc_ref, dt_ref, zs_ref, h0_ref, alog_ref, dsk_ref, gn_ref, e_ref, y_ref, ht_ref, h_scr, y_scr,
                *, q, bb):
    qk = SSD_KEYS
    c = pl.program_id(1)

    @pl.when(c == 0)
    def _():
        h_scr[...] = h0_ref[...]

    a = -jnp.exp(alog_ref[...])
    e = e_ref[...]
    row = lax.broadcasted_iota(jnp.int32, (qk, qk), 0)
    col = lax.broadcasted_iota(jnp.int32, (qk, qk), 1)
    tril = jnp.where(row >= col, 1.0, 0.0).astype(BF16)
    causal = row[:q, :] >= col[:q, :]
    key_head = lax.broadcasted_iota(jnp.int32, (qk, GROUP_W), 1) // HEAD_DIM

    def expand(vs):
        v = jnp.concatenate(vs, axis=0) if len(vs) > 1 else vs[0]
        hi = v.astype(BF16)
        lo = (v - hi.astype(F32)).astype(BF16)
        out = _dot(hi, e) + _dot(lo, e)
        n = vs[0].shape[0]
        return [out[i * n:(i + 1) * n, :] for i in range(len(vs))]

    def pad_keys(v):
        if q == qk:
            return v
        return jnp.concatenate([v, jnp.zeros((qk - q, v.shape[1]), v.dtype)], axis=0)

    seqs = range(bb)
    xqs = [xbc_ref[s] for s in seqs]
    xs = [xq[:, :D_B] for xq in xqs]
    bms = [pad_keys(xq[:, D_B:D_B + N_GROUPS * D_STATE]).astype(BF16) for xq in xqs]
    cms = [xq[:, D_B + N_GROUPS * D_STATE:].astype(BF16) for xq in xqs]
    dts = [dt_ref[s] for s in seqs]
    cums = [sum(_dot(tril, p) for p in _split_bf16(pad_keys(dt * a), 3)) for dt in dts]
    cum_ts = [cum.T for cum in cums]
    cum_qs = [cum[:q, :] for cum in cums]
    xdts = [x * d for x, d in zip(xs, expand(dts))]
    chunk_decay = [jnp.exp(cum[qk - 1:qk, :]) for cum in cums]
    to_end = [jnp.exp(cum[qk - 1:qk, :] - cum_q) for cum, cum_q in zip(cums, cum_qs)]
    wide = expand(to_end + [jnp.exp(cum_q) for cum_q in cum_qs])
    xws = [pad_keys(xdt * w).astype(BF16) for xdt, w in zip(xdts, wide[:bb])]
    xdts = [pad_keys(xdt).astype(BF16) for xdt in xdts]
    ecums = wide[bb:]

    for g in range(N_GROUPS):
        cols = slice(g * GROUP_W, (g + 1) * GROUP_W)
        for s in seqs:
            bg = bms[s][:, g * D_STATE:(g + 1) * D_STATE]
            cg = cms[s][:, g * D_STATE:(g + 1) * D_STATE]
            cb = lax.dot_general(cg, bg, (((1,), (1,)), ((), ())), preferred_element_type=F32)
            xdt_g = xdts[s][:, cols]
            scores, keys = [], []
            for r in range(HEADS_PER_GROUP):
                hd = g * HEADS_PER_GROUP + r
                seg = jnp.where(causal, cum_qs[s][:, hd:hd + 1] - cum_ts[s][hd:hd + 1, :], -1e30)
                scores.append((cb * jnp.exp(seg)).astype(BF16))
                keys.append(jnp.where(key_head == r, xdt_g, jnp.zeros_like(xdt_g)))
            y_g = _dot(jnp.concatenate(scores, axis=1), jnp.concatenate(keys, axis=0))
            h_g = h_scr[s, cols, :]
            y_inter = lax.dot_general(cg, h_g.astype(BF16), (((1,), (1,)), ((), ())),
                                      preferred_element_type=F32)
            y_scr[s, :, cols] = y_g + y_inter * ecums[s][:, cols]
            s_g = lax.dot_general(xws[s][:, cols], bg, (((0,), (0,)), ((), ())), preferred_element_type=F32)
            decay = jnp.concatenate(
                [jnp.broadcast_to(chunk_decay[s][:, g * HEADS_PER_GROUP + r:g * HEADS_PER_GROUP + r + 1],
                                  (HEAD_DIM, D_STATE)) for r in range(HEADS_PER_GROUP)], axis=0)
            h_scr[s, cols, :] = decay * h_g + s_g

    for s in seqs:
        y = y_scr[s] + dsk_ref[...] * xs[s]
        y_ref[s] = _rms(y * zs_ref[s], gn_ref[...]).astype(y_ref.dtype)

    @pl.when(c == pl.num_programs(1) - 1)
    def _():
        ht_ref[...] = h_scr[...]


def _ssd(xbc, dt, zs, h0, alog, dsk, gn, e, *, q, bb):
    b, L, _ = xbc.shape
    seq = lambda w: pl.BlockSpec((bb, q, w), lambda i, c: (i, c, 0))
    full = lambda arr: pl.BlockSpec(arr.shape, lambda i, c: (0,) * arr.ndim)
    state = pl.BlockSpec((bb, D_B, D_STATE), lambda i, c: (i, 0, 0))
    return pl.pallas_call(
        functools.partial(_ssd_kernel, q=q, bb=bb),
        grid=(b // bb, L // q),
        in_specs=[seq(D_XBC), seq(LANES), seq(D_B), state, full(alog), full(dsk), full(gn), full(e)],
        out_specs=[seq(D_B), state],
        out_shape=[jax.ShapeDtypeStruct((b, L, D_B), BF16), jax.ShapeDtypeStruct((b, D_B, D_STATE), F32)],
        scratch_shapes=[pltpu.VMEM((bb, D_B, D_STATE), F32), pltpu.VMEM((bb, q, D_B), F32)],
        compiler_params=_params("parallel", "arbitrary"),
        name="ssd",
    )(xbc, dt, zs, h0, alog, dsk, gn, e)


def _layer(x, wts, conv_a, xbc_fn, ssd_fn, ffn_fn):
    h, dt = _norm(x, wts["g_pre1"], wts["w_dt"], wts["dt_bias"])
    w_main = wts["w_main"]
    uglu = _proj(h, [w_main, w_main], [], _epi_glu, PROJ_COLS, F32, "in_glu", n=D_A, cols=[0, D_A])
    zs = _proj(h, [w_main], [], _epi_silu, PROJ_COLS, F32, "in_z", n=D_B, cols=[COL_Z])
    gates = _proj(h, [wts["w_g"]], [wts["b_gate"]], _epi_gate, PROJ_COLS, F32, "in_gates")

    ua, st_a = conv_a(uglu)
    xbc_act, st_b = xbc_fn(h)
    yb, st_h = ssd_fn(xbc_act, dt, zs)

    x1 = _merge(ua, yb, gates, x, wts["w_a_out"], wts["b_a_out"], wts["w_b_out"], wts["w_o"], wts["g_post1"])
    x2, st_f = ffn_fn(x1)
    return x2, st_a, st_b, st_h, st_f


def _seq_group(x, wts, st_a, st_b, st_h, st_f, tl, q):
    b, L, _ = x.shape

    def conv_a(uglu):
        ua, (ns,) = _conv_seq(uglu.reshape(b, L, D_A), st_a, wts["w_dw_a"], wts["b_dw_a"],
                              [wts["g_ln_a"], wts["b_ln_a"]], _post_ln_silu, width=CONV_A, ct=D_A,
                              tl=min(tl, 256), n_s=1, out_w=D_A, out_dtype=BF16, name="conv_a")
        return ua.reshape(b * L, D_A), ns

    def xbc_fn(h):
        return _proj_conv(h.reshape(b, L, D_MODEL), wts["w_main"], COL_XBC, st_b, wts["w_dw_b"], wts["b_dw_b"],
                          "in_xbc_conv")

    def ssd_fn(xbc_act, dt, zs):
        yb, ht = _ssd(xbc_act, dt.reshape(b, L, LANES), zs.reshape(b, L, D_B), st_h,
                      wts["a_log"], wts["d_skip"], wts["g_norm_b"], wts["expand"], q=q, bb=1)
        return yb.reshape(b * L, D_B), ht

    def ffn_fn(x1):
        x2, ns = _ffn(x1.reshape(b, L, D_MODEL), st_f, wts, tm=min(L, 512), sh=1)
        return x2.reshape(b * L, D_MODEL), ns

    x2, ns_a, ns_b, ns_h, ns_f = _layer(x.reshape(b * L, D_MODEL), wts, conv_a, xbc_fn, ssd_fn, ffn_fn)
    return x2.reshape(b, L, D_MODEL), ns_a, ns_b, ns_h, ns_f


def _step_group(x, wts, st_a, st_b, st_h, st_f):
    nb, L, _ = x.shape
    tmaj = lambda s: jnp.transpose(s, (1, 0, 2))
    lpad = -(-L // SUBLANES) * SUBLANES

    def conv_a(uglu):
        ua, (ns,) = _conv_slab(uglu.reshape(L, nb, D_A), tmaj(st_a), wts["w_dw_a"], wts["b_dw_a"],
                               [wts["g_ln_a"], wts["b_ln_a"]], _post_ln_silu, width=CONV_A, ct=D_A,
                               nbt=32, n_s=1, out_w=D_A, out_dtype=BF16, name="conv_a_step")
        return ua.reshape(L * nb, D_A), tmaj(ns)

    def bmaj_pad(v):
        return jnp.pad(tmaj(v), ((0, 0), (0, lpad - L), (0, 0)))

    def xbc_fn(h):
        xbc = _proj(h, [wts["w_main"]], [], _epi_id, PROJ_COLS, F32, "in_xbc", n=D_XBC, cols=[COL_XBC])
        act, (ns,) = _conv_slab(xbc.reshape(L, nb, D_XBC), tmaj(st_b), wts["w_dw_b"], wts["b_dw_b"], [],
                                _post_silu, width=CONV_B, ct=512, nbt=nb, n_s=1, out_w=D_XBC,
                                out_dtype=F32, name="conv_b_step")
        return act, tmaj(ns)

    def ssd_fn(xbc_act, dt, zs):
        yb, ht = _ssd(bmaj_pad(xbc_act), bmaj_pad(dt.reshape(L, nb, LANES)), bmaj_pad(zs.reshape(L, nb, D_B)),
                      st_h, wts["a_log"], wts["d_skip"], wts["g_norm_b"], wts["expand"], q=lpad, bb=4)
        return tmaj(yb[:, :L]).reshape(L * nb, D_B), ht

    def ffn_fn(x1):
        past = CONV_F - 1
        x2, ns = _ffn(x1.reshape(1, L * nb, D_MODEL), tmaj(st_f).reshape(1, past * nb, 2 * D_FF), wts,
                      tm=L * nb, sh=nb)
        return x2.reshape(L * nb, D_MODEL), tmaj(ns.reshape(past, nb, 2 * D_FF))

    x2, ns_a, ns_b, ns_h, ns_f = _layer(tmaj(x).reshape(L * nb, D_MODEL), wts, conv_a, xbc_fn, ssd_fn, ffn_fn)
    return tmaj(x2.reshape(L, nb, D_MODEL)), ns_a, ns_b, ns_h, ns_f


def _layer_weights(l, g_pre1, g_post1, w_in, b_gate, w_dw_a, b_dw_a, g_ln_a, b_ln_a, w_a_out, b_a_out,
                   w_dw_b, b_dw_b, dt_bias, a_log, d_skip, g_norm_b, w_b_out, w_o,
                   g_pre2, g_post2, w_up, w_dw_f, b_dw_f, w_down):
    row = lambda v: v[l].reshape(1, -1)
    lane_pad = lambda v: jnp.pad(v, ((0, 0), (0, LANES - v.shape[1])))
    w = w_in[l]
    head = jnp.arange(LANES, dtype=jnp.int32)[:, None]
    chan_head = (jnp.arange(D_B, dtype=jnp.int32) // HEAD_DIM)[None, :]
    expand = (head == chan_head).astype(BF16)
    return {
        "g_pre1": row(g_pre1), "g_post1": row(g_post1), "g_pre2": row(g_pre2), "g_post2": row(g_post2),
        "w_main": w[:, :COL_DT].astype(BF16),
        "w_dt": lane_pad(w[:, COL_DT:COL_GATE]).astype(BF16), "w_g": w[:, COL_GATE:].astype(BF16),
        "dt_bias": lane_pad(row(dt_bias)), "b_gate": row(b_gate),
        "w_dw_a": w_dw_a[l], "b_dw_a": row(b_dw_a), "g_ln_a": row(g_ln_a), "b_ln_a": row(b_ln_a),
        "w_a_out": w_a_out[l].astype(BF16), "b_a_out": row(b_a_out),
        "w_dw_b": w_dw_b[l], "b_dw_b": row(b_dw_b),
        "a_log": lane_pad(row(a_log)), "d_skip": jnp.repeat(d_skip[l], HEAD_DIM).reshape(1, D_B),
        "g_norm_b": row(g_norm_b), "expand": expand,
        "w_b_out": w_b_out[l].astype(BF16), "w_o": w_o[l].astype(BF16),
        "w_up": w_up[l].astype(BF16), "w_dw_f": w_dw_f[l], "b_dw_f": row(b_dw_f),
        "w_down": w_down[l].astype(BF16),
    }


def kernel(x_prompt, x_sample, state_conv_a, state_conv_b, state_ssm, state_conv_ffn, meta_tokens, g_pre1, g_post1, w_in, b_gate, w_dw_a, b_dw_a, g_ln_a, b_ln_a, w_a_out, b_a_out, w_dw_b, b_dw_b, dt_bias, a_log, d_skip, g_norm_b, w_b_out, w_o, g_pre2, g_post2, w_up, w_dw_f, b_dw_f, w_down):
    depth = w_in.shape[0]
    bp = x_prompt.shape[0]
    nb = x_sample.shape[0]
    xm = meta_tokens.astype(x_prompt.dtype)[None]
    xp, xs = x_prompt, x_sample
    pa, pb, ph, pf = [], [], [], []
    sa, sb, sh, sf = [], [], [], []
    for l in range(depth):
        wts = _layer_weights(l, g_pre1, g_post1, w_in, b_gate, w_dw_a, b_dw_a, g_ln_a, b_ln_a, w_a_out, b_a_out,
                             w_dw_b, b_dw_b, dt_bias, a_log, d_skip, g_norm_b, w_b_out, w_o,
                             g_pre2, g_post2, w_up, w_dw_f, b_dw_f, w_down)
        xm, m_a, m_b, m_h, m_f = _seq_group(
            xm, wts, jnp.zeros((1, CONV_A - 1, D_A), F32), jnp.zeros((1, CONV_B - 1, D_XBC), F32),
            jnp.zeros((1, D_B, D_STATE), F32), jnp.zeros((1, CONV_F - 1, 2 * D_FF), F32), tl=N_META, q=SUBLANES)
        rep = lambda s: jnp.broadcast_to(s, (bp,) + s.shape[1:])
        xp, c_a, c_b, c_h, c_f = _seq_group(xp, wts, rep(m_a), rep(m_b), rep(m_h), rep(m_f), tl=512, q=SSD_KEYS)
        pa.append(c_a); pb.append(c_b); ph.append(c_h.reshape(bp, N_HEADS, HEAD_DIM, D_STATE)); pf.append(c_f)
        xs, d_a, d_b, d_h, d_f = _step_group(
            xs, wts, state_conv_a[l], state_conv_b[l], state_ssm[l].reshape(nb, D_B, D_STATE), state_conv_ffn[l])
        sa.append(d_a); sb.append(d_b); sh.append(d_h.reshape(nb, N_HEADS, HEAD_DIM, D_STATE)); sf.append(d_f)
    return (xp, xs, jnp.stack(pa), jnp.stack(pb), jnp.stack(ph), jnp.stack(pf),
            jnp.stack(sa), jnp.stack(sb), jnp.stack(sh), jnp.stack(sf))
```

```python
import functools

import jax
import jax.numpy as jnp
from jax import lax
from jax.experimental import pallas as pl
from jax.experimental.pallas import tpu as pltpu

D_MODEL = 2048
N_META = 16
D_A = 1024
CONV_A = 31
D_B = 2048
HEAD_DIM = 64
N_HEADS = D_B // HEAD_DIM
N_GROUPS = 8
HEADS_PER_GROUP = N_HEADS // N_GROUPS
GROUP_W = HEADS_PER_GROUP * HEAD_DIM
D_STATE = 128
CONV_B = 4
D_FF = 5632
CONV_F = 3
EPS = 1e-6
D_XBC = D_B + 2 * N_GROUPS * D_STATE
COL_Z = 2 * D_A
COL_XBC = COL_Z + D_B
COL_DT = COL_XBC + D_XBC
COL_GATE = COL_DT + N_HEADS

LANES = 128
SUBLANES = 8
SSD_KEYS = 128
PROJ_ROWS = 256
PROJ_COLS = 1024
VMEM_LIMIT = 56 * 1024 * 1024

F32 = jnp.float32
BF16 = jnp.bfloat16


def _params(*sem):
    return pltpu.CompilerParams(dimension_semantics=sem, vmem_limit_bytes=VMEM_LIMIT)


def _sigmoid(x):
    return 1.0 / (1.0 + jnp.exp(-x))


def _silu(x):
    return x * _sigmoid(x)


def _softplus(x):
    return jnp.maximum(x, 0.0) + jnp.log1p(jnp.exp(-jnp.abs(x)))


def _gelu_tanh(x):
    return 0.5 * x * (1.0 + jnp.tanh(0.7978845608028654 * (x + 0.044715 * (x * x * x))))


def _rms(x, g):
    r = lax.rsqrt(jnp.mean(x * x, axis=-1, keepdims=True) + EPS)
    return x * r * g


def _dot(a, b):
    return jnp.dot(a, b, preferred_element_type=F32)


def _rows(ref, lo, n, cols=slice(None)):
    base, off = divmod(lo, SUBLANES)
    if off == 0:
        return ref[lo:lo + n, cols]
    blk = ref[base * SUBLANES:base * SUBLANES + n + SUBLANES, cols]
    return pltpu.roll(blk, n + SUBLANES - off, 0)[0:n, :]


def _split_bf16(v, parts):
    out = []
    for _ in range(parts):
        p = v.astype(BF16)
        out.append(p)
        v = v - p.astype(F32)
    return out


def _norm_kernel(x_ref, g_ref, wdt_ref, bdt_ref, h_ref, dt_ref):
    h = _rms(x_ref[...], g_ref[...]).astype(BF16)
    h_ref[...] = h
    dt_ref[...] = _epi_dt([_dot(h, wdt_ref[...])], [bdt_ref[...]])


def _norm(x, g, w_dt, dt_bias):
    rows, d = x.shape
    tm = min(rows, 512)
    const = lambda a: pl.BlockSpec(a.shape, lambda i: (0, 0))
    return pl.pallas_call(
        _norm_kernel,
        grid=(rows // tm,),
        in_specs=[pl.BlockSpec((tm, d), lambda i: (i, 0)), const(g), const(w_dt), const(dt_bias)],
        out_specs=[pl.BlockSpec((tm, d), lambda i: (i, 0)), pl.BlockSpec((tm, LANES), lambda i: (i, 0))],
        out_shape=[jax.ShapeDtypeStruct((rows, d), BF16), jax.ShapeDtypeStruct((rows, LANES), F32)],
        compiler_params=_params("parallel"),
        name="norm_dt",
    )(x, g, w_dt, dt_bias)


def _proj_kernel(*refs, n_w, n_b, epilogue):
    h_ref = refs[0]
    w_refs = refs[1:1 + n_w]
    b_refs = refs[1 + n_w:1 + n_w + n_b]
    o_ref = refs[-1]
    bs = [b[...] for b in b_refs]
    rows = h_ref.shape[0]
    mc = min(rows, PROJ_ROWS)
    for r0 in range(0, rows, mc):
        h = h_ref[r0:r0 + mc, :]
        accs = [_dot(h, w[...]) for w in w_refs]
        o_ref[r0:r0 + mc, :] = epilogue(accs, bs).astype(o_ref.dtype)


def _proj(h, ws, bs, epilogue, tn, out_dtype, name, n=None, cols=None):
    rows, d = h.shape
    n = n or ws[0].shape[1]
    cols = cols or [0] * len(ws)
    tm = min(rows, 1024)
    kern = functools.partial(_proj_kernel, n_w=len(ws), n_b=len(bs), epilogue=epilogue)
    return pl.pallas_call(
        kern,
        grid=(rows // tm, n // tn),
        in_specs=[pl.BlockSpec((tm, d), lambda i, j: (i, 0))]
                 + [pl.BlockSpec((d, tn), lambda i, j, c=c // tn: (0, j + c)) for c in cols]
                 + [pl.BlockSpec((1, tn), lambda i, j: (0, j)) for _ in bs],
        out_specs=pl.BlockSpec((tm, tn), lambda i, j: (i, j)),
        out_shape=jax.ShapeDtypeStruct((rows, n), out_dtype),
        compiler_params=_params("parallel", "parallel"),
        name=name,
    )(h, *ws, *bs)


def _proj_conv_kernel(h_ref, w_ref, st_ref, cw_ref, cb_ref, o_ref, ns_ref, win, *carry, tm, rc):
    l = pl.program_id(1)
    j = pl.program_id(2)
    past = CONV_B - 1
    hist = SUBLANES
    if carry:
        @pl.when(l == 0)
        def _():
            win[hist - past:hist, :] = st_ref[0]

        @pl.when(l > 0)
        def _():
            win[0:hist, :] = carry[0][j]
    else:
        win[hist - past:hist, :] = st_ref[0]

    mc = min(tm, PROJ_ROWS)
    for m0 in range(0, tm, mc):
        win[hist + m0:hist + m0 + mc, :] = _dot(h_ref[0, m0:m0 + mc, :], w_ref[...])
        for r0 in range(m0, m0 + mc, rc):
            pre = cb_ref[...]
            for k in range(CONV_B):
                pre = pre + cw_ref[k:k + 1, :] * _rows(win, r0 + hist - past + k, rc)
            o_ref[0, r0:r0 + rc, :] = _silu(pre)

    ns_ref[0, 0] = win[hist + tm - past:hist + tm, :]
    if carry:
        carry[0][j] = win[tm:tm + hist, :]


def _proj_conv(h, w, col0, state, cw, cb, name):
    b, L, d = h.shape
    n = state.shape[-1]
    past = CONV_B - 1
    tn = PROJ_COLS
    tm = min(L, 1024)
    n_l, nj = L // tm, n // tn
    carry = [pltpu.VMEM((nj, SUBLANES, tn), F32)] if n_l > 1 else []
    out, ns = pl.pallas_call(
        functools.partial(_proj_conv_kernel, tm=tm, rc=min(tm, 32)),
        grid=(b, n_l, nj),
        in_specs=[pl.BlockSpec((1, tm, d), lambda i, l, j: (i, l, 0)),
                  pl.BlockSpec((d, tn), lambda i, l, j: (0, j + col0 // tn)),
                  pl.BlockSpec((1, past, tn), lambda i, l, j: (i, 0, j)),
                  pl.BlockSpec((CONV_B, tn), lambda i, l, j: (0, j)),
                  pl.BlockSpec((1, tn), lambda i, l, j: (0, j))],
        out_specs=[pl.BlockSpec((1, tm, tn), lambda i, l, j: (i, l, j)),
                   pl.BlockSpec((1, 1, past, tn), lambda i, l, j: (i, l, 0, j))],
        out_shape=[jax.ShapeDtypeStruct((b, L, n), F32), jax.ShapeDtypeStruct((b, n_l, past, n), F32)],
        scratch_shapes=[pltpu.VMEM((SUBLANES + tm, tn), F32)] + carry,
        compiler_params=_params("parallel", "arbitrary", "arbitrary"),
        name=name,
    )(h, w, state, cw, cb)
    return out, ns[:, -1]


def _epi_glu(a, b):
    return a[0] * _sigmoid(a[1])


def _epi_dt(a, b):
    lane = lax.broadcasted_iota(jnp.int32, a[0].shape, 1)
    return jnp.where(lane < N_HEADS, _softplus(a[0] + b[0]), 0.0)


def _epi_silu(a, b):
    return _silu(a[0])


def _epi_id(a, b):
    return a[0]


def _epi_gate(a, b):
    return _sigmoid(a[0] + b[0])


def _merge_kernel(ua_ref, yb_ref, gt_ref, x_ref, wa_ref, ba_ref, wb_ref, wo_ref, gp_ref, o_ref):
    ya = _dot(ua_ref[...], wa_ref[...]) + ba_ref[...]
    yb = _dot(yb_ref[...], wb_ref[...])
    mix = gt_ref[:, :D_MODEL] * ya + gt_ref[:, D_MODEL:] * yb
    m = _dot(mix.astype(BF16), wo_ref[...])
    o_ref[...] = x_ref[...] + _rms(m, gp_ref[...])


def _merge(ua, yb, gates, x, wa, ba, wb, wo, gp):
    rows = x.shape[0]
    tm = min(rows, 256)
    row_spec = lambda w: pl.BlockSpec((tm, w), lambda i: (i, 0))
    res_spec = lambda a: pl.BlockSpec(a.shape, lambda i: (0, 0), pipeline_mode=pl.Buffered(1))
    return pl.pallas_call(
        _merge_kernel,
        grid=(rows // tm,),
        in_specs=[row_spec(D_A), row_spec(D_B), row_spec(2 * D_MODEL), row_spec(D_MODEL),
                  res_spec(wa), res_spec(ba), res_spec(wb), res_spec(wo), res_spec(gp)],
        out_specs=row_spec(D_MODEL),
        out_shape=jax.ShapeDtypeStruct((rows, D_MODEL), F32),
        compiler_params=_params("parallel"),
        name="merge",
    )(ua, yb, gates, x, wa, ba, wb, wo, gp)


def _ffn_kernel(x1_ref, gpre_ref, wg_ref, wv_ref, wd_ref, stg_ref, stv_ref, cw_ref, cb_ref,
                gpost_ref, o_ref, nsg_ref, nsv_ref, h_scr, acc_scr, act_scr, win_g, win_v, *carry,
                tm, sh, hist, rc, wide):
    l = pl.program_id(1)
    j = pl.program_id(2)
    past = (CONV_F - 1) * sh
    tf = win_g.shape[1]
    here = pl.ds(pl.multiple_of(j * tf, tf), tf) if wide else slice(None)

    @pl.when(j == 0)
    def _():
        h_scr[...] = _rms(x1_ref[0], gpre_ref[...]).astype(BF16)
        acc_scr[...] = jnp.zeros_like(acc_scr)

    for s, (st_ref, win) in enumerate(((stg_ref, win_g), (stv_ref, win_v))):
        if carry:
            @pl.when(l == 0)
            def _(win=win, st_ref=st_ref):
                win[hist - past:hist, :] = st_ref[0, :, here]

            @pl.when(l > 0)
            def _(win=win, s=s):
                win[0:hist, :] = carry[0][j, s]
        else:
            win[hist - past:hist, :] = st_ref[0, :, here]

    col_g = pl.multiple_of(j * tf, tf)
    col_v = pl.multiple_of(D_FF + j * tf, tf)
    cwg, cbg = cw_ref[:, pl.ds(col_g, tf)], cb_ref[:, pl.ds(col_g, tf)]
    cwv, cbv = cw_ref[:, pl.ds(col_v, tf)], cb_ref[:, pl.ds(col_v, tf)]

    def conv(win, cw_t, cb_t, r0, cs):
        out = cb_t[:, cs]
        for k in range(CONV_F):
            out = out + cw_t[k:k + 1, cs] * _rows(win, hist - (CONV_F - 1 - k) * sh + r0, rc, cs)
        return out

    cw = min(tf, 256)
    mc = min(tm, PROJ_ROWS)
    for c0 in range(0, tf, cw):
        cs = slice(c0, c0 + cw)
        for m0 in range(0, tm, mc):
            win_g[hist + m0:hist + m0 + mc, cs] = _dot(h_scr[m0:m0 + mc, :], wg_ref[:, cs])
            win_v[hist + m0:hist + m0 + mc, cs] = _dot(h_scr[m0:m0 + mc, :], wv_ref[:, cs])
            for r0 in range(m0, m0 + mc, rc):
                gate = _gelu_tanh(conv(win_g, cwg, cbg, r0, cs))
                act_scr[r0:r0 + rc, cs] = (gate * conv(win_v, cwv, cbv, r0, cs)).astype(BF16)
    acc_scr[...] += _dot(act_scr[...], wd_ref[...])

    for s, (ns_ref, win) in enumerate(((nsg_ref, win_g), (nsv_ref, win_v))):
        ns_ref[0, 0, :, here] = win[hist + tm - past:hist + tm, :]
        if carry:
            carry[0][j, s] = win[tm:tm + hist, :]

    @pl.when(j == pl.num_programs(2) - 1)
    def _():
        o_ref[0] = x1_ref[0] + _rms(acc_scr[...], gpost_ref[...])


def _ffn(x1, state, wts, *, tm, sh):
    b, L, _ = x1.shape
    tf = 512
    nj = D_FF // tf
    past = (CONV_F - 1) * sh
    hist = -(-past // SUBLANES) * SUBLANES
    n_l = L // tm
    carry = [pltpu.VMEM((nj, 2, hist, tf), F32)] if n_l > 1 else []
    col = lambda rows, off: pl.BlockSpec((rows, tf), lambda i, l, j: (0, j + off))
    wide = sh == 1
    if wide:
        st = lambda off: pl.BlockSpec((1, past, D_FF), lambda i, l, j: (i, 0, off // nj))
        tail = pl.BlockSpec((1, 1, past, D_FF), lambda i, l, j: (i, l, 0, 0))
    else:
        st = lambda off: pl.BlockSpec((1, past, tf), lambda i, l, j: (i, 0, j + off))
        tail = pl.BlockSpec((1, 1, past, tf), lambda i, l, j: (i, l, 0, j))
    vec = pl.BlockSpec((1, D_MODEL), lambda i, l, j: (0, 0))
    const = lambda a: pl.BlockSpec(a.shape, lambda i, l, j: (0, 0))
    xblk = pl.BlockSpec((1, tm, D_MODEL), lambda i, l, j: (i, l, 0))
    out, ns_g, ns_v = pl.pallas_call(
        functools.partial(_ffn_kernel, tm=tm, sh=sh, hist=hist, rc=min(tm, 32), wide=wide),
        grid=(b, n_l, nj),
        in_specs=[xblk, vec, col(D_MODEL, 0), col(D_MODEL, nj),
                  pl.BlockSpec((tf, D_MODEL), lambda i, l, j: (j, 0)),
                  st(0), st(nj), const(wts["w_dw_f"]), const(wts["b_dw_f"]), vec],
        out_specs=[xblk, tail, tail],
        out_shape=[jax.ShapeDtypeStruct((b, L, D_MODEL), F32),
                   jax.ShapeDtypeStruct((b, n_l, past, D_FF), F32),
                   jax.ShapeDtypeStruct((b, n_l, past, D_FF), F32)],
        scratch_shapes=[pltpu.VMEM((tm, D_MODEL), BF16), pltpu.VMEM((tm, D_MODEL), F32),
                        pltpu.VMEM((tm, tf), BF16),
                        pltpu.VMEM((hist + tm, tf), F32), pltpu.VMEM((hist + tm, tf), F32)] + carry,
        compiler_params=_params("parallel", "arbitrary", "arbitrary"),
        name="ffn",
    )(x1, wts["g_pre2"], wts["w_up"], wts["w_up"], wts["w_down"], state, state,
      wts["w_dw_f"], wts["b_dw_f"], wts["g_post2"])
    return out, jnp.concatenate([ns_g[:, -1], ns_v[:, -1]], axis=-1)


def _post_ln_silu(outs, extras):
    u = outs[0]
    mu = jnp.mean(u, axis=-1, keepdims=True)
    xc = u - mu
    r = lax.rsqrt(jnp.mean(xc * xc, axis=-1, keepdims=True) + EPS)
    return _silu(xc * r * extras[0] + extras[1])


def _post_silu(outs, extras):
    return _silu(outs[0])


def _conv_seq_kernel(*refs, width, tl, n_s, n_x, post, hist, rc):
    u_refs = refs[0:n_s]
    st_refs = refs[n_s:2 * n_s]
    w_refs = refs[2 * n_s:3 * n_s]
    b_refs = refs[3 * n_s:4 * n_s]
    x_refs = refs[4 * n_s:4 * n_s + n_x]
    o_ref = refs[4 * n_s + n_x]
    ns_refs = refs[4 * n_s + n_x + 1:4 * n_s + n_x + 1 + n_s]
    win_refs = refs[4 * n_s + n_x + 1 + n_s:4 * n_s + n_x + 1 + 2 * n_s]
    shf_refs = refs[4 * n_s + n_x + 1 + 2 * n_s:]
    past = width - 1
    l = pl.program_id(2)
    n_shf = hist + tl - SUBLANES

    for s in range(n_s):
        win = win_refs[s]

        @pl.when(l == 0)
        def _(win=win, s=s):
            if hist > past:
                win[0:hist - past, :] = jnp.zeros((hist - past, win.shape[1]), F32)
            win[hist - past:hist, :] = st_refs[s][0]

        @pl.when(l > 0)
        def _(win=win):
            win[0:hist, :] = win[tl:tl + hist, :]

        win[hist:hist + tl, :] = u_refs[s][0]
        for r in range(1, SUBLANES):
            shf_refs[s][r - 1, :, :] = _rows(win, r, n_shf)

    def tap(s, lo):
        a, r = divmod(lo, SUBLANES)
        if r == 0:
            return win_refs[s][lo:lo + rc, :]
        return shf_refs[s][r - 1, a * SUBLANES:a * SUBLANES + rc, :]

    extras = [x[...] for x in x_refs]
    for c in range(tl // rc):
        outs = []
        for s in range(n_s):
            acc = b_refs[s][...] + w_refs[s][0:1, :] * tap(s, c * rc + hist - past)
            for k in range(1, width):
                acc = acc + w_refs[s][k:k + 1, :] * tap(s, c * rc + hist - past + k)
            outs.append(acc)
        o_ref[0, c * rc:(c + 1) * rc, :] = post(outs, extras).astype(o_ref.dtype)

    @pl.when(l == pl.num_programs(2) - 1)
    def _():
        for s in range(n_s):
            ns_refs[s][0] = win_refs[s][hist + tl - past:hist + tl, :]


def _conv_seq(u, state, w, bias, extras, post, *, width, ct, tl, n_s, out_w, out_dtype, name):
    b, L, _ = u.shape
    past = width - 1
    hist = -(-past // SUBLANES) * SUBLANES
    nj = out_w // ct
    rc = min(tl, 64)
    kern = functools.partial(_conv_seq_kernel, width=width, tl=tl, n_s=n_s, n_x=len(extras),
                             post=post, hist=hist, rc=rc)
    in_specs = ([pl.BlockSpec((1, tl, ct), lambda i, j, l, s=s: (i, l, j + s * nj)) for s in range(n_s)]
                + [pl.BlockSpec((1, past, ct), lambda i, j, l, s=s: (i, 0, j + s * nj)) for s in range(n_s)]
                + [pl.BlockSpec((width, ct), lambda i, j, l, s=s: (0, j + s * nj)) for s in range(n_s)]
                + [pl.BlockSpec((1, ct), lambda i, j, l, s=s: (0, j + s * nj)) for s in range(n_s)]
                + [pl.BlockSpec((1, ct), lambda i, j, l: (0, j)) for _ in extras])
    out_specs = ([pl.BlockSpec((1, tl, ct), lambda i, j, l: (i, l, j))]
                 + [pl.BlockSpec((1, past, ct), lambda i, j, l: (i, 0, j)) for _ in range(n_s)])
    out_shape = ([jax.ShapeDtypeStruct((b, L, out_w), out_dtype)]
                 + [jax.ShapeDtypeStruct((b, past, out_w), F32) for _ in range(n_s)])
    res = pl.pallas_call(
        kern,
        grid=(b, nj, L // tl),
        in_specs=in_specs,
        out_specs=out_specs,
        out_shape=out_shape,
        scratch_shapes=[pltpu.VMEM((hist + tl, ct), F32) for _ in range(n_s)]
                       + [pltpu.VMEM((SUBLANES - 1, hist + tl - SUBLANES, ct), F32) for _ in range(n_s)],
        compiler_params=_params("parallel", "parallel", "arbitrary"),
        name=name,
    )(*([u] * n_s), *([state] * n_s), *([w] * n_s), *([bias] * n_s), *extras)
    return res[0], res[1:]


def _conv_slab_kernel(*refs, width, steps, n_s, n_x, post):
    u_refs = refs[0:n_s]
    st_refs = refs[n_s:2 * n_s]
    w_refs = refs[2 * n_s:3 * n_s]
    b_refs = refs[3 * n_s:4 * n_s]
    x_refs = refs[4 * n_s:4 * n_s + n_x]
    o_ref = refs[4 * n_s + n_x]
    ns_refs = refs[4 * n_s + n_x + 1:]
    past = width - 1

    def slab(s, i):
        return st_refs[s][i] if i < past else u_refs[s][i - past]

    extras = [x[...] for x in x_refs]
    for t in range(steps):
        outs = []
        for s in range(n_s):
            acc = b_refs[s][...] + w_refs[s][0:1, :] * slab(s, t)
            for k in range(1, width):
                acc = acc + w_refs[s][k:k + 1, :] * slab(s, t + k)
            outs.append(acc)
        o_ref[t] = post(outs, extras).astype(o_ref.dtype)
    for s in range(n_s):
        for i in range(past):
            ns_refs[s][i] = slab(s, i + steps)


def _conv_slab(u, state, w, bias, extras, post, *, width, ct, nbt, n_s, out_w, out_dtype, name):
    steps, nb, _ = u.shape
    past = width - 1
    nj = out_w // ct
    kern = functools.partial(_conv_slab_kernel, width=width, steps=steps, n_s=n_s, n_x=len(extras), post=post)
    in_specs = ([pl.BlockSpec((steps, nbt, ct), lambda i, j, s=s: (0, i, j + s * nj)) for s in range(n_s)]
                + [pl.BlockSpec((past, nbt, ct), lambda i, j, s=s: (0, i, j + s * nj)) for s in range(n_s)]
                + [pl.BlockSpec((width, ct), lambda i, j, s=s: (0, j + s * nj)) for s in range(n_s)]
                + [pl.BlockSpec((1, ct), lambda i, j, s=s: (0, j + s * nj)) for s in range(n_s)]
                + [pl.BlockSpec((1, ct), lambda i, j: (0, j)) for _ in extras])
    out_specs = ([pl.BlockSpec((steps, nbt, ct), lambda i, j: (0, i, j))]
                 + [pl.BlockSpec((past, nbt, ct), lambda i, j: (0, i, j)) for _ in range(n_s)])
    out_shape = ([jax.ShapeDtypeStruct((steps, nb, out_w), out_dtype)]
                 + [jax.ShapeDtypeStruct((past, nb, out_w), F32) for _ in range(n_s)])
    res = pl.pallas_call(
        kern,
        grid=(nb // nbt, nj),
        in_specs=in_specs,
        out_specs=out_specs,
        out_shape=out_shape,
        compiler_params=_params("parallel", "parallel"),
        name=name,
    )(*([u] * n_s), *([state] * n_s), *([w] * n_s), *([bias] * n_s), *extras)
    return res[0], res[1:]


def _ssd_kernel(xbc_ref, dt_ref, zs_ref, h0_ref, alog_ref, dsk_ref, gn_ref, e_ref, y_ref, ht_ref, h_scr, y_scr,
                *, q, bb):
    qk = SSD_KEYS
    c = pl.program_id(1)

    @pl.when(c == 0)
    def _():
        h_scr[...] = h0_ref[...]

    a = -jnp.exp(alog_ref[...])
    e = e_ref[...]
    row = lax.broadcasted_iota(jnp.int32, (qk, qk), 0)
    col = lax.broadcasted_iota(jnp.int32, (qk, qk), 1)
    tril = jnp.where(row >= col, 1.0, 0.0).astype(BF16)
    causal = row[:q, :] >= col[:q, :]
    key_head = lax.broadcasted_iota(jnp.int32, (qk, GROUP_W), 1) // HEAD_DIM

    def expand(vs):
        v = jnp.concatenate(vs, axis=0) if len(vs) > 1 else vs[0]
        hi = v.astype(BF16)
        lo = (v - hi.astype(F32)).astype(BF16)
        out = _dot(hi, e) + _dot(lo, e)
        n = vs[0].shape[0]
        return [out[i * n:(i + 1) * n, :] for i in range(len(vs))]

    def pad_keys(v):
        if q == qk:
            return v
        return jnp.concatenate([v, jnp.zeros((qk - q, v.shape[1]), v.dtype)], axis=0)

    seqs = range(bb)
    xqs = [xbc_ref[s] for s in seqs]
    xs = [xq[:, :D_B] for xq in xqs]
    bms = [pad_keys(xq[:, D_B:D_B + N_GROUPS * D_STATE]).astype(BF16) for xq in xqs]
    cms = [xq[:, D_B + N_GROUPS * D_STATE:].astype(BF16) for xq in xqs]
    dts = [dt_ref[s] for s in seqs]
    cums = [sum(_dot(tril, p) for p in _split_bf16(pad_keys(dt * a), 3)) for dt in dts]
    cum_ts = [cum.T for cum in cums]
    cum_qs = [cum[:q, :] for cum in cums]
    xdts = [x * d for x, d in zip(xs, expand(dts))]
    chunk_decay = [jnp.exp(cum[qk - 1:qk, :]) for cum in cums]
    to_end = [jnp.exp(cum[qk - 1:qk, :] - cum_q) for cum, cum_q in zip(cums, cum_qs)]
    wide = expand(to_end + [jnp.exp(cum_q) for cum_q in cum_qs])
    xws = [pad_keys(xdt * w).astype(BF16) for xdt, w in zip(xdts, wide[:bb])]
    xdts = [pad_keys(xdt).astype(BF16) for xdt in xdts]
    ecums = wide[bb:]

    for g in range(N_GROUPS):
        cols = slice(g * GROUP_W, (g + 1) * GROUP_W)
        for s in seqs:
            bg = bms[s][:, g * D_STATE:(g + 1) * D_STATE]
            cg = cms[s][:, g * D_STATE:(g + 1) * D_STATE]
            cb = lax.dot_general(cg, bg, (((1,), (1,)), ((), ())), preferred_element_type=F32)
            xdt_g = xdts[s][:, cols]
            scores, keys = [], []
            for r in range(HEADS_PER_GROUP):
                hd = g * HEADS_PER_GROUP + r
                seg = jnp.where(causal, cum_qs[s][:, hd:hd + 1] - cum_ts[s][hd:hd + 1, :], -1e30)
                scores.append((cb * jnp.exp(seg)).astype(BF16))
                keys.append(jnp.where(key_head == r, xdt_g, jnp.zeros_like(xdt_g)))
            y_g = _dot(jnp.concatenate(scores, axis=1), jnp.concatenate(keys, axis=0))
            h_g = h_scr[s, cols, :]
            y_inter = lax.dot_general(cg, h_g.astype(BF16), (((1,), (1,)), ((), ())),
                                      preferred_element_type=F32)
            y_scr[s, :, cols] = y_g + y_inter * ecums[s][:, cols]
            s_g = lax.dot_general(xws[s][:, cols], bg, (((0,), (0,)), ((), ())), preferred_element_type=F32)
            decay = jnp.concatenate(
                [jnp.broadcast_to(chunk_decay[s][:, g * HEADS_PER_GROUP + r:g * HEADS_PER_GROUP + r + 1],
                                  (HEAD_DIM, D_STATE)) for r in range(HEADS_PER_GROUP)], axis=0)
            h_scr[s, cols, :] = decay * h_g + s_g

    for s in seqs:
        y = y_scr[s] + dsk_ref[...] * xs[s]
        y_ref[s] = _rms(y * zs_ref[s], gn_ref[...]).astype(y_ref.dtype)

    @pl.when(c == pl.num_programs(1) - 1)
    def _():
        ht_ref[...] = h_scr[...]


def _ssd(xbc, dt, zs, h0, alog, dsk, gn, e, *, q, bb):
    b, L, _ = xbc.shape
    seq = lambda w: pl.BlockSpec((bb, q, w), lambda i, c: (i, c, 0))
    full = lambda arr: pl.BlockSpec(arr.shape, lambda i, c: (0,) * arr.ndim)
    state = pl.BlockSpec((bb, D_B, D_STATE), lambda i, c: (i, 0, 0))
    return pl.pallas_call(
        functools.partial(_ssd_kernel, q=q, bb=bb),
        grid=(b // bb, L // q),
        in_specs=[seq(D_XBC), seq(LANES), seq(D_B), state, full(alog), full(dsk), full(gn), full(e)],
        out_specs=[seq(D_B), state],
        out_shape=[jax.ShapeDtypeStruct((b, L, D_B), BF16), jax.ShapeDtypeStruct((b, D_B, D_STATE), F32)],
        scratch_shapes=[pltpu.VMEM((bb, D_B, D_STATE), F32), pltpu.VMEM((bb, q, D_B), F32)],
        compiler_params=_params("parallel", "arbitrary"),
        name="ssd",
    )(xbc, dt, zs, h0, alog, dsk, gn, e)


def _layer(x, wts, conv_a, xbc_fn, ssd_fn, ffn_fn):
    h, dt = _norm(x, wts["g_pre1"], wts["w_dt"], wts["dt_bias"])
    w_main = wts["w_main"]
    uglu = _proj(h, [w_main, w_main], [], _epi_glu, PROJ_COLS, F32, "in_glu", n=D_A, cols=[0, D_A])
    zs = _proj(h, [w_main], [], _epi_silu, PROJ_COLS, F32, "in_z", n=D_B, cols=[COL_Z])
    gates = _proj(h, [wts["w_g"]], [wts["b_gate"]], _epi_gate, PROJ_COLS, F32, "in_gates")

    ua, st_a = conv_a(uglu)
    xbc_act, st_b = xbc_fn(h)
    yb, st_h = ssd_fn(xbc_act, dt, zs)

    x1 = _merge(ua, yb, gates, x, wts["w_a_out"], wts["b_a_out"], wts["w_b_out"], wts["w_o"], wts["g_post1"])
    x2, st_f = ffn_fn(x1)
    return x2, st_a, st_b, st_h, st_f


def _seq_group(x, wts, st_a, st_b, st_h, st_f, tl, q):
    b, L, _ = x.shape

    def conv_a(uglu):
        ua, (ns,) = _conv_seq(uglu.reshape(b, L, D_A), st_a, wts["w_dw_a"], wts["b_dw_a"],
                              [wts["g_ln_a"], wts["b_ln_a"]], _post_ln_silu, width=CONV_A, ct=D_A,
                              tl=min(tl, 256), n_s=1, out_w=D_A, out_dtype=BF16, name="conv_a")
        return ua.reshape(b * L, D_A), ns

    def xbc_fn(h):
        return _proj_conv(h.reshape(b, L, D_MODEL), wts["w_main"], COL_XBC, st_b, wts["w_dw_b"], wts["b_dw_b"],
                          "in_xbc_conv")

    def ssd_fn(xbc_act, dt, zs):
        yb, ht = _ssd(xbc_act, dt.reshape(b, L, LANES), zs.reshape(b, L, D_B), st_h,
                      wts["a_log"], wts["d_skip"], wts["g_norm_b"], wts["expand"], q=q, bb=1)
        return yb.reshape(b * L, D_B), ht

    def ffn_fn(x1):
        x2, ns = _ffn(x1.reshape(b, L, D_MODEL), st_f, wts, tm=min(L, 512), sh=1)
        return x2.reshape(b * L, D_MODEL), ns

    x2, ns_a, ns_b, ns_h, ns_f = _layer(x.reshape(b * L, D_MODEL), wts, conv_a, xbc_fn, ssd_fn, ffn_fn)
    return x2.reshape(b, L, D_MODEL), ns_a, ns_b, ns_h, ns_f


def _step_group(x, wts, st_a, st_b, st_h, st_f):
    nb, L, _ = x.shape
    tmaj = lambda s: jnp.transpose(s, (1, 0, 2))
    lpad = -(-L // SUBLANES) * SUBLANES

    def conv_a(uglu):
        ua, (ns,) = _conv_slab(uglu.reshape(L, nb, D_A), tmaj(st_a), wts["w_dw_a"], wts["b_dw_a"],
                               [wts["g_ln_a"], wts["b_ln_a"]], _post_ln_silu, width=CONV_A, ct=D_A,
                               nbt=32, n_s=1, out_w=D_A, out_dtype=BF16, name="conv_a_step")
        return ua.reshape(L * nb, D_A), tmaj(ns)

    def bmaj_pad(v):
        return jnp.pad(tmaj(v), ((0, 0), (0, lpad - L), (0, 0)))

    def xbc_fn(h):
        xbc = _proj(h, [wts["w_main"]], [], _epi_id, PROJ_COLS, F32, "in_xbc", n=D_XBC, cols=[COL_XBC])
        act, (ns,) = _conv_slab(xbc.reshape(L, nb, D_XBC), tmaj(st_b), wts["w_dw_b"], wts["b_dw_b"], [],
                                _post_silu, width=CONV_B, ct=512, nbt=nb, n_s=1, out_w=D_XBC,
                                out_dtype=F32, name="conv_b_step")
        return act, tmaj(ns)

    def ssd_fn(xbc_act, dt, zs):
        yb, ht = _ssd(bmaj_pad(xbc_act), bmaj_pad(dt.reshape(L, nb, LANES)), bmaj_pad(zs.reshape(L, nb, D_B)),
                      st_h, wts["a_log"], wts["d_skip"], wts["g_norm_b"], wts["expand"], q=lpad, bb=4)
        return tmaj(yb[:, :L]).reshape(L * nb, D_B), ht

    def ffn_fn(x1):
        past = CONV_F - 1
        x2, ns = _ffn(x1.reshape(1, L * nb, D_MODEL), tmaj(st_f).reshape(1, past * nb, 2 * D_FF), wts,
                      tm=L * nb, sh=nb)
        return x2.reshape(L * nb, D_MODEL), tmaj(ns.reshape(past, nb, 2 * D_FF))

    x2, ns_a, ns_b, ns_h, ns_f = _layer(tmaj(x).reshape(L * nb, D_MODEL), wts, conv_a, xbc_fn, ssd_fn, ffn_fn)
    return tmaj(x2.reshape(L, nb, D_MODEL)), ns_a, ns_b, ns_h, ns_f


def _layer_weights(l, g_pre1, g_post1, w_in, b_gate, w_dw_a, b_dw_a, g_ln_a, b_ln_a, w_a_out, b_a_out,
                   w_dw_b, b_dw_b, dt_bias, a_log, d_skip, g_norm_b, w_b_out, w_o,
                   g_pre2, g_post2, w_up, w_dw_f, b_dw_f, w_down):
    row = lambda v: v[l].reshape(1, -1)
    lane_pad = lambda v: jnp.pad(v, ((0, 0), (0, LANES - v.shape[1])))
    w = w_in[l]
    head = jnp.arange(LANES, dtype=jnp.int32)[:, None]
    chan_head = (jnp.arange(D_B, dtype=jnp.int32) // HEAD_DIM)[None, :]
    expand = (head == chan_head).astype(BF16)
    return {
        "g_pre1": row(g_pre1), "g_post1": row(g_post1), "g_pre2": row(g_pre2), "g_post2": row(g_post2),
        "w_main": w[:, :COL_DT].astype(BF16),
        "w_dt": lane_pad(w[:, COL_DT:COL_GATE]).astype(BF16), "w_g": w[:, COL_GATE:].astype(BF16),
        "dt_bias": lane_pad(row(dt_bias)), "b_gate": row(b_gate),
        "w_dw_a": w_dw_a[l], "b_dw_a": row(b_dw_a), "g_ln_a": row(g_ln_a), "b_ln_a": row(b_ln_a),
        "w_a_out": w_a_out[l].astype(BF16), "b_a_out": row(b_a_out),
        "w_dw_b": w_dw_b[l], "b_dw_b": row(b_dw_b),
        "a_log": lane_pad(row(a_log)), "d_skip": jnp.repeat(d_skip[l], HEAD_DIM).reshape(1, D_B),
        "g_norm_b": row(g_norm_b), "expand": expand,
        "w_b_out": w_b_out[l].astype(BF16), "w_o": w_o[l].astype(BF16),
        "w_up": w_up[l].astype(BF16), "w_dw_f": w_dw_f[l], "b_dw_f": row(b_dw_f),
        "w_down": w_down[l].astype(BF16),
    }


def kernel(x_prompt, x_sample, state_conv_a, state_conv_b, state_ssm, state_conv_ffn, meta_tokens, g_pre1, g_post1, w_in, b_gate, w_dw_a, b_dw_a, g_ln_a, b_ln_a, w_a_out, b_a_out, w_dw_b, b_dw_b, dt_bias, a_log, d_skip, g_norm_b, w_b_out, w_o, g_pre2, g_post2, w_up, w_dw_f, b_dw_f, w_down):
    depth = w_in.shape[0]
    bp = x_prompt.shape[0]
    nb = x_sample.shape[0]
    xm = meta_tokens.astype(x_prompt.dtype)[None]
    xp, xs = x_prompt, x_sample
    pa, pb, ph, pf = [], [], [], []
    sa, sb, sh, sf = [], [], [], []
    for l in range(depth):
        wts = _layer_weights(l, g_pre1, g_post1, w_in, b_gate, w_dw_a, b_dw_a, g_ln_a, b_ln_a, w_a_out, b_a_out,
                             w_dw_b, b_dw_b, dt_bias, a_log, d_skip, g_norm_b, w_b_out, w_o,
                             g_pre2, g_post2, w_up, w_dw_f, b_dw_f, w_down)
        xm, m_a, m_b, m_h, m_f = _seq_group(
            xm, wts, jnp.zeros((1, CONV_A - 1, D_A), F32), jnp.zeros((1, CONV_B - 1, D_XBC), F32),
            jnp.zeros((1, D_B, D_STATE), F32), jnp.zeros((1, CONV_F - 1, 2 * D_FF), F32), tl=N_META, q=SUBLANES)
        rep = lambda s: jnp.broadcast_to(s, (bp,) + s.shape[1:])
        xp, c_a, c_b, c_h, c_f = _seq_group(xp, wts, rep(m_a), rep(m_b), rep(m_h), rep(m_f), tl=512, q=SSD_KEYS)
        pa.append(c_a); pb.append(c_b); ph.append(c_h.reshape(bp, N_HEADS, HEAD_DIM, D_STATE)); pf.append(c_f)
        xs, d_a, d_b, d_h, d_f = _step_group(
            xs, wts, state_conv_a[l], state_conv_b[l], state_ssm[l].reshape(nb, D_B, D_STATE), state_conv_ffn[l])
        sa.append(d_a); sb.append(d_b); sh.append(d_h.reshape(nb, N_HEADS, HEAD_DIM, D_STATE)); sf.append(d_f)
    return (xp, xs, jnp.stack(pa), jnp.stack(pb), jnp.stack(ph), jnp.stack(pf),
            jnp.stack(sa), jnp.stack(sb), jnp.stack(sh), jnp.stack(sf))
```

```python
import functools

import jax
import jax.numpy as jnp
from jax import lax
from jax.experimental import pallas as pl
from jax.experimental.pallas import tpu as pltpu

D_MODEL = 2048
N_META = 16
D_A = 1024
CONV_A = 31
D_B = 2048
HEAD_DIM = 64
N_HEADS = D_B // HEAD_DIM
N_GROUPS = 8
HEADS_PER_GROUP = N_HEADS // N_GROUPS
GROUP_W = HEADS_PER_GROUP * HEAD_DIM
D_STATE = 128
CONV_B = 4
D_FF = 5632
CONV_F = 3
EPS = 1e-6
D_XBC = D_B + 2 * N_GROUPS * D_STATE
COL_Z = 2 * D_A
COL_XBC = COL_Z + D_B
COL_DT = COL_XBC + D_XBC
COL_GATE = COL_DT + N_HEADS

LANES = 128
SUBLANES = 8
SSD_KEYS = 128
PROJ_ROWS = 256
PROJ_COLS = 1024
VMEM_LIMIT = 56 * 1024 * 1024

F32 = jnp.float32
BF16 = jnp.bfloat16


def _params(*sem):
    return pltpu.CompilerParams(dimension_semantics=sem, vmem_limit_bytes=VMEM_LIMIT)


def _sigmoid(x):
    return 1.0 / (1.0 + jnp.exp(-x))


def _silu(x):
    return x * _sigmoid(x)


def _softplus(x):
    return jnp.maximum(x, 0.0) + jnp.log1p(jnp.exp(-jnp.abs(x)))


def _gelu_tanh(x):
    return 0.5 * x * (1.0 + jnp.tanh(0.7978845608028654 * (x + 0.044715 * (x * x * x))))


def _rms(x, g):
    r = lax.rsqrt(jnp.mean(x * x, axis=-1, keepdims=True) + EPS)
    return x * r * g


def _dot(a, b):
    return jnp.dot(a, b, preferred_element_type=F32)


def _rows(ref, lo, n, cols=slice(None)):
    base, off = divmod(lo, SUBLANES)
    if off == 0:
        return ref[lo:lo + n, cols]
    blk = ref[base * SUBLANES:base * SUBLANES + n + SUBLANES, cols]
    return pltpu.roll(blk, n + SUBLANES - off, 0)[0:n, :]


def _split_bf16(v, parts):
    out = []
    for _ in range(parts):
        p = v.astype(BF16)
        out.append(p)
        v = v - p.astype(F32)
    return out


def _norm_kernel(x_ref, g_ref, wdt_ref, bdt_ref, h_ref, dt_ref):
    h = _rms(x_ref[...], g_ref[...]).astype(BF16)
    h_ref[...] = h
    dt_ref[...] = _epi_dt([_dot(h, wdt_ref[...])], [bdt_ref[...]])


def _norm(x, g, w_dt, dt_bias):
    rows, d = x.shape
    tm = min(rows, 512)
    const = lambda a: pl.BlockSpec(a.shape, lambda i: (0, 0))
    return pl.pallas_call(
        _norm_kernel,
        grid=(rows // tm,),
        in_specs=[pl.BlockSpec((tm, d), lambda i: (i, 0)), const(g), const(w_dt), const(dt_bias)],
        out_specs=[pl.BlockSpec((tm, d), lambda i: (i, 0)), pl.BlockSpec((tm, LANES), lambda i: (i, 0))],
        out_shape=[jax.ShapeDtypeStruct((rows, d), BF16), jax.ShapeDtypeStruct((rows, LANES), F32)],
        compiler_params=_params("parallel"),
        name="norm_dt",
    )(x, g, w_dt, dt_bias)


def _proj_kernel(*refs, n_w, n_b, epilogue):
    h_ref = refs[0]
    w_refs = refs[1:1 + n_w]
    b_refs = refs[1 + n_w:1 + n_w + n_b]
    o_ref = refs[-1]
    bs = [b[...] for b in b_refs]
    rows = h_ref.shape[0]
    mc = min(rows, PROJ_ROWS)
    for r0 in range(0, rows, mc):
        h = h_ref[r0:r0 + mc, :]
        accs = [_dot(h, w[...]) for w in w_refs]
        o_ref[r0:r0 + mc, :] = epilogue(accs, bs).astype(o_ref.dtype)


def _proj(h, ws, bs, epilogue, tn, out_dtype, name, n=None, cols=None):
    rows, d = h.shape
    n = n or ws[0].shape[1]
    cols = cols or [0] * len(ws)
    tm = min(rows, 1024)
    kern = functools.partial(_proj_kernel, n_w=len(ws), n_b=len(bs), epilogue=epilogue)
    return pl.pallas_call(
        kern,
        grid=(rows // tm, n // tn),
        in_specs=[pl.BlockSpec((tm, d), lambda i, j: (i, 0))]
                 + [pl.BlockSpec((d, tn), lambda i, j, c=c // tn: (0, j + c)) for c in cols]
                 + [pl.BlockSpec((1, tn), lambda i, j: (0, j)) for _ in bs],
        out_specs=pl.BlockSpec((tm, tn), lambda i, j: (i, j)),
        out_shape=jax.ShapeDtypeStruct((rows, n), out_dtype),
        compiler_params=_params("parallel", "parallel"),
        name=name,
    )(h, *ws, *bs)


def _proj_conv_kernel(h_ref, w_ref, st_ref, cw_ref, cb_ref, o_ref, ns_ref, win, *carry, tm, rc):
    l = pl.program_id(1)
    j = pl.program_id(2)
    past = CONV_B - 1
    hist = SUBLANES
    if carry:
        @pl.when(l == 0)
        def _():
            win[hist - past:hist, :] = st_ref[0]

        @pl.when(l > 0)
        def _():
            win[0:hist, :] = carry[0][j]
    else:
        win[hist - past:hist, :] = st_ref[0]

    mc = min(tm, PROJ_ROWS)
    for m0 in range(0, tm, mc):
        win[hist + m0:hist + m0 + mc, :] = _dot(h_ref[0, m0:m0 + mc, :], w_ref[...])
        for r0 in range(m0, m0 + mc, rc):
            pre = cb_ref[...]
            for k in range(CONV_B):
                pre = pre + cw_ref[k:k + 1, :] * _rows(win, r0 + hist - past + k, rc)
            o_ref[0, r0:r0 + rc, :] = _silu(pre)

    ns_ref[0, 0] = win[hist + tm - past:hist + tm, :]
    if carry:
        carry[0][j] = win[tm:tm + hist, :]


def _proj_conv(h, w, col0, state, cw, cb, name):
    b, L, d = h.shape
    n = state.shape[-1]
    past = CONV_B - 1
    tn = PROJ_COLS
    tm = min(L, 1024)
    n_l, nj = L // tm, n // tn
    carry = [pltpu.VMEM((nj, SUBLANES, tn), F32)] if n_l > 1 else []
    out, ns = pl.pallas_call(
        functools.partial(_proj_conv_kernel, tm=tm, rc=min(tm, 32)),
        grid=(b, n_l, nj),
        in_specs=[pl.BlockSpec((1, tm, d), lambda i, l, j: (i, l, 0)),
                  pl.BlockSpec((d, tn), lambda i, l, j: (0, j + col0 // tn)),
                  pl.BlockSpec((1, past, tn), lambda i, l, j: (i, 0, j)),
                  pl.BlockSpec((CONV_B, tn), lambda i, l, j: (0, j)),
                  pl.BlockSpec((1, tn), lambda i, l, j: (0, j))],
        out_specs=[pl.BlockSpec((1, tm, tn), lambda i, l, j: (i, l, j)),
                   pl.BlockSpec((1, 1, past, tn), lambda i, l, j: (i, l, 0, j))],
        out_shape=[jax.ShapeDtypeStruct((b, L, n), F32), jax.ShapeDtypeStruct((b, n_l, past, n), F32)],
        scratch_shapes=[pltpu.VMEM((SUBLANES + tm, tn), F32)] + carry,
        compiler_params=_params("parallel", "arbitrary", "arbitrary"),
        name=name,
    )(h, w, state, cw, cb)
    return out, ns[:, -1]


def _epi_glu(a, b):
    return a[0] * _sigmoid(a[1])


def _epi_dt(a, b):
    lane = lax.broadcasted_iota(jnp.int32, a[0].shape, 1)
    return jnp.where(lane < N_HEADS, _softplus(a[0] + b[0]), 0.0)


def _epi_silu(a, b):
    return _silu(a[0])


def _epi_id(a, b):
    return a[0]


def _epi_gate(a, b):
    return _sigmoid(a[0] + b[0])


def _merge_kernel(ua_ref, yb_ref, gt_ref, x_ref, wa_ref, ba_ref, wb_ref, wo_ref, gp_ref, o_ref):
    ya = _dot(ua_ref[...], wa_ref[...]) + ba_ref[...]
    yb = _dot(yb_ref[...], wb_ref[...])
    mix = gt_ref[:, :D_MODEL] * ya + gt_ref[:, D_MODEL:] * yb
    m = _dot(mix.astype(BF16), wo_ref[...])
    o_ref[...] = x_ref[...] + _rms(m, gp_ref[...])


def _merge(ua, yb, gates, x, wa, ba, wb, wo, gp):
    rows = x.shape[0]
    tm = min(rows, 256)
    row_spec = lambda w: pl.BlockSpec((tm, w), lambda i: (i, 0))
    res_spec = lambda a: pl.BlockSpec(a.shape, lambda i: (0, 0), pipeline_mode=pl.Buffered(1))
    return pl.pallas_call(
        _merge_kernel,
        grid=(rows // tm,),
        in_specs=[row_spec(D_A), row_spec(D_B), row_spec(2 * D_MODEL), row_spec(D_MODEL),
                  res_spec(wa), res_spec(ba), res_spec(wb), res_spec(wo), res_spec(gp)],
        out_specs=row_spec(D_MODEL),
        out_shape=jax.ShapeDtypeStruct((rows, D_MODEL), F32),
        compiler_params=_params("parallel"),
        name="merge",
    )(ua, yb, gates, x, wa, ba, wb, wo, gp)


def _ffn_kernel(x1_ref, gpre_ref, wg_ref, wv_ref, wd_ref, stg_ref, stv_ref, cw_ref, cb_ref,
                gpost_ref, o_ref, nsg_ref, nsv_ref, h_scr, acc_scr, act_scr, win_g, win_v, *carry,
                tm, sh, hist, rc, wide):
    l = pl.program_id(1)
    j = pl.program_id(2)
    past = (CONV_F - 1) * sh
    tf = win_g.shape[1]
    here = pl.ds(pl.multiple_of(j * tf, tf), tf) if wide else slice(None)

    @pl.when(j == 0)
    def _():
        h_scr[...] = _rms(x1_ref[0], gpre_ref[...]).astype(BF16)
        acc_scr[...] = jnp.zeros_like(acc_scr)

    for s, (st_ref, win) in enumerate(((stg_ref, win_g), (stv_ref, win_v))):
        if carry:
            @pl.when(l == 0)
            def _(win=win, st_ref=st_ref):
                win[hist - past:hist, :] = st_ref[0, :, here]

            @pl.when(l > 0)
            def _(win=win, s=s):
                win[0:hist, :] = carry[0][j, s]
        else:
            win[hist - past:hist, :] = st_ref[0, :, here]

    col_g = pl.multiple_of(j * tf, tf)
    col_v = pl.multiple_of(D_FF + j * tf, tf)
    cwg, cbg = cw_ref[:, pl.ds(col_g, tf)], cb_ref[:, pl.ds(col_g, tf)]
    cwv, cbv = cw_ref[:, pl.ds(col_v, tf)], cb_ref[:, pl.ds(col_v, tf)]

    def conv(win, cw_t, cb_t, r0, cs):
        out = cb_t[:, cs]
        for k in range(CONV_F):
            out = out + cw_t[k:k + 1, cs] * _rows(win, hist - (CONV_F - 1 - k) * sh + r0, rc, cs)
        return out

    cw = min(tf, 256)
    mc = min(tm, PROJ_ROWS)
    for c0 in range(0, tf, cw):
        cs = slice(c0, c0 + cw)
        for m0 in range(0, tm, mc):
            win_g[hist + m0:hist + m0 + mc, cs] = _dot(h_scr[m0:m0 + mc, :], wg_ref[:, cs])
            win_v[hist + m0:hist + m0 + mc, cs] = _dot(h_scr[m0:m0 + mc, :], wv_ref[:, cs])
            for r0 in range(m0, m0 + mc, rc):
                gate = _gelu_tanh(conv(win_g, cwg, cbg, r0, cs))
                act_scr[r0:r0 + rc, cs] = (gate * conv(win_v, cwv, cbv, r0, cs)).astype(BF16)
    acc_scr[...] += _dot(act_scr[...], wd_ref[...])

    for s, (ns_ref, win) in enumerate(((nsg_ref, win_g), (nsv_ref, win_v))):
        ns_ref[0, 0, :, here] = win[hist + tm - past:hist + tm, :]
        if carry:
            carry[0][j, s] = win[tm:tm + hist, :]

    @pl.when(j == pl.num_programs(2) - 1)
    def _():
        o_ref[0] = x1_ref[0] + _rms(acc_scr[...], gpost_ref[...])


def _ffn(x1, state, wts, *, tm, sh):
    b, L, _ = x1.shape
    tf = 512
    nj = D_FF // tf
    past = (CONV_F - 1) * sh
    hist = -(-past // SUBLANES) * SUBLANES
    n_l = L // tm
    carry = [pltpu.VMEM((nj, 2, hist, tf), F32)] if n_l > 1 else []
    col = lambda rows, off: pl.BlockSpec((rows, tf), lambda i, l, j: (0, j + off))
    wide = sh == 1
    if wide:
        st = lambda off: pl.BlockSpec((1, past, D_FF), lambda i, l, j: (i, 0, off // nj))
        tail = pl.BlockSpec((1, 1, past, D_FF), lambda i, l, j: (i, l, 0, 0))
    else:
        st = lambda off: pl.BlockSpec((1, past, tf), lambda i, l, j: (i, 0, j + off))
        tail = pl.BlockSpec((1, 1, past, tf), lambda i, l, j: (i, l, 0, j))
    vec = pl.BlockSpec((1, D_MODEL), lambda i, l, j: (0, 0))
    const = lambda a: pl.BlockSpec(a.shape, lambda i, l, j: (0, 0))
    xblk = pl.BlockSpec((1, tm, D_MODEL), lambda i, l, j: (i, l, 0))
    out, ns_g, ns_v = pl.pallas_call(
        functools.partial(_ffn_kernel, tm=tm, sh=sh, hist=hist, rc=min(tm, 32), wide=wide),
        grid=(b, n_l, nj),
        in_specs=[xblk, vec, col(D_MODEL, 0), col(D_MODEL, nj),
                  pl.BlockSpec((tf, D_MODEL), lambda i, l, j: (j, 0)),
                  st(0), st(nj), const(wts["w_dw_f"]), const(wts["b_dw_f"]), vec],
        out_specs=[xblk, tail, tail],
        out_shape=[jax.ShapeDtypeStruct((b, L, D_MODEL), F32),
                   jax.ShapeDtypeStruct((b, n_l, past, D_FF), F32),
                   jax.ShapeDtypeStruct((b, n_l, past, D_FF), F32)],
        scratch_shapes=[pltpu.VMEM((tm, D_MODEL), BF16), pltpu.VMEM((tm, D_MODEL), F32),
                        pltpu.VMEM((tm, tf), BF16),
                        pltpu.VMEM((hist + tm, tf), F32), pltpu.VMEM((hist + tm, tf), F32)] + carry,
        compiler_params=_params("parallel", "arbitrary", "arbitrary"),
        name="ffn",
    )(x1, wts["g_pre2"], wts["w_up"], wts["w_up"], wts["w_down"], state, state,
      wts["w_dw_f"], wts["b_dw_f"], wts["g_post2"])
    return out, jnp.concatenate([ns_g[:, -1], ns_v[:, -1]], axis=-1)


def _post_ln_silu(outs, extras):
    u = outs[0]
    mu = jnp.mean(u, axis=-1, keepdims=True)
    xc = u - mu
    r = lax.rsqrt(jnp.mean(xc * xc, axis=-1, keepdims=True) + EPS)
    return _silu(xc * r * extras[0] + extras[1])


def _post_silu(outs, extras):
    return _silu(outs[0])


def _conv_seq_kernel(*refs, width, tl, n_s, n_x, post, hist, rc):
    u_refs = refs[0:n_s]
    st_refs = refs[n_s:2 * n_s]
    w_refs = refs[2 * n_s:3 * n_s]
    b_refs = refs[3 * n_s:4 * n_s]
    x_refs = refs[4 * n_s:4 * n_s + n_x]
    o_ref = refs[4 * n_s + n_x]
    ns_refs = refs[4 * n_s + n_x + 1:4 * n_s + n_x + 1 + n_s]
    win_refs = refs[4 * n_s + n_x + 1 + n_s:4 * n_s + n_x + 1 + 2 * n_s]
    shf_refs = refs[4 * n_s + n_x + 1 + 2 * n_s:]
    past = width - 1
    l = pl.program_id(2)
    n_shf = hist + tl - SUBLANES

    for s in range(n_s):
        win = win_refs[s]

        @pl.when(l == 0)
        def _(win=win, s=s):
            if hist > past:
                win[0:hist - past, :] = jnp.zeros((hist - past, win.shape[1]), F32)
            win[hist - past:hist, :] = st_refs[s][0]

        @pl.when(l > 0)
        def _(win=win):
            win[0:hist, :] = win[tl:tl + hist, :]

        win[hist:hist + tl, :] = u_refs[s][0]
        for r in range(1, SUBLANES):
            shf_refs[s][r - 1, :, :] = _rows(win, r, n_shf)

    def tap(s, lo):
        a, r = divmod(lo, SUBLANES)
        if r == 0:
            return win_refs[s][lo:lo + rc, :]
        return shf_refs[s][r - 1, a * SUBLANES:a * SUBLANES + rc, :]

    extras = [x[...] for x in x_refs]
    for c in range(tl // rc):
        outs = []
        for s in range(n_s):
            acc = b_refs[s][...] + w_refs[s][0:1, :] * tap(s, c * rc + hist - past)
            for k in range(1, width):
                acc = acc + w_refs[s][k:k + 1, :] * tap(s, c * rc + hist - past + k)
            outs.append(acc)
        o_ref[0, c * rc:(c + 1) * rc, :] = post(outs, extras).astype(o_ref.dtype)

    @pl.when(l == pl.num_programs(2) - 1)
    def _():
        for s in range(n_s):
            ns_refs[s][0] = win_refs[s][hist + tl - past:hist + tl, :]


def _conv_seq(u, state, w, bias, extras, post, *, width, ct, tl, n_s, out_w, out_dtype, name):
    b, L, _ = u.shape
    past = width - 1
    hist = -(-past // SUBLANES) * SUBLANES
    nj = out_w // ct
    rc = min(tl, 64)
    kern = functools.partial(_conv_seq_kernel, width=width, tl=tl, n_s=n_s, n_x=len(extras),
                             post=post, hist=hist, rc=rc)
    in_specs = ([pl.BlockSpec((1, tl, ct), lambda i, j, l, s=s: (i, l, j + s * nj)) for s in range(n_s)]
                + [pl.BlockSpec((1, past, ct), lambda i, j, l, s=s: (i, 0, j + s * nj)) for s in range(n_s)]
                + [pl.BlockSpec((width, ct), lambda i, j, l, s=s: (0, j + s * nj)) for s in range(n_s)]
                + [pl.BlockSpec((1, ct), lambda i, j, l, s=s: (0, j + s * nj)) for s in range(n_s)]
                + [pl.BlockSpec((1, ct), lambda i, j, l: (0, j)) for _ in extras])
    out_specs = ([pl.BlockSpec((1, tl, ct), lambda i, j, l: (i, l, j))]
                 + [pl.BlockSpec((1, past, ct), lambda i, j, l: (i, 0, j)) for _ in range(n_s)])
    out_shape = ([jax.ShapeDtypeStruct((b, L, out_w), out_dtype)]
                 + [jax.ShapeDtypeStruct((b, past, out_w), F32) for _ in range(n_s)])
    res = pl.pallas_call(
        kern,
        grid=(b, nj, L // tl),
        in_specs=in_specs,
        out_specs=out_specs,
        out_shape=out_shape,
        scratch_shapes=[pltpu.VMEM((hist + tl, ct), F32) for _ in range(n_s)]
                       + [pltpu.VMEM((SUBLANES - 1, hist + tl - SUBLANES, ct), F32) for _ in range(n_s)],
        compiler_params=_params("parallel", "parallel", "arbitrary"),
        name=name,
    )(*([u] * n_s), *([state] * n_s), *([w] * n_s), *([bias] * n_s), *extras)
    return res[0], res[1:]


def _conv_slab_kernel(*refs, width, steps, n_s, n_x, post):
    u_refs = refs[0:n_s]
    st_refs = refs[n_s:2 * n_s]
    w_refs = refs[2 * n_s:3 * n_s]
    b_refs = refs[3 * n_s:4 * n_s]
    x_refs = refs[4 * n_s:4 * n_s + n_x]
    o_ref = refs[4 * n_s + n_x]
    ns_refs = refs[4 * n_s + n_x + 1:]
    past = width - 1

    def slab(s, i):
        return st_refs[s][i] if i < past else u_refs[s][i - past]

    extras = [x[...] for x in x_refs]
    for t in range(steps):
        outs = []
        for s in range(n_s):
            acc = b_refs[s][...] + w_refs[s][0:1, :] * slab(s, t)
            for k in range(1, width):
                acc = acc + w_refs[s][k:k + 1, :] * slab(s, t + k)
            outs.append(acc)
        o_ref[t] = post(outs, extras).astype(o_ref.dtype)
    for s in range(n_s):
        for i in range(past):
            ns_refs[s][i] = slab(s, i + steps)


def _conv_slab(u, state, w, bias, extras, post, *, width, ct, nbt, n_s, out_w, out_dtype, name):
    steps, nb, _ = u.shape
    past = width - 1
    nj = out_w // ct
    kern = functools.partial(_conv_slab_kernel, width=width, steps=steps, n_s=n_s, n_x=len(extras), post=post)
    in_specs = ([pl.BlockSpec((steps, nbt, ct), lambda i, j, s=s: (0, i, j + s * nj)) for s in range(n_s)]
                + [pl.BlockSpec((past, nbt, ct), lambda i, j, s=s: (0, i, j + s * nj)) for s in range(n_s)]
                + [pl.BlockSpec((width, ct), lambda i, j, s=s: (0, j + s * nj)) for s in range(n_s)]
                + [pl.BlockSpec((1, ct), lambda i, j, s=s: (0, j + s * nj)) for s in range(n_s)]
                + [pl.BlockSpec((1, ct), lambda i, j: (0, j)) for _ in extras])
    out_specs = ([pl.BlockSpec((steps, nbt, ct), lambda i, j: (0, i, j))]
                 + [pl.BlockSpec((past, nbt, ct), lambda i, j: (0, i, j)) for _ in range(n_s)])
    out_shape = ([jax.ShapeDtypeStruct((steps, nb, out_w), out_dtype)]
                 + [jax.ShapeDtypeStruct((past, nb, out_w), F32) for _ in range(n_s)])
    res = pl.pallas_call(
        kern,
        grid=(nb // nbt, nj),
        in_specs=in_specs,
        out_specs=out_specs,
        out_shape=out_shape,
        compiler_params=_params("parallel", "parallel"),
        name=name,
    )(*([u] * n_s), *([state] * n_s), *([w] * n_s), *([bias] * n_s), *extras)
    return res[0], res[1:]


def _ssd_kernel(xbc_ref, dt_ref, zs_ref, h0_ref, alog_ref, dsk_ref, gn_ref, e_ref, y_ref, ht_ref, h_scr, y_scr,
                *, q, bb):
    qk = SSD_KEYS
    c = pl.program_id(1)

    @pl.when(c == 0)
    def _():
        h_scr[...] = h0_ref[...]

    a = -jnp.exp(alog_ref[...])
    e = e_ref[...]
    row = lax.broadcasted_iota(jnp.int32, (qk, qk), 0)
    col = lax.broadcasted_iota(jnp.int32, (qk, qk), 1)
    tril = jnp.where(row >= col, 1.0, 0.0).astype(BF16)
    causal = row[:q, :] >= col[:q, :]
    key_head = lax.broadcasted_iota(jnp.int32, (qk, GROUP_W), 1) // HEAD_DIM

    def expand(vs):
        v = jnp.concatenate(vs, axis=0) if len(vs) > 1 else vs[0]
        hi = v.astype(BF16)
        lo = (v - hi.astype(F32)).astype(BF16)
        out = _dot(hi, e) + _dot(lo, e)
        n = vs[0].shape[0]
        return [out[i * n:(i + 1) * n, :] for i in range(len(vs))]

    def pad_keys(v):
        if q == qk:
            return v
        return jnp.concatenate([v, jnp.zeros((qk - q, v.shape[1]), v.dtype)], axis=0)

    seqs = range(bb)
    xqs = [xbc_ref[s] for s in seqs]
    xs = [xq[:, :D_B] for xq in xqs]
    bms = [pad_keys(xq[:, D_B:D_B + N_GROUPS * D_STATE]).astype(BF16) for xq in xqs]
    cms = [xq[:, D_B + N_GROUPS * D_STATE:].astype(BF16) for xq in xqs]
    dts = [dt_ref[s] for s in seqs]
    cums = [sum(_dot(tril, p) for p in _split_bf16(pad_keys(dt * a), 3)) for dt in dts]
    cum_ts = [cum.T for cum in cums]
    cum_qs = [cum[:q, :] for cum in cums]
    xdts = [x * d for x, d in zip(xs, expand(dts))]
    chunk_decay = [jnp.exp(cum[qk - 1:qk, :]) for cum in cums]
    to_end = [jnp.exp(cum[qk - 1:qk, :] - cum_q) for cum, cum_q in zip(cums, cum_qs)]
    wide = expand(to_end + [jnp.exp(cum_q) for cum_q in cum_qs])
    xws = [pad_keys(xdt * w).astype(BF16) for xdt, w in zip(xdts, wide[:bb])]
    xdts = [pad_keys(xdt).astype(BF16) for xdt in xdts]
    ecums = wide[bb:]

    for g in range(N_GROUPS):
        cols = slice(g * GROUP_W, (g + 1) * GROUP_W)
        for s in seqs:
            bg = bms[s][:, g * D_STATE:(g + 1) * D_STATE]
            cg = cms[s][:, g * D_STATE:(g + 1) * D_STATE]
            cb = lax.dot_general(cg, bg, (((1,), (1,)), ((), ())), preferred_element_type=F32)
            xdt_g = xdts[s][:, cols]
            scores, keys = [], []
            for r in range(HEADS_PER_GROUP):
                hd = g * HEADS_PER_GROUP + r
                seg = jnp.where(causal, cum_qs[s][:, hd:hd + 1] - cum_ts[s][hd:hd + 1, :], -1e30)
                scores.append((cb * jnp.exp(seg)).astype(BF16))
                keys.append(jnp.where(key_head == r, xdt_g, jnp.zeros_like(xdt_g)))
            y_g = _dot(jnp.concatenate(scores, axis=1), jnp.concatenate(keys, axis=0))
            h_g = h_scr[s, cols, :]
            y_inter = lax.dot_general(cg, h_g.astype(BF16), (((1,), (1,)), ((), ())),
                                      preferred_element_type=F32)
            y_scr[s, :, cols] = y_g + y_inter * ecums[s][:, cols]
            s_g = lax.dot_general(xws[s][:, cols], bg, (((0,), (0,)), ((), ())), preferred_element_type=F32)
            decay = jnp.concatenate(
                [jnp.broadcast_to(chunk_decay[s][:, g * HEADS_PER_GROUP + r:g * HEADS_PER_GROUP + r + 1],
                                  (HEAD_DIM, D_STATE)) for r in range(HEADS_PER_GROUP)], axis=0)
            h_scr[s, cols, :] = decay * h_g + s_g

    for s in seqs:
        y = y_scr[s] + dsk_ref[...] * xs[s]
        y_ref[s] = _rms(y * zs_ref[s], gn_ref[...]).astype(y_ref.dtype)

    @pl.when(c == pl.num_programs(1) - 1)
    def _():
        ht_ref[...] = h_scr[...]


def _ssd(xbc, dt, zs, h0, alog, dsk, gn, e, *, q, bb):
    b, L, _ = xbc.shape
    seq = lambda w: pl.BlockSpec((bb, q, w), lambda i, c: (i, c, 0))
    full = lambda arr: pl.BlockSpec(arr.shape, lambda i, c: (0,) * arr.ndim)
    state = pl.BlockSpec((bb, D_B, D_STATE), lambda i, c: (i, 0, 0))
    return pl.pallas_call(
        functools.partial(_ssd_kernel, q=q, bb=bb),
        grid=(b // bb, L // q),
        in_specs=[seq(D_XBC), seq(LANES), seq(D_B), state, full(alog), full(dsk), full(gn), full(e)],
        out_specs=[seq(D_B), state],
        out_shape=[jax.ShapeDtypeStruct((b, L, D_B), BF16), jax.ShapeDtypeStruct((b, D_B, D_STATE), F32)],
        scratch_shapes=[pltpu.VMEM((bb, D_B, D_STATE), F32), pltpu.VMEM((bb, q, D_B), F32)],
        compiler_params=_params("parallel", "arbitrary"),
        name="ssd",
    )(xbc, dt, zs, h0, alog, dsk, gn, e)


def _layer(x, wts, conv_a, xbc_fn, ssd_fn, ffn_fn):
    h, dt = _norm(x, wts["g_pre1"], wts["w_dt"], wts["dt_bias"])
    w_main = wts["w_main"]
    uglu = _proj(h, [w_main, w_main], [], _epi_glu, PROJ_COLS, F32, "in_glu", n=D_A, cols=[0, D_A])
    zs = _proj(h, [w_main], [], _epi_silu, PROJ_COLS, F32, "in_z", n=D_B, cols=[COL_Z])
    gates = _proj(h, [wts["w_g"]], [wts["b_gate"]], _epi_gate, PROJ_COLS, F32, "in_gates")

    ua, st_a = conv_a(uglu)
    xbc_act, st_b = xbc_fn(h)
    yb, st_h = ssd_fn(xbc_act, dt, zs)

    x1 = _merge(ua, yb, gates, x, wts["w_a_out"], wts["b_a_out"], wts["w_b_out"], wts["w_o"], wts["g_post1"])
    x2, st_f = ffn_fn(x1)
    return x2, st_a, st_b, st_h, st_f


def _seq_group(x, wts, st_a, st_b, st_h, st_f, tl, q):
    b, L, _ = x.shape

    def conv_a(uglu):
        ua, (ns,) = _conv_seq(uglu.reshape(b, L, D_A), st_a, wts["w_dw_a"], wts["b_dw_a"],
                              [wts["g_ln_a"], wts["b_ln_a"]], _post_ln_silu, width=CONV_A, ct=D_A,
                              tl=min(tl, 256), n_s=1, out_w=D_A, out_dtype=BF16, name="conv_a")
        return ua.reshape(b * L, D_A), ns

    def xbc_fn(h):
        return _proj_conv(h.reshape(b, L, D_MODEL), wts["w_main"], COL_XBC, st_b, wts["w_dw_b"], wts["b_dw_b"],
                          "in_xbc_conv")

    def ssd_fn(xbc_act, dt, zs):
        yb, ht = _ssd(xbc_act, dt.reshape(b, L, LANES), zs.reshape(b, L, D_B), st_h,
                      wts["a_log"], wts["d_skip"], wts["g_norm_b"], wts["expand"], q=q, bb=1)
        return yb.reshape(b * L, D_B), ht

    def ffn_fn(x1):
        x2, ns = _ffn(x1.reshape(b, L, D_MODEL), st_f, wts, tm=min(L, 512), sh=1)
        return x2.reshape(b * L, D_MODEL), ns

    x2, ns_a, ns_b, ns_h, ns_f = _layer(x.reshape(b * L, D_MODEL), wts, conv_a, xbc_fn, ssd_fn, ffn_fn)
    return x2.reshape(b, L, D_MODEL), ns_a, ns_b, ns_h, ns_f


def _step_group(x, wts, st_a, st_b, st_h, st_f):
    nb, L, _ = x.shape
    tmaj = lambda s: jnp.transpose(s, (1, 0, 2))
    lpad = -(-L // SUBLANES) * SUBLANES

    def conv_a(uglu):
        ua, (ns,) = _conv_slab(uglu.reshape(L, nb, D_A), tmaj(st_a), wts["w_dw_a"], wts["b_dw_a"],
                               [wts["g_ln_a"], wts["b_ln_a"]], _post_ln_silu, width=CONV_A, ct=D_A,
                               nbt=32, n_s=1, out_w=D_A, out_dtype=BF16, name="conv_a_step")
        return ua.reshape(L * nb, D_A), tmaj(ns)

    def bmaj_pad(v):
        return jnp.pad(tmaj(v), ((0, 0), (0, lpad - L), (0, 0)))

    def xbc_fn(h):
        xbc = _proj(h, [wts["w_main"]], [], _epi_id, PROJ_COLS, F32, "in_xbc", n=D_XBC, cols=[COL_XBC])
        act, (ns,) = _conv_slab(xbc.reshape(L, nb, D_XBC), tmaj(st_b), wts["w_dw_b"], wts["b_dw_b"], [],
                                _post_silu, width=CONV_B, ct=512, nbt=nb, n_s=1, out_w=D_XBC,
                                out_dtype=F32, name="conv_b_step")
        return act, tmaj(ns)

    def ssd_fn(xbc_act, dt, zs):
        yb, ht = _ssd(bmaj_pad(xbc_act), bmaj_pad(dt.reshape(L, nb, LANES)), bmaj_pad(zs.reshape(L, nb, D_B)),
                      st_h, wts["a_log"], wts["d_skip"], wts["g_norm_b"], wts["expand"], q=lpad, bb=8)
        return tmaj(yb[:, :L]).reshape(L * nb, D_B), ht

    def ffn_fn(x1):
        past = CONV_F - 1
        x2, ns = _ffn(x1.reshape(1, L * nb, D_MODEL), tmaj(st_f).reshape(1, past * nb, 2 * D_FF), wts,
                      tm=L * nb, sh=nb)
        return x2.reshape(L * nb, D_MODEL), tmaj(ns.reshape(past, nb, 2 * D_FF))

    x2, ns_a, ns_b, ns_h, ns_f = _layer(tmaj(x).reshape(L * nb, D_MODEL), wts, conv_a, xbc_fn, ssd_fn, ffn_fn)
    return tmaj(x2.reshape(L, nb, D_MODEL)), ns_a, ns_b, ns_h, ns_f


def _layer_weights(l, g_pre1, g_post1, w_in, b_gate, w_dw_a, b_dw_a, g_ln_a, b_ln_a, w_a_out, b_a_out,
                   w_dw_b, b_dw_b, dt_bias, a_log, d_skip, g_norm_b, w_b_out, w_o,
                   g_pre2, g_post2, w_up, w_dw_f, b_dw_f, w_down):
    row = lambda v: v[l].reshape(1, -1)
    lane_pad = lambda v: jnp.pad(v, ((0, 0), (0, LANES - v.shape[1])))
    w = w_in[l]
    head = jnp.arange(LANES, dtype=jnp.int32)[:, None]
    chan_head = (jnp.arange(D_B, dtype=jnp.int32) // HEAD_DIM)[None, :]
    expand = (head == chan_head).astype(BF16)
    return {
        "g_pre1": row(g_pre1), "g_post1": row(g_post1), "g_pre2": row(g_pre2), "g_post2": row(g_post2),
        "w_main": w[:, :COL_DT].astype(BF16),
        "w_dt": lane_pad(w[:, COL_DT:COL_GATE]).astype(BF16), "w_g": w[:, COL_GATE:].astype(BF16),
        "dt_bias": lane_pad(row(dt_bias)), "b_gate": row(b_gate),
        "w_dw_a": w_dw_a[l], "b_dw_a": row(b_dw_a), "g_ln_a": row(g_ln_a), "b_ln_a": row(b_ln_a),
        "w_a_out": w_a_out[l].astype(BF16), "b_a_out": row(b_a_out),
        "w_dw_b": w_dw_b[l], "b_dw_b": row(b_dw_b),
        "a_log": lane_pad(row(a_log)), "d_skip": jnp.repeat(d_skip[l], HEAD_DIM).reshape(1, D_B),
        "g_norm_b": row(g_norm_b), "expand": expand,
        "w_b_out": w_b_out[l].astype(BF16), "w_o": w_o[l].astype(BF16),
        "w_up": w_up[l].astype(BF16), "w_dw_f": w_dw_f[l], "b_dw_f": row(b_dw_f),
        "w_down": w_down[l].astype(BF16),
    }


def kernel(x_prompt, x_sample, state_conv_a, state_conv_b, state_ssm, state_conv_ffn, meta_tokens, g_pre1, g_post1, w_in, b_gate, w_dw_a, b_dw_a, g_ln_a, b_ln_a, w_a_out, b_a_out, w_dw_b, b_dw_b, dt_bias, a_log, d_skip, g_norm_b, w_b_out, w_o, g_pre2, g_post2, w_up, w_dw_f, b_dw_f, w_down):
    depth = w_in.shape[0]
    bp = x_prompt.shape[0]
    nb = x_sample.shape[0]
    xm = meta_tokens.astype(x_prompt.dtype)[None]
    xp, xs = x_prompt, x_sample
    pa, pb, ph, pf = [], [], [], []
    sa, sb, sh, sf = [], [], [], []
    for l in range(depth):
        wts = _layer_weights(l, g_pre1, g_post1, w_in, b_gate, w_dw_a, b_dw_a, g_ln_a, b_ln_a, w_a_out, b_a_out,
                             w_dw_b, b_dw_b, dt_bias, a_log, d_skip, g_norm_b, w_b_out, w_o,
                             g_pre2, g_post2, w_up, w_dw_f, b_dw_f, w_down)
        xm, m_a, m_b, m_h, m_f = _seq_group(
            xm, wts, jnp.zeros((1, CONV_A - 1, D_A), F32), jnp.zeros((1, CONV_B - 1, D_XBC), F32),
            jnp.zeros((1, D_B, D_STATE), F32), jnp.zeros((1, CONV_F - 1, 2 * D_FF), F32), tl=N_META, q=SUBLANES)
        rep = lambda s: jnp.broadcast_to(s, (bp,) + s.shape[1:])
        xp, c_a, c_b, c_h, c_f = _seq_group(xp, wts, rep(m_a), rep(m_b), rep(m_h), rep(m_f), tl=512, q=SSD_KEYS)
        pa.append(c_a); pb.append(c_b); ph.append(c_h.reshape(bp, N_HEADS, HEAD_DIM, D_STATE)); pf.append(c_f)
        xs, d_a, d_b, d_h, d_f = _step_group(
            xs, wts, state_conv_a[l], state_conv_b[l], state_ssm[l].reshape(nb, D_B, D_STATE), state_conv_ffn[l])
        sa.append(d_a); sb.append(d_b); sh.append(d_h.reshape(nb, N_HEADS, HEAD_DIM, D_STATE)); sf.append(d_f)
    return (xp, xs, jnp.stack(pa), jnp.stack(pb), jnp.stack(ph), jnp.stack(pf),
            jnp.stack(sa), jnp.stack(sb), jnp.stack(sh), jnp.stack(sf))
```

```python
import functools

import jax
import jax.numpy as jnp
from jax import lax
from jax.experimental import pallas as pl
from jax.experimental.pallas import tpu as pltpu

D_MODEL = 2048
N_META = 16
D_A = 1024
CONV_A = 31
D_B = 2048
HEAD_DIM = 64
N_HEADS = D_B // HEAD_DIM
N_GROUPS = 8
HEADS_PER_GROUP = N_HEADS // N_GROUPS
GROUP_W = HEADS_PER_GROUP * HEAD_DIM
D_STATE = 128
CONV_B = 4
D_FF = 5632
CONV_F = 3
EPS = 1e-6
D_XBC = D_B + 2 * N_GROUPS * D_STATE
COL_Z = 2 * D_A
COL_XBC = COL_Z + D_B
COL_DT = COL_XBC + D_XBC
COL_GATE = COL_DT + N_HEADS

LANES = 128
SUBLANES = 8
SSD_KEYS = 128
PROJ_ROWS = 256
PROJ_COLS = 1024
VMEM_LIMIT = 56 * 1024 * 1024

F32 = jnp.float32
BF16 = jnp.bfloat16


def _params(*sem):
    return pltpu.CompilerParams(dimension_semantics=sem, vmem_limit_bytes=VMEM_LIMIT)


def _sigmoid(x):
    return 1.0 / (1.0 + jnp.exp(-x))


def _silu(x):
    return x * _sigmoid(x)


def _softplus(x):
    return jnp.maximum(x, 0.0) + jnp.log1p(jnp.exp(-jnp.abs(x)))


def _gelu_tanh(x):
    return 0.5 * x * (1.0 + jnp.tanh(0.7978845608028654 * (x + 0.044715 * (x * x * x))))


def _rms(x, g):
    r = lax.rsqrt(jnp.mean(x * x, axis=-1, keepdims=True) + EPS)
    return x * r * g


def _dot(a, b):
    return jnp.dot(a, b, preferred_element_type=F32)


def _rows(ref, lo, n, cols=slice(None)):
    base, off = divmod(lo, SUBLANES)
    if off == 0:
        return ref[lo:lo + n, cols]
    blk = ref[base * SUBLANES:base * SUBLANES + n + SUBLANES, cols]
    return pltpu.roll(blk, n + SUBLANES - off, 0)[0:n, :]


def _split_bf16(v, parts):
    out = []
    for _ in range(parts):
        p = v.astype(BF16)
        out.append(p)
        v = v - p.astype(F32)
    return out


def _norm_kernel(x_ref, g_ref, wdt_ref, bdt_ref, h_ref, dt_ref):
    h = _rms(x_ref[...], g_ref[...]).astype(BF16)
    h_ref[...] = h
    dt_ref[...] = _epi_dt([_dot(h, wdt_ref[...])], [bdt_ref[...]])


def _norm(x, g, w_dt, dt_bias):
    rows, d = x.shape
    tm = min(rows, 512)
    const = lambda a: pl.BlockSpec(a.shape, lambda i: (0, 0))
    return pl.pallas_call(
        _norm_kernel,
        grid=(rows // tm,),
        in_specs=[pl.BlockSpec((tm, d), lambda i: (i, 0)), const(g), const(w_dt), const(dt_bias)],
        out_specs=[pl.BlockSpec((tm, d), lambda i: (i, 0)), pl.BlockSpec((tm, LANES), lambda i: (i, 0))],
        out_shape=[jax.ShapeDtypeStruct((rows, d), BF16), jax.ShapeDtypeStruct((rows, LANES), F32)],
        compiler_params=_params("parallel"),
        name="norm_dt",
    )(x, g, w_dt, dt_bias)


def _norm_ring_kernel(x_hbm, g_ref, wdt_ref, bdt_ref, h_hbm, dt_hbm, *, tm):
    rows, d = x_hbm.shape

    def body(x_ref, h_ref, dt_ref):
        _norm_kernel(x_ref, g_ref, wdt_ref, bdt_ref, h_ref, dt_ref)

    pltpu.emit_pipeline(
        body,
        grid=(rows // tm,),
        in_specs=[pl.BlockSpec((tm, d), lambda i: (i, 0), pipeline_mode=pl.Buffered(3))],
        out_specs=[pl.BlockSpec((tm, d), lambda i: (i, 0)), pl.BlockSpec((tm, LANES), lambda i: (i, 0))],
    )(x_hbm, h_hbm, dt_hbm)


def _norm_ring(x, g, w_dt, dt_bias):
    rows, d = x.shape
    tm = 512
    whole = lambda a: pl.BlockSpec(a.shape, lambda: (0, 0))
    hbm = pl.BlockSpec(memory_space=pl.ANY)
    return pl.pallas_call(
        functools.partial(_norm_ring_kernel, tm=tm),
        in_specs=[hbm, whole(g), whole(w_dt), whole(dt_bias)],
        out_specs=[hbm, hbm],
        out_shape=[jax.ShapeDtypeStruct((rows, d), BF16), jax.ShapeDtypeStruct((rows, LANES), F32)],
        compiler_params=pltpu.CompilerParams(vmem_limit_bytes=VMEM_LIMIT),
        name="norm_dt_ring",
    )(x, g, w_dt, dt_bias)


def _proj_kernel(*refs, n_w, n_b, epilogue):
    h_ref = refs[0]
    w_refs = refs[1:1 + n_w]
    b_refs = refs[1 + n_w:1 + n_w + n_b]
    o_ref = refs[-1]
    bs = [b[...] for b in b_refs]
    rows = h_ref.shape[0]
    mc = min(rows, PROJ_ROWS)
    for r0 in range(0, rows, mc):
        h = h_ref[r0:r0 + mc, :]
        accs = [_dot(h, w[...]) for w in w_refs]
        o_ref[r0:r0 + mc, :] = epilogue(accs, bs).astype(o_ref.dtype)


def _proj(h, ws, bs, epilogue, tn, out_dtype, name, n=None, cols=None):
    rows, d = h.shape
    n = n or ws[0].shape[1]
    cols = cols or [0] * len(ws)
    tm = min(rows, 1024)
    kern = functools.partial(_proj_kernel, n_w=len(ws), n_b=len(bs), epilogue=epilogue)
    return pl.pallas_call(
        kern,
        grid=(rows // tm, n // tn),
        in_specs=[pl.BlockSpec((tm, d), lambda i, j: (i, 0))]
                 + [pl.BlockSpec((d, tn), lambda i, j, c=c // tn: (0, j + c)) for c in cols]
                 + [pl.BlockSpec((1, tn), lambda i, j: (0, j)) for _ in bs],
        out_specs=pl.BlockSpec((tm, tn), lambda i, j: (i, j)),
        out_shape=jax.ShapeDtypeStruct((rows, n), out_dtype),
        compiler_params=_params("parallel", "parallel"),
        name=name,
    )(h, *ws, *bs)


def _proj_conv_kernel(h_ref, w_ref, st_ref, cw_ref, cb_ref, o_ref, ns_ref, win, *carry, tm, rc):
    l = pl.program_id(1)
    j = pl.program_id(2)
    past = CONV_B - 1
    hist = SUBLANES
    if carry:
        @pl.when(l == 0)
        def _():
            win[hist - past:hist, :] = st_ref[0]

        @pl.when(l > 0)
        def _():
            win[0:hist, :] = carry[0][j]
    else:
        win[hist - past:hist, :] = st_ref[0]

    mc = min(tm, PROJ_ROWS)
    for m0 in range(0, tm, mc):
        win[hist + m0:hist + m0 + mc, :] = _dot(h_ref[0, m0:m0 + mc, :], w_ref[...])
        for r0 in range(m0, m0 + mc, rc):
            pre = cb_ref[...]
            for k in range(CONV_B):
                pre = pre + cw_ref[k:k + 1, :] * _rows(win, r0 + hist - past + k, rc)
            o_ref[0, r0:r0 + rc, :] = _silu(pre)

    ns_ref[0, 0] = win[hist + tm - past:hist + tm, :]
    if carry:
        carry[0][j] = win[tm:tm + hist, :]


def _proj_conv(h, w, col0, state, cw, cb, name):
    b, L, d = h.shape
    n = state.shape[-1]
    past = CONV_B - 1
    tn = PROJ_COLS
    tm = min(L, 1024)
    n_l, nj = L // tm, n // tn
    carry = [pltpu.VMEM((nj, SUBLANES, tn), F32)] if n_l > 1 else []
    out, ns = pl.pallas_call(
        functools.partial(_proj_conv_kernel, tm=tm, rc=min(tm, 32)),
        grid=(b, n_l, nj),
        in_specs=[pl.BlockSpec((1, tm, d), lambda i, l, j: (i, l, 0)),
                  pl.BlockSpec((d, tn), lambda i, l, j: (0, j + col0 // tn)),
                  pl.BlockSpec((1, past, tn), lambda i, l, j: (i, 0, j)),
                  pl.BlockSpec((CONV_B, tn), lambda i, l, j: (0, j)),
                  pl.BlockSpec((1, tn), lambda i, l, j: (0, j))],
        out_specs=[pl.BlockSpec((1, tm, tn), lambda i, l, j: (i, l, j)),
                   pl.BlockSpec((1, 1, past, tn), lambda i, l, j: (i, l, 0, j))],
        out_shape=[jax.ShapeDtypeStruct((b, L, n), F32), jax.ShapeDtypeStruct((b, n_l, past, n), F32)],
        scratch_shapes=[pltpu.VMEM((SUBLANES + tm, tn), F32)] + carry,
        compiler_params=_params("parallel", "arbitrary", "arbitrary"),
        name=name,
    )(h, w, state, cw, cb)
    return out, ns[:, -1]


def _epi_glu(a, b):
    return a[0] * _sigmoid(a[1])


def _epi_dt(a, b):
    lane = lax.broadcasted_iota(jnp.int32, a[0].shape, 1)
    return jnp.where(lane < N_HEADS, _softplus(a[0] + b[0]), 0.0)


def _epi_silu(a, b):
    return _silu(a[0])


def _epi_id(a, b):
    return a[0]


def _epi_gate(a, b):
    return _sigmoid(a[0] + b[0])


def _merge_kernel(ua_ref, yb_ref, gt_ref, x_ref, wa_ref, ba_ref, wb_ref, wo_ref, gp_ref, o_ref):
    ya = _dot(ua_ref[...], wa_ref[...]) + ba_ref[...]
    yb = _dot(yb_ref[...], wb_ref[...])
    mix = gt_ref[:, :D_MODEL] * ya + gt_ref[:, D_MODEL:] * yb
    m = _dot(mix.astype(BF16), wo_ref[...])
    o_ref[...] = x_ref[...] + _rms(m, gp_ref[...])


def _merge(ua, yb, gates, x, wa, ba, wb, wo, gp):
    rows = x.shape[0]
    tm = min(rows, 256)
    row_spec = lambda w: pl.BlockSpec((tm, w), lambda i: (i, 0))
    res_spec = lambda a: pl.BlockSpec(a.shape, lambda i: (0, 0), pipeline_mode=pl.Buffered(1))
    return pl.pallas_call(
        _merge_kernel,
        grid=(rows // tm,),
        in_specs=[row_spec(D_A), row_spec(D_B), row_spec(2 * D_MODEL), row_spec(D_MODEL),
                  res_spec(wa), res_spec(ba), res_spec(wb), res_spec(wo), res_spec(gp)],
        out_specs=row_spec(D_MODEL),
        out_shape=jax.ShapeDtypeStruct((rows, D_MODEL), F32),
        compiler_params=_params("parallel"),
        name="merge",
    )(ua, yb, gates, x, wa, ba, wb, wo, gp)


def _ffn_kernel(x1_ref, gpre_ref, wg_ref, wv_ref, wd_ref, stg_ref, stv_ref, cw_ref, cb_ref,
                gpost_ref, o_ref, nsg_ref, nsv_ref, h_scr, acc_scr, act_scr, win_g, win_v, *carry,
                tm, sh, hist, rc, wide):
    l = pl.program_id(1)
    j = pl.program_id(2)
    past = (CONV_F - 1) * sh
    tf = win_g.shape[1]
    here = pl.ds(pl.multiple_of(j * tf, tf), tf) if wide else slice(None)

    @pl.when(j == 0)
    def _():
        h_scr[...] = _rms(x1_ref[0], gpre_ref[...]).astype(BF16)
        acc_scr[...] = jnp.zeros_like(acc_scr)

    for s, (st_ref, win) in enumerate(((stg_ref, win_g), (stv_ref, win_v))):
        if carry:
            @pl.when(l == 0)
            def _(win=win, st_ref=st_ref):
                win[hist - past:hist, :] = st_ref[0, :, here]

            @pl.when(l > 0)
            def _(win=win, s=s):
                win[0:hist, :] = carry[0][j, s]
        else:
            win[hist - past:hist, :] = st_ref[0, :, here]

    col_g = pl.multiple_of(j * tf, tf)
    col_v = pl.multiple_of(D_FF + j * tf, tf)
    cwg, cbg = cw_ref[:, pl.ds(col_g, tf)], cb_ref[:, pl.ds(col_g, tf)]
    cwv, cbv = cw_ref[:, pl.ds(col_v, tf)], cb_ref[:, pl.ds(col_v, tf)]

    def conv(win, cw_t, cb_t, r0, cs):
        out = cb_t[:, cs]
        for k in range(CONV_F):
            out = out + cw_t[k:k + 1, cs] * _rows(win, hist - (CONV_F - 1 - k) * sh + r0, rc, cs)
        return out

    cw = min(tf, 256)
    mc = min(tm, PROJ_ROWS)
    for c0 in range(0, tf, cw):
        cs = slice(c0, c0 + cw)
        for m0 in range(0, tm, mc):
            win_g[hist + m0:hist + m0 + mc, cs] = _dot(h_scr[m0:m0 + mc, :], wg_ref[:, cs])
            win_v[hist + m0:hist + m0 + mc, cs] = _dot(h_scr[m0:m0 + mc, :], wv_ref[:, cs])
            for r0 in range(m0, m0 + mc, rc):
                gate = _gelu_tanh(conv(win_g, cwg, cbg, r0, cs))
                act_scr[r0:r0 + rc, cs] = (gate * conv(win_v, cwv, cbv, r0, cs)).astype(BF16)
    acc_scr[...] += _dot(act_scr[...], wd_ref[...])

    for s, (ns_ref, win) in enumerate(((nsg_ref, win_g), (nsv_ref, win_v))):
        ns_ref[0, 0, :, here] = win[hist + tm - past:hist + tm, :]
        if carry:
            carry[0][j, s] = win[tm:tm + hist, :]

    @pl.when(j == pl.num_programs(2) - 1)
    def _():
        o_ref[0] = x1_ref[0] + _rms(acc_scr[...], gpost_ref[...])


def _ffn(x1, state, wts, *, tm, sh):
    b, L, _ = x1.shape
    tf = 512
    nj = D_FF // tf
    past = (CONV_F - 1) * sh
    hist = -(-past // SUBLANES) * SUBLANES
    n_l = L // tm
    carry = [pltpu.VMEM((nj, 2, hist, tf), F32)] if n_l > 1 else []
    col = lambda rows, off: pl.BlockSpec((rows, tf), lambda i, l, j: (0, j + off))
    wide = sh == 1
    if wide:
        st = lambda off: pl.BlockSpec((1, past, D_FF), lambda i, l, j: (i, 0, off // nj))
        tail = pl.BlockSpec((1, 1, past, D_FF), lambda i, l, j: (i, l, 0, 0))
    else:
        st = lambda off: pl.BlockSpec((1, past, tf), lambda i, l, j: (i, 0, j + off))
        tail = pl.BlockSpec((1, 1, past, tf), lambda i, l, j: (i, l, 0, j))
    vec = pl.BlockSpec((1, D_MODEL), lambda i, l, j: (0, 0))
    const = lambda a: pl.BlockSpec(a.shape, lambda i, l, j: (0, 0))
    xblk = pl.BlockSpec((1, tm, D_MODEL), lambda i, l, j: (i, l, 0))
    out, ns_g, ns_v = pl.pallas_call(
        functools.partial(_ffn_kernel, tm=tm, sh=sh, hist=hist, rc=min(tm, 32), wide=wide),
        grid=(b, n_l, nj),
        in_specs=[xblk, vec, col(D_MODEL, 0), col(D_MODEL, nj),
                  pl.BlockSpec((tf, D_MODEL), lambda i, l, j: (j, 0)),
                  st(0), st(nj), const(wts["w_dw_f"]), const(wts["b_dw_f"]), vec],
        out_specs=[xblk, tail, tail],
        out_shape=[jax.ShapeDtypeStruct((b, L, D_MODEL), F32),
                   jax.ShapeDtypeStruct((b, n_l, past, D_FF), F32),
                   jax.ShapeDtypeStruct((b, n_l, past, D_FF), F32)],
        scratch_shapes=[pltpu.VMEM((tm, D_MODEL), BF16), pltpu.VMEM((tm, D_MODEL), F32),
                        pltpu.VMEM((tm, tf), BF16),
                        pltpu.VMEM((hist + tm, tf), F32), pltpu.VMEM((hist + tm, tf), F32)] + carry,
        compiler_params=_params("parallel", "arbitrary", "arbitrary"),
        name="ffn",
    )(x1, wts["g_pre2"], wts["w_up"], wts["w_up"], wts["w_down"], state, state,
      wts["w_dw_f"], wts["b_dw_f"], wts["g_post2"])
    return out, jnp.concatenate([ns_g[:, -1], ns_v[:, -1]], axis=-1)


def _post_ln_silu(outs, extras):
    u = outs[0]
    mu = jnp.mean(u, axis=-1, keepdims=True)
    xc = u - mu
    r = lax.rsqrt(jnp.mean(xc * xc, axis=-1, keepdims=True) + EPS)
    return _silu(xc * r * extras[0] + extras[1])


def _post_silu(outs, extras):
    return _silu(outs[0])


def _conv_seq_kernel(*refs, width, tl, n_s, n_x, post, hist, rc):
    u_refs = refs[0:n_s]
    st_refs = refs[n_s:2 * n_s]
    w_refs = refs[2 * n_s:3 * n_s]
    b_refs = refs[3 * n_s:4 * n_s]
    x_refs = refs[4 * n_s:4 * n_s + n_x]
    o_ref = refs[4 * n_s + n_x]
    ns_refs = refs[4 * n_s + n_x + 1:4 * n_s + n_x + 1 + n_s]
    win_refs = refs[4 * n_s + n_x + 1 + n_s:4 * n_s + n_x + 1 + 2 * n_s]
    shf_refs = refs[4 * n_s + n_x + 1 + 2 * n_s:]
    past = width - 1
    l = pl.program_id(2)
    n_shf = hist + tl - SUBLANES

    for s in range(n_s):
        win = win_refs[s]

        @pl.when(l == 0)
        def _(win=win, s=s):
            if hist > past:
                win[0:hist - past, :] = jnp.zeros((hist - past, win.shape[1]), F32)
            win[hist - past:hist, :] = st_refs[s][0]

        @pl.when(l > 0)
        def _(win=win):
            win[0:hist, :] = win[tl:tl + hist, :]

        win[hist:hist + tl, :] = u_refs[s][0]
        for r in range(1, SUBLANES):
            shf_refs[s][r - 1, :, :] = _rows(win, r, n_shf)

    def tap(s, lo):
        a, r = divmod(lo, SUBLANES)
        if r == 0:
            return win_refs[s][lo:lo + rc, :]
        return shf_refs[s][r - 1, a * SUBLANES:a * SUBLANES + rc, :]

    extras = [x[...] for x in x_refs]
    for c in range(tl // rc):
        outs = []
        for s in range(n_s):
            acc = b_refs[s][...] + w_refs[s][0:1, :] * tap(s, c * rc + hist - past)
            for k in range(1, width):
                acc = acc + w_refs[s][k:k + 1, :] * tap(s, c * rc + hist - past + k)
            outs.append(acc)
        o_ref[0, c * rc:(c + 1) * rc, :] = post(outs, extras).astype(o_ref.dtype)

    @pl.when(l == pl.num_programs(2) - 1)
    def _():
        for s in range(n_s):
            ns_refs[s][0] = win_refs[s][hist + tl - past:hist + tl, :]


def _conv_seq(u, state, w, bias, extras, post, *, width, ct, tl, n_s, out_w, out_dtype, name):
    b, L, _ = u.shape
    past = width - 1
    hist = -(-past // SUBLANES) * SUBLANES
    nj = out_w // ct
    rc = min(tl, 64)
    kern = functools.partial(_conv_seq_kernel, width=width, tl=tl, n_s=n_s, n_x=len(extras),
                             post=post, hist=hist, rc=rc)
    in_specs = ([pl.BlockSpec((1, tl, ct), lambda i, j, l, s=s: (i, l, j + s * nj)) for s in range(n_s)]
                + [pl.BlockSpec((1, past, ct), lambda i, j, l, s=s: (i, 0, j + s * nj)) for s in range(n_s)]
                + [pl.BlockSpec((width, ct), lambda i, j, l, s=s: (0, j + s * nj)) for s in range(n_s)]
                + [pl.BlockSpec((1, ct), lambda i, j, l, s=s: (0, j + s * nj)) for s in range(n_s)]
                + [pl.BlockSpec((1, ct), lambda i, j, l: (0, j)) for _ in extras])
    out_specs = ([pl.BlockSpec((1, tl, ct), lambda i, j, l: (i, l, j))]
                 + [pl.BlockSpec((1, past, ct), lambda i, j, l: (i, 0, j)) for _ in range(n_s)])
    out_shape = ([jax.ShapeDtypeStruct((b, L, out_w), out_dtype)]
                 + [jax.ShapeDtypeStruct((b, past, out_w), F32) for _ in range(n_s)])
    res = pl.pallas_call(
        kern,
        grid=(b, nj, L // tl),
        in_specs=in_specs,
        out_specs=out_specs,
        out_shape=out_shape,
        scratch_shapes=[pltpu.VMEM((hist + tl, ct), F32) for _ in range(n_s)]
                       + [pltpu.VMEM((SUBLANES - 1, hist + tl - SUBLANES, ct), F32) for _ in range(n_s)],
        compiler_params=_params("parallel", "parallel", "arbitrary"),
        name=name,
    )(*([u] * n_s), *([state] * n_s), *([w] * n_s), *([bias] * n_s), *extras)
    return res[0], res[1:]


def _conv_slab_kernel(*refs, width, steps, n_s, n_x, post):
    u_refs = refs[0:n_s]
    st_refs = refs[n_s:2 * n_s]
    w_refs = refs[2 * n_s:3 * n_s]
    b_refs = refs[3 * n_s:4 * n_s]
    x_refs = refs[4 * n_s:4 * n_s + n_x]
    o_ref = refs[4 * n_s + n_x]
    ns_refs = refs[4 * n_s + n_x + 1:]
    past = width - 1

    def slab(s, i):
        return st_refs[s][i] if i < past else u_refs[s][i - past]

    extras = [x[...] for x in x_refs]
    for t in range(steps):
        outs = []
        for s in range(n_s):
            acc = b_refs[s][...] + w_refs[s][0:1, :] * slab(s, t)
            for k in range(1, width):
                acc = acc + w_refs[s][k:k + 1, :] * slab(s, t + k)
            outs.append(acc)
        o_ref[t] = post(outs, extras).astype(o_ref.dtype)
    for s in range(n_s):
        for i in range(past):
            ns_refs[s][i] = slab(s, i + steps)


def _conv_slab(u, state, w, bias, extras, post, *, width, ct, nbt, n_s, out_w, out_dtype, name):
    steps, nb, _ = u.shape
    past = width - 1
    nj = out_w // ct
    kern = functools.partial(_conv_slab_kernel, width=width, steps=steps, n_s=n_s, n_x=len(extras), post=post)
    in_specs = ([pl.BlockSpec((steps, nbt, ct), lambda i, j, s=s: (0, i, j + s * nj)) for s in range(n_s)]
                + [pl.BlockSpec((past, nbt, ct), lambda i, j, s=s: (0, i, j + s * nj)) for s in range(n_s)]
                + [pl.BlockSpec((width, ct), lambda i, j, s=s: (0, j + s * nj)) for s in range(n_s)]
                + [pl.BlockSpec((1, ct), lambda i, j, s=s: (0, j + s * nj)) for s in range(n_s)]
                + [pl.BlockSpec((1, ct), lambda i, j: (0, j)) for _ in extras])
    out_specs = ([pl.BlockSpec((steps, nbt, ct), lambda i, j: (0, i, j))]
                 + [pl.BlockSpec((past, nbt, ct), lambda i, j: (0, i, j)) for _ in range(n_s)])
    out_shape = ([jax.ShapeDtypeStruct((steps, nb, out_w), out_dtype)]
                 + [jax.ShapeDtypeStruct((past, nb, out_w), F32) for _ in range(n_s)])
    res = pl.pallas_call(
        kern,
        grid=(nb // nbt, nj),
        in_specs=in_specs,
        out_specs=out_specs,
        out_shape=out_shape,
        compiler_params=_params("parallel", "parallel"),
        name=name,
    )(*([u] * n_s), *([state] * n_s), *([w] * n_s), *([bias] * n_s), *extras)
    return res[0], res[1:]


def _ssd_kernel(xbc_ref, dt_ref, zs_ref, h0_ref, alog_ref, dsk_ref, gn_ref, e_ref, y_ref, ht_ref, h_scr, y_scr,
                *, q, bb):
    qk = SSD_KEYS
    c = pl.program_id(1)

    @pl.when(c == 0)
    def _():
        h_scr[...] = h0_ref[...]

    a = -jnp.exp(alog_ref[...])
    e = e_ref[...]
    row = lax.broadcasted_iota(jnp.int32, (qk, qk), 0)
    col = lax.broadcasted_iota(jnp.int32, (qk, qk), 1)
    tril = jnp.where(row >= col, 1.0, 0.0).astype(BF16)
    causal = row[:q, :] >= col[:q, :]
    key_head = lax.broadcasted_iota(jnp.int32, (qk, GROUP_W), 1) // HEAD_DIM

    def expand(vs):
        v = jnp.concatenate(vs, axis=0) if len(vs) > 1 else vs[0]
        hi = v.astype(BF16)
        lo = (v - hi.astype(F32)).astype(BF16)
        out = _dot(hi, e) + _dot(lo, e)
        n = vs[0].shape[0]
        return [out[i * n:(i + 1) * n, :] for i in range(len(vs))]

    def pad_keys(v):
        if q == qk:
            return v
        return jnp.concatenate([v, jnp.zeros((qk - q, v.shape[1]), v.dtype)], axis=0)

    seqs = range(bb)
    xqs = [xbc_ref[s] for s in seqs]
    xs = [xq[:, :D_B] for xq in xqs]
    bms = [pad_keys(xq[:, D_B:D_B + N_GROUPS * D_STATE]).astype(BF16) for xq in xqs]
    cms = [xq[:, D_B + N_GROUPS * D_STATE:].astype(BF16) for xq in xqs]
    dts = [dt_ref[s] for s in seqs]
    cums = [sum(_dot(tril, p) for p in _split_bf16(pad_keys(dt * a), 3)) for dt in dts]
    cum_ts = [cum.T for cum in cums]
    cum_qs = [cum[:q, :] for cum in cums]
    xdts = [x * d for x, d in zip(xs, expand(dts))]
    chunk_decay = [jnp.exp(cum[qk - 1:qk, :]) for cum in cums]
    to_end = [jnp.exp(cum[qk - 1:qk, :] - cum_q) for cum, cum_q in zip(cums, cum_qs)]
    wide = expand(to_end + [jnp.exp(cum_q) for cum_q in cum_qs])
    xws = [pad_keys(xdt * w).astype(BF16) for xdt, w in zip(xdts, wide[:bb])]
    xdts = [pad_keys(xdt).astype(BF16) for xdt in xdts]
    ecums = wide[bb:]

    for g in range(N_GROUPS):
        cols = slice(g * GROUP_W, (g + 1) * GROUP_W)
        for s in seqs:
            bg = bms[s][:, g * D_STATE:(g + 1) * D_STATE]
            cg = cms[s][:, g * D_STATE:(g + 1) * D_STATE]
            cb = lax.dot_general(cg, bg, (((1,), (1,)), ((), ())), preferred_element_type=F32)
            xdt_g = xdts[s][:, cols]
            scores, keys = [], []
            for r in range(HEADS_PER_GROUP):
                hd = g * HEADS_PER_GROUP + r
                seg = jnp.where(causal, cum_qs[s][:, hd:hd + 1] - cum_ts[s][hd:hd + 1, :], -1e30)
                scores.append((cb * jnp.exp(seg)).astype(BF16))
                keys.append(jnp.where(key_head == r, xdt_g, jnp.zeros_like(xdt_g)))
            y_g = _dot(jnp.concatenate(scores, axis=1), jnp.concatenate(keys, axis=0))
            h_g = h_scr[s, cols, :]
            y_inter = lax.dot_general(cg, h_g.astype(BF16), (((1,), (1,)), ((), ())),
                                      preferred_element_type=F32)
            y_scr[s, :, cols] = y_g + y_inter * ecums[s][:, cols]
            s_g = lax.dot_general(xws[s][:, cols], bg, (((0,), (0,)), ((), ())), preferred_element_type=F32)
            decay = jnp.concatenate(
                [jnp.broadcast_to(chunk_decay[s][:, g * HEADS_PER_GROUP + r:g * HEADS_PER_GROUP + r + 1],
                                  (HEAD_DIM, D_STATE)) for r in range(HEADS_PER_GROUP)], axis=0)
            h_scr[s, cols, :] = decay * h_g + s_g

    for s in seqs:
        y = y_scr[s] + dsk_ref[...] * xs[s]
        y_ref[s] = _rms(y * zs_ref[s], gn_ref[...]).astype(y_ref.dtype)

    @pl.when(c == pl.num_programs(1) - 1)
    def _():
        ht_ref[...] = h_scr[...]


def _ssd(xbc, dt, zs, h0, alog, dsk, gn, e, *, q, bb):
    b, L, _ = xbc.shape
    seq = lambda w: pl.BlockSpec((bb, q, w), lambda i, c: (i, c, 0))
    full = lambda arr: pl.BlockSpec(arr.shape, lambda i, c: (0,) * arr.ndim)
    state = pl.BlockSpec((bb, D_B, D_STATE), lambda i, c: (i, 0, 0))
    return pl.pallas_call(
        functools.partial(_ssd_kernel, q=q, bb=bb),
        grid=(b // bb, L // q),
        in_specs=[seq(D_XBC), seq(LANES), seq(D_B), state, full(alog), full(dsk), full(gn), full(e)],
        out_specs=[seq(D_B), state],
        out_shape=[jax.ShapeDtypeStruct((b, L, D_B), BF16), jax.ShapeDtypeStruct((b, D_B, D_STATE), F32)],
        scratch_shapes=[pltpu.VMEM((bb, D_B, D_STATE), F32), pltpu.VMEM((bb, q, D_B), F32)],
        compiler_params=_params("parallel", "arbitrary"),
        name="ssd",
    )(xbc, dt, zs, h0, alog, dsk, gn, e)


def _layer(x, wts, conv_a, xbc_fn, ssd_fn, ffn_fn):
    norm = _norm_ring if x.shape[0] > 1024 else _norm
    h, dt = norm(x, wts["g_pre1"], wts["w_dt"], wts["dt_bias"])
    w_main = wts["w_main"]
    uglu = _proj(h, [w_main, w_main], [], _epi_glu, PROJ_COLS, F32, "in_glu", n=D_A, cols=[0, D_A])
    zs = _proj(h, [w_main], [], _epi_silu, PROJ_COLS, F32, "in_z", n=D_B, cols=[COL_Z])
    gates = _proj(h, [wts["w_g"]], [wts["b_gate"]], _epi_gate, PROJ_COLS, F32, "in_gates")

    ua, st_a = conv_a(uglu)
    xbc_act, st_b = xbc_fn(h)
    yb, st_h = ssd_fn(xbc_act, dt, zs)

    x1 = _merge(ua, yb, gates, x, wts["w_a_out"], wts["b_a_out"], wts["w_b_out"], wts["w_o"], wts["g_post1"])
    x2, st_f = ffn_fn(x1)
    return x2, st_a, st_b, st_h, st_f


def _seq_group(x, wts, st_a, st_b, st_h, st_f, tl, q):
    b, L, _ = x.shape

    def conv_a(uglu):
        ua, (ns,) = _conv_seq(uglu.reshape(b, L, D_A), st_a, wts["w_dw_a"], wts["b_dw_a"],
                              [wts["g_ln_a"], wts["b_ln_a"]], _post_ln_silu, width=CONV_A, ct=D_A,
                              tl=min(tl, 256), n_s=1, out_w=D_A, out_dtype=BF16, name="conv_a")
        return ua.reshape(b * L, D_A), ns

    def xbc_fn(h):
        return _proj_conv(h.reshape(b, L, D_MODEL), wts["w_main"], COL_XBC, st_b, wts["w_dw_b"], wts["b_dw_b"],
                          "in_xbc_conv")

    def ssd_fn(xbc_act, dt, zs):
        yb, ht = _ssd(xbc_act, dt.reshape(b, L, LANES), zs.reshape(b, L, D_B), st_h,
                      wts["a_log"], wts["d_skip"], wts["g_norm_b"], wts["expand"], q=q, bb=1)
        return yb.reshape(b * L, D_B), ht

    def ffn_fn(x1):
        x2, ns = _ffn(x1.reshape(b, L, D_MODEL), st_f, wts, tm=min(L, 512), sh=1)
        return x2.reshape(b * L, D_MODEL), ns

    x2, ns_a, ns_b, ns_h, ns_f = _layer(x.reshape(b * L, D_MODEL), wts, conv_a, xbc_fn, ssd_fn, ffn_fn)
    return x2.reshape(b, L, D_MODEL), ns_a, ns_b, ns_h, ns_f


def _step_group(x, wts, st_a, st_b, st_h, st_f):
    nb, L, _ = x.shape
    tmaj = lambda s: jnp.transpose(s, (1, 0, 2))
    lpad = -(-L // SUBLANES) * SUBLANES

    def conv_a(uglu):
        ua, (ns,) = _conv_slab(uglu.reshape(L, nb, D_A), tmaj(st_a), wts["w_dw_a"], wts["b_dw_a"],
                               [wts["g_ln_a"], wts["b_ln_a"]], _post_ln_silu, width=CONV_A, ct=D_A,
                               nbt=32, n_s=1, out_w=D_A, out_dtype=BF16, name="conv_a_step")
        return ua.reshape(L * nb, D_A), tmaj(ns)

    def bmaj_pad(v):
        return jnp.pad(tmaj(v), ((0, 0), (0, lpad - L), (0, 0)))

    def xbc_fn(h):
        xbc = _proj(h, [wts["w_main"]], [], _epi_id, PROJ_COLS, F32, "in_xbc", n=D_XBC, cols=[COL_XBC])
        act, (ns,) = _conv_slab(xbc.reshape(L, nb, D_XBC), tmaj(st_b), wts["w_dw_b"], wts["b_dw_b"], [],
                                _post_silu, width=CONV_B, ct=512, nbt=nb, n_s=1, out_w=D_XBC,
                                out_dtype=F32, name="conv_b_step")
        return act, tmaj(ns)

    def ssd_fn(xbc_act, dt, zs):
        yb, ht = _ssd(bmaj_pad(xbc_act), bmaj_pad(dt.reshape(L, nb, LANES)), bmaj_pad(zs.reshape(L, nb, D_B)),
                      st_h, wts["a_log"], wts["d_skip"], wts["g_norm_b"], wts["expand"], q=lpad, bb=8)
        return tmaj(yb[:, :L]).reshape(L * nb, D_B), ht

    def ffn_fn(x1):
        past = CONV_F - 1
        x2, ns = _ffn(x1.reshape(1, L * nb, D_MODEL), tmaj(st_f).reshape(1, past * nb, 2 * D_FF), wts,
                      tm=L * nb, sh=nb)
        return x2.reshape(L * nb, D_MODEL), tmaj(ns.reshape(past, nb, 2 * D_FF))

    x2, ns_a, ns_b, ns_h, ns_f = _layer(tmaj(x).reshape(L * nb, D_MODEL), wts, conv_a, xbc_fn, ssd_fn, ffn_fn)
    return tmaj(x2.reshape(L, nb, D_MODEL)), ns_a, ns_b, ns_h, ns_f


def _layer_weights(l, g_pre1, g_post1, w_in, b_gate, w_dw_a, b_dw_a, g_ln_a, b_ln_a, w_a_out, b_a_out,
                   w_dw_b, b_dw_b, dt_bias, a_log, d_skip, g_norm_b, w_b_out, w_o,
                   g_pre2, g_post2, w_up, w_dw_f, b_dw_f, w_down):
    row = lambda v: v[l].reshape(1, -1)
    lane_pad = lambda v: jnp.pad(v, ((0, 0), (0, LANES - v.shape[1])))
    w = w_in[l]
    head = jnp.arange(LANES, dtype=jnp.int32)[:, None]
    chan_head = (jnp.arange(D_B, dtype=jnp.int32) // HEAD_DIM)[None, :]
    expand = (head == chan_head).astype(BF16)
    return {
        "g_pre1": row(g_pre1), "g_post1": row(g_post1), "g_pre2": row(g_pre2), "g_post2": row(g_post2),
        "w_main": w[:, :COL_DT].astype(BF16),
        "w_dt": lane_pad(w[:, COL_DT:COL_GATE]).astype(BF16), "w_g": w[:, COL_GATE:].astype(BF16),
        "dt_bias": lane_pad(row(dt_bias)), "b_gate": row(b_gate),
        "w_dw_a": w_dw_a[l], "b_dw_a": row(b_dw_a), "g_ln_a": row(g_ln_a), "b_ln_a": row(b_ln_a),
        "w_a_out": w_a_out[l].astype(BF16), "b_a_out": row(b_a_out),
        "w_dw_b": w_dw_b[l], "b_dw_b": row(b_dw_b),
        "a_log": lane_pad(row(a_log)), "d_skip": jnp.repeat(d_skip[l], HEAD_DIM).reshape(1, D_B),
        "g_norm_b": row(g_norm_b), "expand": expand,
        "w_b_out": w_b_out[l].astype(BF16), "w_o": w_o[l].astype(BF16),
        "w_up": w_up[l].astype(BF16), "w_dw_f": w_dw_f[l], "b_dw_f": row(b_dw_f),
        "w_down": w_down[l].astype(BF16),
    }


def kernel(x_prompt, x_sample, state_conv_a, state_conv_b, state_ssm, state_conv_ffn, meta_tokens, g_pre1, g_post1, w_in, b_gate, w_dw_a, b_dw_a, g_ln_a, b_ln_a, w_a_out, b_a_out, w_dw_b, b_dw_b, dt_bias, a_log, d_skip, g_norm_b, w_b_out, w_o, g_pre2, g_post2, w_up, w_dw_f, b_dw_f, w_down):
    depth = w_in.shape[0]
    bp = x_prompt.shape[0]
    nb = x_sample.shape[0]
    xm = meta_tokens.astype(x_prompt.dtype)[None]
    xp, xs = x_prompt, x_sample
    pa, pb, ph, pf = [], [], [], []
    sa, sb, sh, sf = [], [], [], []
    for l in range(depth):
        wts = _layer_weights(l, g_pre1, g_post1, w_in, b_gate, w_dw_a, b_dw_a, g_ln_a, b_ln_a, w_a_out, b_a_out,
                             w_dw_b, b_dw_b, dt_bias, a_log, d_skip, g_norm_b, w_b_out, w_o,
                             g_pre2, g_post2, w_up, w_dw_f, b_dw_f, w_down)
        xm, m_a, m_b, m_h, m_f = _seq_group(
            xm, wts, jnp.zeros((1, CONV_A - 1, D_A), F32), jnp.zeros((1, CONV_B - 1, D_XBC), F32),
            jnp.zeros((1, D_B, D_STATE), F32), jnp.zeros((1, CONV_F - 1, 2 * D_FF), F32), tl=N_META, q=SUBLANES)
        rep = lambda s: jnp.broadcast_to(s, (bp,) + s.shape[1:])
        xp, c_a, c_b, c_h, c_f = _seq_group(xp, wts, rep(m_a), rep(m_b), rep(m_h), rep(m_f), tl=512, q=SSD_KEYS)
        pa.append(c_a); pb.append(c_b); ph.append(c_h.reshape(bp, N_HEADS, HEAD_DIM, D_STATE)); pf.append(c_f)
        xs, d_a, d_b, d_h, d_f = _step_group(
            xs, wts, state_conv_a[l], state_conv_b[l], state_ssm[l].reshape(nb, D_B, D_STATE), state_conv_ffn[l])
        sa.append(d_a); sb.append(d_b); sh.append(d_h.reshape(nb, N_HEADS, HEAD_DIM, D_STATE)); sf.append(d_f)
    return (xp, xs, jnp.stack(pa), jnp.stack(pb), jnp.stack(ph), jnp.stack(pf),
            jnp.stack(sa), jnp.stack(sb), jnp.stack(sh), jnp.stack(sf))
```
